```python
import jax
import jax.numpy as jnp
from jax import lax
import numpy as np

D_MODEL = 4096
BATCH = 4
SEQ = 2048
DEPTH = 2
DEC_BATCH = 8
DEC_SEQ = 4
PAST_LEN = 16384
PAGE_SIZE = 128

HEAD_DIM = 64
RWKV_HEADS = 24
RWKV_DIM = RWKV_HEADS * HEAD_DIM
ATT_HEADS = 24
ATT_DIM = ATT_HEADS * HEAD_DIM
CONV_DIM = D_MODEL - RWKV_DIM - ATT_DIM
MIX_DIM = RWKV_DIM + ATT_DIM + CONV_DIM
DECAY_LORA = 128
AAA_LORA = 128
GATE_LORA = 480
RWKV_PROJ = 3 * RWKV_DIM + DECAY_LORA + AAA_LORA + GATE_LORA
ATT_PROJ = 3 * ATT_DIM
CONV_PROJ = 2 * CONV_DIM
IN_PROJ = RWKV_PROJ + ATT_PROJ + CONV_PROJ
DILATED_PATTERNS = ((128, 1), (512, 4), (2048, 16))
MAX_WINDOW = 2048
BAND_BLOCK = 128
CONV_WIDTH = 31
D_FF = 11008
FFN_CONV_WIDTH = 3
ROPE_THETA = 10000.0
RMS_EPS = 1e-6
LN_EPS = 1e-5
GN_EPS = 64e-5
NEG_INF = -1e30

kernel_name = 'hybrid_rwkv7_dilated_swa_conformer_convffn_step'


def rmsnorm(x, g):
    x32 = x.astype(jnp.float32)
    y = x32 * lax.rsqrt(jnp.mean(x32 * x32, axis=-1, keepdims=True) + RMS_EPS)
    return (y * g.astype(jnp.float32)).astype(x.dtype)


def layernorm(x, g, b, eps):
    x32 = x.astype(jnp.float32)
    mu = jnp.mean(x32, axis=-1, keepdims=True)
    var = jnp.mean(jnp.square(x32 - mu), axis=-1, keepdims=True)
    return (x32 - mu) * lax.rsqrt(var + eps) * g.astype(jnp.float32) + b.astype(jnp.float32)


def rotary(x, pos):
    half = HEAD_DIM // 2
    inv_freq = ROPE_THETA ** (-jnp.arange(half, dtype=jnp.float32) * 2.0 / HEAD_DIM)
    ang = pos.astype(jnp.float32)[..., None] * inv_freq
    cos = jnp.cos(ang)[..., None, :]
    sin = jnp.sin(ang)[..., None, :]
    x32 = x.astype(jnp.float32)
    x1, x2 = x32[..., :half], x32[..., half:]
    return jnp.concatenate([x1 * cos - x2 * sin, x2 * cos + x1 * sin], axis=-1).astype(x.dtype)


def causal_dwconv(buf, u, w, b):
    K = w.shape[0]
    T = u.shape[1]
    ext = jnp.concatenate([buf.astype(u.dtype), u], axis=1)
    y = b
    for j in range(K):
        y = y + ext[:, j:j + T] * w[j]
    return y, ext[:, T:]


def rwkv7_time_mix(p, prev, wkv0, prm):
    N, T, _ = p.shape
    f32 = jnp.float32
    shifted = jnp.concatenate([prev.astype(p.dtype), p[:, :-1]], axis=1)
    xs = p + (shifted - p) * prm['mu']
    o1, o2, o3 = RWKV_DIM, 2 * RWKV_DIM, 3 * RWKV_DIM
    o4 = o3 + DECAY_LORA
    o5 = o4 + AAA_LORA
    xr, xk, xv, xw, xa, xg = jnp.split(xs, [o1, o2, o3, o4, o5], axis=-1)
    w_log = -jax.nn.softplus(-(prm['w0'] + jnp.tanh(xw) @ prm['w2']).astype(f32)) - 0.5
    decay = jnp.exp(-jnp.exp(w_log))
    a = jax.nn.sigmoid((prm['a0'] + xa @ prm['a2']).astype(f32))
    g = (jax.nn.sigmoid(xg) @ prm['g2']).astype(f32)

    def heads(t):
        return t.astype(f32).reshape(N, T, RWKV_HEADS, HEAD_DIM)

    def per_head(vec):
        return vec.astype(f32).reshape(RWKV_HEADS, HEAD_DIM)

    r, k_raw, v = heads(xr), heads(xk), heads(xv)
    decay, a = heads(decay), heads(a)
    kk = k_raw * per_head(prm['k_k'])
    kk = kk * lax.rsqrt(jnp.maximum(jnp.sum(kk * kk, axis=-1, keepdims=True), 1e-24))
    k = k_raw * (1.0 + (a - 1.0) * per_head(prm['k_a']))
    a_vec, b_vec = -kk, kk * a

    def step(state, inp):
        r_t, w_t, k_t, v_t, a_t, b_t = inp
        sa = jnp.einsum('nhvk,nhk->nhv', state, a_t)
        state = (state * w_t[:, :, None, :] + sa[..., None] * b_t[:, :, None, :]
                 + v_t[..., None] * k_t[:, :, None, :])
        return state, jnp.einsum('nhvk,nhk->nhv', state, r_t)

    seq = tuple(jnp.swapaxes(t, 0, 1) for t in (r, decay, k, v, a_vec, b_vec))
    state, ys = lax.scan(step, wkv0.astype(f32), seq)
    y = jnp.swapaxes(ys, 0, 1)
    mu = jnp.mean(y, axis=-1, keepdims=True)
    var = jnp.mean(jnp.square(y - mu), axis=-1, keepdims=True)
    y = (y - mu) * lax.rsqrt(var + GN_EPS) * per_head(prm['ln_w']) + per_head(prm['ln_b'])
    y = y + jnp.sum(r * k * prm['r_k'].astype(f32), axis=-1, keepdims=True) * v
    out = y.reshape(N, T, RWKV_DIM) * g
    return out.astype(p.dtype), p[:, -1:], state.astype(wkv0.dtype)


def dilated_band_attention(q, k, v, dilation, n_back):
    B, S, H, Dh = q.shape
    L = S // dilation
    nb = -(-L // BAND_BLOCK)
    pad = nb * BAND_BLOCK - L

    def blocks(t):
        t = t.reshape(B, L, dilation, H, Dh).transpose(0, 2, 1, 3, 4).reshape(B * dilation, L, H, Dh)
        t = jnp.pad(t, ((0, 0), (0, pad), (0, 0), (0, 0)))
        return t.reshape(B * dilation, nb, BAND_BLOCK, H, Dh)

    def band(t):
        prev = jnp.pad(t, ((0, 0), (1, 0), (0, 0), (0, 0), (0, 0)))[:, :-1]
        return jnp.concatenate([prev, t], axis=2)

    qb = blocks(q)
    kb = band(blocks(k))
    vb = band(blocks(v))
    s = jnp.einsum('bnqhd,bnkhd->bnhqk', qb, kb, preferred_element_type=jnp.float32) * (Dh ** -0.5)
    qi = jnp.arange(BAND_BLOCK)[:, None]
    ki = jnp.arange(2 * BAND_BLOCK)[None, :]
    dist = qi + BAND_BLOCK - ki
    key_idx = (jnp.arange(nb)[:, None, None] - 1) * BAND_BLOCK + ki[None]
    valid = (dist >= 0) & (dist <= n_back) & (key_idx >= 0)
    s = jnp.where(valid[None, :, None], s, NEG_INF)
    lse = jax.nn.logsumexp(s, axis=-1)
    pr = jnp.exp(s - lse[..., None])
    o = jnp.einsum('bnhqk,bnkhd->bnqhd', pr, vb.astype(jnp.float32))

    def unblock(t):
        t = t.reshape((B * dilation, nb * BAND_BLOCK) + t.shape[3:])[:, :L]
        t = t.reshape((B, dilation, L) + t.shape[2:])
        return jnp.swapaxes(t, 1, 2).reshape((B, S) + t.shape[3:])

    return unblock(o), unblock(jnp.swapaxes(lse, 2, 3))


def dilated_gather_attention(q, k_all, v_all, dilation, n_back):
    T = q.shape[1]
    M = k_all.shape[1] - T
    q_idx = M + jnp.arange(T)
    k_idx = q_idx[:, None] - dilation * jnp.arange(n_back + 1)[None, :]
    valid = k_idx >= 0
    k_idx = jnp.maximum(k_idx, 0)
    kg = k_all[:, k_idx]
    vg = v_all[:, k_idx]
    s = jnp.einsum('nthd,ntjhd->nthj', q, kg, preferred_element_type=jnp.float32) * (q.shape[-1] ** -0.5)
    s = jnp.where(valid[None, :, None, :], s, NEG_INF)
    lse = jax.nn.logsumexp(s, axis=-1)
    pr = jnp.exp(s - lse[..., None])
    o = jnp.einsum('nthj,ntjhd->nthd', pr, vg.astype(jnp.float32))
    return o, lse


def combine_dilations(outs, lses):
    wts = jax.nn.softmax(jnp.stack(lses, axis=0), axis=0)
    return jnp.einsum('pnth,pnthd->nthd', wts, jnp.stack(outs, axis=0))


def prompt_attend(q, k, v, win_buf):
    outs, lses = [], []
    for window, dil in DILATED_PATTERNS:
        o, l = dilated_band_attention(q, k, v, dil, window // dil)
        outs.append(o)
        lses.append(l)
    o = combine_dilations(outs, lses)
    S = k.shape[1]
    pad = ((0, 0), (max(win_buf - S, 0), 0), (0, 0), (0, 0))
    return o, jnp.pad(k, pad)[:, -win_buf:], jnp.pad(v, pad)[:, -win_buf:]


def sample_attend(q, k, v, k_buf, v_buf):
    T = q.shape[1]
    k_all = jnp.concatenate([k_buf.astype(k.dtype), k], axis=1)
    v_all = jnp.concatenate([v_buf.astype(v.dtype), v], axis=1)
    outs, lses = [], []
    for window, dil in DILATED_PATTERNS:
        o, l = dilated_gather_attention(q, k_all, v_all, dil, window // dil)
        outs.append(o)
        lses.append(l)
    return combine_dilations(outs, lses), k_all[:, T:], v_all[:, T:]


def conformer_conv(p, buf, prm):
    u = p[..., :CONV_DIM] * jax.nn.sigmoid(p[..., CONV_DIM:])
    c, new_buf = causal_dwconv(buf, u, prm['conv_w'], prm['conv_b'])
    c = layernorm(c, prm['conv_ln_w'], prm['conv_ln_b'], LN_EPS)
    return jax.nn.silu(c).astype(p.dtype), new_buf


def trunk_layer(x, pos, carry, attend, prm):
    prev_shift, wkv0, conv_buf, ffn_buf = carry
    N, T, _ = x.shape
    h = rmsnorm(x, prm['norm_mix'])
    proj = h @ prm['w_in']
    p_rwkv, p_att, p_conv = jnp.split(proj, [RWKV_PROJ, RWKV_PROJ + ATT_PROJ], axis=-1)
    o_rwkv, shift_new, wkv_new = rwkv7_time_mix(p_rwkv, prev_shift, wkv0, prm)
    q, k, v = (t.reshape(N, T, ATT_HEADS, HEAD_DIM) for t in jnp.split(p_att, 3, axis=-1))
    q = rotary(q, pos)
    k = rotary(k, pos)
    o_att, k_state, v_state = attend(q, k, v)
    o_conv, conv_new = conformer_conv(p_conv, conv_buf, prm)
    mixed = jnp.concatenate([o_rwkv, o_att.reshape(N, T, ATT_DIM).astype(x.dtype), o_conv], axis=-1)
    x = x + mixed @ prm['w_out']
    u = rmsnorm(x, prm['norm_ffn']) @ prm['w_up']
    u_c, ffn_new = causal_dwconv(ffn_buf, u, prm['ffn_conv_w'], prm['ffn_conv_b'])
    gate, val = jnp.split(u_c, 2, axis=-1)
    x = x + (jax.nn.silu(gate) * val) @ prm['w_down']
    return x, (shift_new, wkv_new, k_state, v_state, conv_new, ffn_new)


def setup_inputs(seed: int = 0) -> dict:
    key = jax.random.key(seed)
    ks = jax.random.split(key, 32)
    f32 = jnp.float32

    def nrm(k, shape, scale):
        return jax.random.normal(k, shape, f32) * scale

    win_buf = min(MAX_WINDOW, PAST_LEN)
    L = DEPTH
    return {
        'x_prompt': nrm(ks[0], (BATCH, SEQ, D_MODEL), 1.0),
        'x_sample': nrm(ks[1], (DEC_BATCH, DEC_SEQ, D_MODEL), 1.0),
        'state_rwkv_shift': nrm(ks[2], (L, DEC_BATCH, 1, RWKV_PROJ), 1.0),
        'state_rwkv_wkv': nrm(ks[3], (L, DEC_BATCH, RWKV_HEADS, HEAD_DIM, HEAD_DIM), 0.3),
        'state_attn_k': nrm(ks[4], (L, DEC_BATCH, win_buf, ATT_HEADS, HEAD_DIM), 1.0),
        'state_attn_v': nrm(ks[5], (L, DEC_BATCH, win_buf, ATT_HEADS, HEAD_DIM), 1.0),
        'state_conv': nrm(ks[6], (L, DEC_BATCH, CONV_WIDTH - 1, CONV_DIM), 0.5),
        'state_ffn_conv': nrm(ks[7], (L, DEC_BATCH, FFN_CONV_WIDTH - 1, 2 * D_FF), 1.0),
        'pos_sample': jnp.broadcast_to(PAST_LEN + jnp.arange(DEC_SEQ, dtype=jnp.int32), (DEC_BATCH, DEC_SEQ)).astype(jnp.int32),
        'norm_mix': 1.0 + nrm(ks[8], (L, D_MODEL), 0.02),
        'w_in': nrm(ks[9], (L, D_MODEL, IN_PROJ), D_MODEL ** -0.5),
        'rwkv_mu': jax.random.uniform(ks[10], (L, RWKV_PROJ), f32),
        'rwkv_w0': nrm(ks[11], (L, RWKV_DIM), 0.5),
        'rwkv_w2': nrm(ks[12], (L, DECAY_LORA, RWKV_DIM), DECAY_LORA ** -0.5),
        'rwkv_a0': nrm(ks[13], (L, RWKV_DIM), 0.1),
        'rwkv_a2': nrm(ks[14], (L, AAA_LORA, RWKV_DIM), AAA_LORA ** -0.5),
        'rwkv_g2': nrm(ks[15], (L, GATE_LORA, RWKV_DIM), GATE_LORA ** -0.5),
        'rwkv_k_k': 0.85 + nrm(ks[16], (L, RWKV_DIM), 0.05),
        'rwkv_k_a': 1.0 + nrm(ks[17], (L, RWKV_DIM), 0.05),
        'rwkv_r_k': -0.04 + nrm(ks[18], (L, RWKV_HEADS, HEAD_DIM), 0.1),
        'rwkv_ln_w': 1.0 + nrm(ks[19], (L, RWKV_DIM), 0.02),
        'rwkv_ln_b': nrm(ks[20], (L, RWKV_DIM), 0.01),
        'conv_w': nrm(ks[21], (L, CONV_WIDTH, CONV_DIM), CONV_WIDTH ** -0.5),
        'conv_b': nrm(ks[22], (L, CONV_DIM), 0.01),
        'conv_ln_w': 1.0 + nrm(ks[23], (L, CONV_DIM), 0.02),
        'conv_ln_b': nrm(ks[24], (L, CONV_DIM), 0.01),
        'w_out': nrm(ks[25], (L, MIX_DIM, D_MODEL), MIX_DIM ** -0.5),
        'norm_ffn': 1.0 + nrm(ks[26], (L, D_MODEL), 0.02),
        'w_up': nrm(ks[27], (L, D_MODEL, 2 * D_FF), D_MODEL ** -0.5),
        'ffn_conv_w': nrm(ks[28], (L, FFN_CONV_WIDTH, 2 * D_FF), FFN_CONV_WIDTH ** -0.5),
        'ffn_conv_b': nrm(ks[29], (L, 2 * D_FF), 0.01),
        'w_down': nrm(ks[30], (L, D_FF, D_MODEL), D_FF ** -0.5),
        'norm_final': 1.0 + nrm(ks[31], (D_MODEL,), 0.02),
    }


def reference(x_prompt, x_sample, state_rwkv_shift, state_rwkv_wkv, state_attn_k, state_attn_v,
              state_conv, state_ffn_conv, pos_sample,
              norm_mix, w_in, rwkv_mu, rwkv_w0, rwkv_w2, rwkv_a0, rwkv_a2, rwkv_g2, rwkv_k_k,
              rwkv_k_a, rwkv_r_k, rwkv_ln_w, rwkv_ln_b, conv_w, conv_b, conv_ln_w, conv_ln_b,
              w_out, norm_ffn, w_up, ffn_conv_w, ffn_conv_b, w_down, norm_final):
    B, S, _ = x_prompt.shape
    dt = x_prompt.dtype
    win_buf = state_attn_k.shape[2]
    pos_prompt = jnp.arange(S, dtype=jnp.int32)
    carry_prompt = (jnp.zeros((B, 1, RWKV_PROJ), dt),
                    jnp.zeros((B, RWKV_HEADS, HEAD_DIM, HEAD_DIM), dt),
                    jnp.zeros((B, CONV_WIDTH - 1, CONV_DIM), dt),
                    jnp.zeros((B, FFN_CONV_WIDTH - 1, 2 * D_FF), dt))
    xp, xs = x_prompt, x_sample
    new_p, new_s = [], []
    for l in range(DEPTH):
        prm = {
            'norm_mix': norm_mix[l], 'w_in': w_in[l], 'mu': rwkv_mu[l], 'w0': rwkv_w0[l],
            'w2': rwkv_w2[l], 'a0': rwkv_a0[l], 'a2': rwkv_a2[l], 'g2': rwkv_g2[l],
            'k_k': rwkv_k_k[l], 'k_a': rwkv_k_a[l], 'r_k': rwkv_r_k[l], 'ln_w': rwkv_ln_w[l],
            'ln_b': rwkv_ln_b[l], 'conv_w': conv_w[l], 'conv_b': conv_b[l],
            'conv_ln_w': conv_ln_w[l], 'conv_ln_b': conv_ln_b[l], 'w_out': w_out[l],
            'norm_ffn': norm_ffn[l], 'w_up': w_up[l], 'ffn_conv_w': ffn_conv_w[l],
            'ffn_conv_b': ffn_conv_b[l], 'w_down': w_down[l],
        }
        xp, st_p = trunk_layer(xp, pos_prompt, carry_prompt,
                               lambda q, k, v: prompt_attend(q, k, v, win_buf), prm)
        carry_sample = (state_rwkv_shift[l], state_rwkv_wkv[l], state_conv[l], state_ffn_conv[l])
        xs, st_s = trunk_layer(xs, pos_sample, carry_sample,
                               lambda q, k, v: sample_attend(q, k, v, state_attn_k[l], state_attn_v[l]), prm)
        new_p.append(st_p)
        new_s.append(st_s)

    def stack(states, i):
        return jnp.stack([st[i] for st in states], axis=0)

    y_prompt = rmsnorm(xp, norm_final)
    y_sample = rmsnorm(xs, norm_final)
    return (y_prompt, y_sample,
            stack(new_p, 0), stack(new_s, 0), stack(new_p, 1), stack(new_s, 1),
            stack(new_p, 2), stack(new_s, 2), stack(new_p, 3), stack(new_s, 3),
            stack(new_p, 4), stack(new_s, 4), stack(new_p, 5), stack(new_s, 5))
```

```python
import functools

import jax
import jax.numpy as jnp
from jax import lax
from jax.experimental import pallas as pl
from jax.experimental.pallas import tpu as pltpu

F32 = jnp.float32
BF16 = jnp.bfloat16

D_MODEL = 4096
HEAD_DIM = 64
HEADS = 24
MIX = HEADS * HEAD_DIM
CONV_DIM = 1024
DECAY_LORA = 128
AAA_LORA = 128
GATE_LORA = 480
GATE_PAD = 512
RWKV_PROJ = 3 * MIX + DECAY_LORA + AAA_LORA + GATE_LORA
RWKV_PAD = 3 * MIX + GATE_PAD + DECAY_LORA + AAA_LORA
ATT_PROJ = 3 * MIX
CONV_PROJ = 2 * CONV_DIM
CONV_WIDTH = 31
CONV_HALO = 32
D_FF = 11008
FFN_CONV_WIDTH = 3
FFN_HALO = 8
BAND = 128
DILATIONS = (1, 4, 16)
WINDOWS = (128, 512, 2048)
ROPE_THETA = 10000.0
RMS_EPS = 1e-6
LN_EPS = 1e-5
GN_EPS = 64e-5
NEG_INF = -1e30
CHUNK = 64

C_CONV = 0
C_ATT = CONV_PROJ
C_RWKV = C_ATT + ATT_PROJ
NP = 12288

VMEM_LIMIT = 56 * 1024 * 1024


def _cparams(sem):
    return pltpu.CompilerParams(dimension_semantics=sem, vmem_limit_bytes=VMEM_LIMIT)


def _norm_matmul_kernel(x_ref, g_ref, w_ref, o_ref, h_ref):
    @pl.when(pl.program_id(1) == 0)
    def _():
        x = x_ref[...]
        ms = jnp.mean(x * x, axis=-1, keepdims=True)
        h_ref[...] = (x * lax.rsqrt(ms + RMS_EPS) * g_ref[...]).astype(BF16)

    o_ref[...] = jnp.dot(h_ref[...], w_ref[...], preferred_element_type=F32)


def norm_matmul(x, g, w, tm, tn):
    M, K = x.shape
    N = w.shape[1]
    tm = min(tm, M)
    assert M % tm == 0 and N % tn == 0
    return pl.pallas_call(
        _norm_matmul_kernel,
        grid=(M // tm, N // tn),
        in_specs=[pl.BlockSpec((tm, K), lambda i, j: (i, 0)),
                  pl.BlockSpec((1, K), lambda i, j: (0, 0)),
                  pl.BlockSpec((K, tn), lambda i, j: (0, j))],
        out_specs=pl.BlockSpec((tm, tn), lambda i, j: (i, j)),
        out_shape=jax.ShapeDtypeStruct((M, N), F32),
        scratch_shapes=[pltpu.VMEM((tm, K), BF16)],
        compiler_params=_cparams(("parallel", "arbitrary")),
        name="norm_matmul",
    )(x, g, w)


def _matmul_res_kernel(a_ref, w_ref, r_ref, o_ref, acc_ref, *, nk):
    k = pl.program_id(2)

    @pl.when(k == 0)
    def _():
        acc_ref[...] = r_ref[...]

    acc_ref[...] += jnp.dot(a_ref[...], w_ref[...], preferred_element_type=F32)

    @pl.when(k == nk - 1)
    def _():
        o_ref[...] = acc_ref[...]


def matmul_res(a, w, res, tm, tn, tk):
    M, K = a.shape
    N = w.shape[1]
    tm = min(tm, M)
    assert M % tm == 0 and N % tn == 0 and K % tk == 0
    nk = K // tk
    return pl.pallas_call(
        functools.partial(_matmul_res_kernel, nk=nk),
        grid=(M // tm, N // tn, nk),
        in_specs=[pl.BlockSpec((tm, tk), lambda i, j, k: (i, k)),
                  pl.BlockSpec((tk, tn), lambda i, j, k: (k, j)),
                  pl.BlockSpec((tm, tn), lambda i, j, k: (i, j))],
        out_specs=pl.BlockSpec((tm, tn), lambda i, j, k: (i, j)),
        out_shape=jax.ShapeDtypeStruct((M, N), F32),
        scratch_shapes=[pltpu.VMEM((tm, tn), F32)],
        compiler_params=_cparams(("parallel", "parallel", "arbitrary")),
        name="matmul_res",
    )(a, w, res)


def _rmsnorm_kernel(x_ref, g_ref, o_ref):
    x = x_ref[...]
    ms = jnp.mean(x * x, axis=-1, keepdims=True)
    o_ref[...] = x * lax.rsqrt(ms + RMS_EPS) * g_ref[...]


def rmsnorm(x, g, tm):
    M, K = x.shape
    tm = min(tm, M)
    return pl.pallas_call(
        _rmsnorm_kernel,
        grid=(M // tm,),
        in_specs=[pl.BlockSpec((tm, K), lambda i: (i, 0)), pl.BlockSpec((1, K), lambda i: (0, 0))],
        out_specs=pl.BlockSpec((tm, K), lambda i: (i, 0)),
        out_shape=jax.ShapeDtypeStruct((M, K), F32),
        compiler_params=_cparams(("parallel",)),
        name="rmsnorm",
    )(x, g)


def _rope_kernel(q_ref, k_ref, cos_ref, sin_ref, qo_ref, ko_ref):
    cos = jnp.concatenate([cos_ref[0]] * 4, axis=1)
    sin = jnp.concatenate([sin_ref[0]] * 4, axis=1)
    lane = lax.broadcasted_iota(jnp.int32, cos.shape, 1)
    first_half = (lane % HEAD_DIM) < (HEAD_DIM // 2)

    def rot(x):
        w = x.shape[1]
        return jnp.where(first_half, pltpu.roll(x, w - HEAD_DIM // 2, 1), pltpu.roll(x, HEAD_DIM // 2, 1))

    q = q_ref[0]
    k = k_ref[0]
    qo_ref[0] = q * cos + rot(q) * sin
    ko_ref[0] = k * cos + rot(k) * sin


def rope(proj3, cos_t, sin_t, tt):
    N, T, _ = proj3.shape
    tt = min(tt, T)
    nt = cos_t.shape[0]
    tab = (lambda n, i, j: (n, i, 0)) if nt > 1 else (lambda n, i, j: (0, i, 0))
    qb = C_ATT // 512
    kb = (C_ATT + MIX) // 512
    out = jax.ShapeDtypeStruct((N, T, MIX), F32)
    return pl.pallas_call(
        _rope_kernel,
        grid=(N, T // tt, MIX // 512),
        in_specs=[pl.BlockSpec((1, tt, 512), lambda n, i, j: (n, i, qb + j)),
                  pl.BlockSpec((1, tt, 512), lambda n, i, j: (n, i, kb + j)),
                  pl.BlockSpec((1, tt, 128), tab),
                  pl.BlockSpec((1, tt, 128), tab)],
        out_specs=[pl.BlockSpec((1, tt, 512), lambda n, i, j: (n, i, j)),
                   pl.BlockSpec((1, tt, 512), lambda n, i, j: (n, i, j))],
        out_shape=[out, out],
        compiler_params=_cparams(("parallel", "parallel", "parallel")),
        name="rope",
    )(proj3, proj3, cos_t, sin_t)


def _head0_mask(shape):
    return lax.broadcasted_iota(jnp.int32, shape, 1) % (2 * HEAD_DIM) < HEAD_DIM


def _band_attn_kernel(*refs, first, last):
    q_ref, kc_ref, kp_ref, vc_ref, vp_ref = refs[:5]
    if first:
        outs = refs[5:]
    else:
        acc_in, m_in, l_in = refs[5:8]
        outs = refs[8:]
    i = pl.program_id(2)
    qi = lax.broadcasted_iota(jnp.int32, (BAND, 2 * BAND), 0)
    ki = lax.broadcasted_iota(jnp.int32, (BAND, 2 * BAND), 1)
    dist = qi + BAND - ki
    valid = (dist >= 0) & (dist <= BAND) & ((ki >= BAND) | (i > 0))
    h0 = _head0_mask((BAND, 2 * HEAD_DIM))
    scale = HEAD_DIM ** -0.5
    for p in range(4):
        sl = slice(128 * p, 128 * (p + 1))
        qp = q_ref[0, :, sl]
        kk = jnp.concatenate([kp_ref[0, :, sl], kc_ref[0, :, sl]], axis=0).astype(BF16)
        vv = jnp.concatenate([vp_ref[0, :, sl], vc_ref[0, :, sl]], axis=0).astype(BF16)
        ms, ls, os_ = [], [], []
        for h in range(2):
            qh = jnp.where(h0 if h == 0 else ~h0, qp, 0.0).astype(BF16)
            s = lax.dot_general(qh, kk, (((1,), (1,)), ((), ())), preferred_element_type=F32) * scale
            s = jnp.where(valid, s, NEG_INF)
            m = jnp.max(s, axis=-1, keepdims=True)
            e = jnp.exp(s - m)
            ls.append(jnp.sum(e, axis=-1, keepdims=True))
            ms.append(m)
            os_.append(jnp.dot(e.astype(BF16), vv, preferred_element_type=F32))
        m_c = jnp.where(h0, ms[0], ms[1])
        l_c = jnp.where(h0, ls[0], ls[1])
        o_c = jnp.where(h0, os_[0], os_[1])
        if first:
            acc, m_n, l_n = o_c, m_c, l_c
        else:
            m_p = m_in[0, :, sl]
            m_n = jnp.maximum(m_p, m_c)
            a_p = jnp.exp(m_p - m_n)
            a_c = jnp.exp(m_c - m_n)
            acc = acc_in[0, :, sl] * a_p + o_c * a_c
            l_n = l_in[0, :, sl] * a_p + l_c * a_c
        if last:
            outs[0][0, :, sl] = (acc / l_n).astype(BF16)
        else:
            outs[0][0, :, sl] = acc
            outs[1][0, :, sl] = m_n
            outs[2][0, :, sl] = l_n


def band_attention(q_rot, k_rot, proj3, dil, carry):
    N, S, _ = q_rot.shape
    L = S // dil
    assert L % BAND == 0
    nb = L // BAND
    first = carry is None
    last = dil == DILATIONS[-1]
    qv = q_rot.reshape(N, L, dil * MIX)
    kv = k_rot.reshape(N, L, dil * MIX)
    vv = proj3.reshape(N, L, dil * NP)
    vb = (C_ATT + 2 * MIX) // 512
    npb = NP // 512
    cur = lambda n, c, i: (n, i, c)
    prev = lambda n, c, i: (n, jnp.maximum(i - 1, 0), c)
    vcur = lambda n, c, i: (n, i, (c // 3) * npb + vb + c % 3)
    vprev = lambda n, c, i: (n, jnp.maximum(i - 1, 0), (c // 3) * npb + vb + c % 3)
    blk = (1, BAND, 512)
    in_specs = [pl.BlockSpec(blk, cur), pl.BlockSpec(blk, cur), pl.BlockSpec(blk, prev),
                pl.BlockSpec(blk, vcur), pl.BlockSpec(blk, vprev)]
    args = [qv, kv, kv, vv, vv]
    if not first:
        in_specs += [pl.BlockSpec(blk, cur)] * 3
        args += [c.reshape(N, L, dil * MIX) for c in carry]
    if last:
        out_shape = [jax.ShapeDtypeStruct((N, L, dil * MIX), BF16)]
    else:
        out_shape = [jax.ShapeDtypeStruct((N, L, dil * MIX), F32)] * 3
    outs = pl.pallas_call(
        functools.partial(_band_attn_kernel, first=first, last=last),
        grid=(N, dil * 3, nb),
        in_specs=in_specs,
        out_specs=[pl.BlockSpec(blk, cur)] * len(out_shape),
        out_shape=out_shape,
        compiler_params=_cparams(("parallel", "parallel", "parallel")),
        name=f"band_attention_d{dil}",
    )(*args)
    return [o.reshape(N, S, MIX) for o in outs]


def prompt_attention(q_rot, k_rot, proj3):
    carry = None
    for dil in DILATIONS:
        carry = band_attention(q_rot, k_rot, proj3, dil, carry)
    return carry[0]


def _sample_attn_kernel(q_ref, kn_ref, vn_ref, kb_ref, vb_ref, o_ref, ks_ref, vs_ref, *, t_valid):
    TQ = q_ref.shape[1]
    M = kb_ref.shape[1]
    TN = kn_ref.shape[1]
    h0 = _head0_mask((TQ, 2 * HEAD_DIM))
    scale = HEAD_DIM ** -0.5

    def counts(delta, in_range):
        c = jnp.zeros(delta.shape, F32)
        for win, dil in zip(WINDOWS, DILATIONS):
            c = c + jnp.where((delta % dil == 0) & (delta <= win) & in_range, 1.0, 0.0)
        return c

    tq = lax.broadcasted_iota(jnp.int32, (TQ, M), 0)
    kb_i = lax.broadcasted_iota(jnp.int32, (TQ, M), 1)
    d_buf = M + tq - kb_i
    c_buf = counts(d_buf, d_buf >= 0)
    tq2 = lax.broadcasted_iota(jnp.int32, (TQ, TN), 0)
    tn2 = lax.broadcasted_iota(jnp.int32, (TQ, TN), 1)
    d_new = tq2 - tn2
    c_new = counts(d_new, (d_new >= 0) & (tn2 < t_valid))

    row8 = lax.broadcasted_iota(jnp.int32, (8, 2 * HEAD_DIM), 0)
    for p in range(4):
        sl = slice(128 * p, 128 * (p + 1))
        qp = q_ref[0, :, sl]
        kb = kb_ref[0, :, sl]
        vb = vb_ref[0, :, sl]
        kn = kn_ref[0, :, sl]
        vn = vn_ref[0, :, sl]
        kb16, vb16, kn16, vn16 = kb.astype(BF16), vb.astype(BF16), kn.astype(BF16), vn.astype(BF16)
        outs = []
        for h in range(2):
            qh = jnp.where(h0 if h == 0 else ~h0, qp, 0.0).astype(BF16)
            s_b = lax.dot_general(qh, kb16, (((1,), (1,)), ((), ())), preferred_element_type=F32) * scale
            s_n = lax.dot_general(qh, kn16, (((1,), (1,)), ((), ())), preferred_element_type=F32) * scale
            s_b = jnp.where(c_buf > 0, s_b, NEG_INF)
            s_n = jnp.where(c_new > 0, s_n, NEG_INF)
            m = jnp.maximum(jnp.max(s_b, axis=-1, keepdims=True), jnp.max(s_n, axis=-1, keepdims=True))
            w_b = c_buf * jnp.exp(s_b - m)
            w_n = c_new * jnp.exp(s_n - m)
            l = jnp.sum(w_b, axis=-1, keepdims=True) + jnp.sum(w_n, axis=-1, keepdims=True)
            o = (jnp.dot(w_b.astype(BF16), vb16, preferred_element_type=F32)
                 + jnp.dot(w_n.astype(BF16), vn16, preferred_element_type=F32))
            outs.append(o / l)
        o_ref[0, :, sl] = jnp.where(h0, outs[0], outs[1]).astype(BF16)

        for src, new, dst in ((kb, kn, ks_ref), (vb, vn, vs_ref)):
            rolled = pltpu.roll(src, M - t_valid, 0)
            new_r = pltpu.roll(new[0:8], 8 - t_valid, 0)
            dst[0, 0:M - 8, sl] = rolled[0:M - 8]
            dst[0, M - 8:M, sl] = jnp.where(row8 >= 8 - t_valid, new_r, rolled[M - 8:M])


def sample_attention(q_rot, k_new, v_new, k_buf, v_buf, t_valid):
    N, TQ, _ = q_rot.shape
    M = k_buf.shape[1]
    TN = k_new.shape[1]
    assert t_valid <= 8 <= TQ
    blk = lambda r: pl.BlockSpec((1, r, 512), lambda n, j: (n, 0, j))
    return pl.pallas_call(
        functools.partial(_sample_attn_kernel, t_valid=t_valid),
        grid=(N, MIX // 512),
        in_specs=[blk(TQ), blk(TN), blk(TN), blk(M), blk(M)],
        out_specs=[blk(TQ), blk(M), blk(M)],
        out_shape=[jax.ShapeDtypeStruct((N, TQ, MIX), BF16),
                   jax.ShapeDtypeStruct((N, M, MIX), F32),
                   jax.ShapeDtypeStruct((N, M, MIX), F32)],
        compiler_params=_cparams(("parallel", "parallel")),
        name="sample_attention",
    )(q_rot, k_new, v_new, k_buf, v_buf)


def _glu_kernel(a_ref, g_ref, o_ref):
    g = g_ref[0]
    o_ref[0] = a_ref[0] * (1.0 / (1.0 + jnp.exp(-g)))


def glu(proj3, tt):
    N, T, _ = proj3.shape
    tt = min(tt, T)
    return pl.pallas_call(
        _glu_kernel,
        grid=(N, T // tt),
        in_specs=[pl.BlockSpec((1, tt, CONV_DIM), lambda n, i: (n, i, C_CONV // CONV_DIM)),
                  pl.BlockSpec((1, tt, CONV_DIM), lambda n, i: (n, i, C_CONV // CONV_DIM + 1))],
        out_specs=pl.BlockSpec((1, tt, CONV_DIM), lambda n, i: (n, i, 0)),
        out_shape=jax.ShapeDtypeStruct((N, T, CONV_DIM), F32),
        compiler_params=_cparams(("parallel", "parallel")),
        name="glu",
    )(proj3, proj3)


def _dwconv_kernel(u_ref, uh_ref, h0_ref, w_ref, b_ref, lw_ref, lb_ref, o_ref, ext_ref, y_ref, *, single_tile):
    tt = u_ref.shape[1]
    i = pl.program_id(1)

    @pl.when(i == 0)
    def _():
        ext_ref[0:CONV_HALO, :] = h0_ref[0]

    if not single_tile:
        @pl.when(i > 0)
        def _():
            ext_ref[0:CONV_HALO, :] = uh_ref[0]

    ext_ref[CONV_HALO:, :] = u_ref[0]
    lead = CONV_HALO - (CONV_WIDTH - 1)
    rows = min(16, tt)
    for r0 in range(0, tt, rows):
        for c0 in range(0, CONV_DIM, 512):
            acc = jnp.broadcast_to(b_ref[:, c0:c0 + 512], (rows, 512))
            for j in range(CONV_WIDTH):
                acc = acc + ext_ref[r0 + lead + j:r0 + lead + j + rows, c0:c0 + 512] * w_ref[j:j + 1, c0:c0 + 512]
            y_ref[r0:r0 + rows, c0:c0 + 512] = acc
    y = y_ref[...]
    mu = jnp.mean(y, axis=-1, keepdims=True)
    var = jnp.mean(jnp.square(y - mu), axis=-1, keepdims=True)
    c = (y - mu) * lax.rsqrt(var + LN_EPS) * lw_ref[...] + lb_ref[...]
    o_ref[0] = (c * (1.0 / (1.0 + jnp.exp(-c)))).astype(BF16)


def dwconv_ln_silu(u, halo0, w, b, lw, lb, tt):
    N, T, C = u.shape
    tt = min(tt, T)
    hb = tt // CONV_HALO if tt >= CONV_HALO else 1
    return pl.pallas_call(
        functools.partial(_dwconv_kernel, single_tile=(T == tt)),
        grid=(N, T // tt),
        in_specs=[pl.BlockSpec((1, tt, C), lambda n, i: (n, i, 0)),
                  pl.BlockSpec((1, min(CONV_HALO, T), C), lambda n, i: (n, jnp.maximum(i * hb - 1, 0), 0)),
                  pl.BlockSpec((1, CONV_HALO, C), lambda n, i: (n, 0, 0)),
                  pl.BlockSpec((32, C), lambda n, i: (0, 0)),
                  pl.BlockSpec((1, C), lambda n, i: (0, 0)),
                  pl.BlockSpec((1, C), lambda n, i: (0, 0)),
                  pl.BlockSpec((1, C), lambda n, i: (0, 0))],
        out_specs=pl.BlockSpec((1, tt, C), lambda n, i: (n, i, 0)),
        out_shape=jax.ShapeDtypeStruct((N, T, C), BF16),
        scratch_shapes=[pltpu.VMEM((CONV_HALO + tt, C), F32), pltpu.VMEM((tt, C), F32)],
        compiler_params=_cparams(("parallel", "arbitrary")),
        name="dwconv_ln_silu",
    )(u, u, halo0, w, b, lw, lb)


def _ffn_act_kernel(ug_ref, uv_ref, hg_ref, hv_ref, h0g_ref, h0v_ref, wg_ref, wv_ref, bg_ref, bv_ref, o_ref, ext_ref):
    tt = ug_ref.shape[1]
    i = pl.program_id(1)

    @pl.when(i == 0)
    def _():
        ext_ref[0, 0:FFN_HALO, :] = h0g_ref[0]
        ext_ref[1, 0:FFN_HALO, :] = h0v_ref[0]

    @pl.when(i > 0)
    def _():
        ext_ref[0, 0:FFN_HALO, :] = hg_ref[0]
        ext_ref[1, 0:FFN_HALO, :] = hv_ref[0]

    ext_ref[0, FFN_HALO:, :] = ug_ref[0]
    ext_ref[1, FFN_HALO:, :] = uv_ref[0]
    lead = FFN_HALO - (FFN_CONV_WIDTH - 1)
    rows = min(32, tt)
    for r0 in range(0, tt, rows):
        ys = []
        for s, w_ref, b_ref in ((0, wg_ref, bg_ref), (1, wv_ref, bv_ref)):
            acc = jnp.broadcast_to(b_ref[...], (rows, b_ref.shape[1]))
            for j in range(FFN_CONV_WIDTH):
                acc = acc + ext_ref[s, r0 + lead + j:r0 + lead + j + rows, :] * w_ref[j:j + 1, :]
            ys.append(acc)
        gate, val = ys
        o_ref[0, r0:r0 + rows, :] = (gate * (1.0 / (1.0 + jnp.exp(-gate))) * val).astype(BF16)


def ffn_act(u3, halo0, w, b, tt, tf):
    N, T, F2 = u3.shape
    F = F2 // 2
    tt = min(tt, T)
    nf = F // tf
    hb = tt // FFN_HALO
    g = lambda n, i, j: (n, i, j)
    v = lambda n, i, j: (n, i, j + nf)
    hg = lambda n, i, j: (n, jnp.maximum(i * hb - 1, 0), j)
    hv = lambda n, i, j: (n, jnp.maximum(i * hb - 1, 0), j + nf)
    return pl.pallas_call(
        _ffn_act_kernel,
        grid=(N, T // tt, nf),
        in_specs=[pl.BlockSpec((1, tt, tf), g), pl.BlockSpec((1, tt, tf), v),
                  pl.BlockSpec((1, FFN_HALO, tf), hg), pl.BlockSpec((1, FFN_HALO, tf), hv),
                  pl.BlockSpec((1, FFN_HALO, tf), lambda n, i, j: (n, 0, j)),
                  pl.BlockSpec((1, FFN_HALO, tf), lambda n, i, j: (n, 0, j + nf)),
                  pl.BlockSpec((8, tf), lambda n, i, j: (0, j)), pl.BlockSpec((8, tf), lambda n, i, j: (0, j + nf)),
                  pl.BlockSpec((1, tf), lambda n, i, j: (0, j)), pl.BlockSpec((1, tf), lambda n, i, j: (0, j + nf))],
        out_specs=pl.BlockSpec((1, tt, tf), g),
        out_shape=jax.ShapeDtypeStruct((N, T, F), BF16),
        scratch_shapes=[pltpu.VMEM((2, FFN_HALO + tt, tf), F32)],
        compiler_params=_cparams(("parallel", "parallel", "parallel")),
        name="ffn_act",
    )(u3, u3, u3, u3, halo0, halo0, w, w, b, b)


def _rwkv_kernel(r_ref, k_ref, v_ref, xg_ref, xwa_ref,
                 sr_ref, sk_ref, sv_ref, sxg_ref, sxwa_ref,
                 mur_ref, muk_ref, muv_ref, muxg_ref, muxwa_ref,
                 w0_ref, a0_ref, kk_ref, ka_ref, rk_ref, lnw_ref, lnb_ref,
                 w2_ref, a2_ref, g2_ref, s0_ref,
                 o_ref, so_ref,
                 S_ref, pr_ref, pk_ref, pv_ref, pxg_ref, pxwa_ref, *, t_valid, t_total):
    c = pl.program_id(2)
    Tc = r_ref.shape[1]

    @pl.when(c == 0)
    def _():
        S_ref[...] = s0_ref[0, 0]
        pr_ref[...] = sr_ref[0]
        pk_ref[...] = sk_ref[0]
        pv_ref[...] = sv_ref[0]
        pxg_ref[...] = sxg_ref[0]
        pxwa_ref[...] = sxwa_ref[0]

    def lerp(p_ref, prev_ref, mu_ref):
        p = p_ref[0]
        row = lax.broadcasted_iota(jnp.int32, p.shape, 0)
        shifted = jnp.where(row == 0, prev_ref[...], pltpu.roll(p, 1, 0))
        prev_ref[...] = p[Tc - 1:Tc, :]
        return p + (shifted - p) * mu_ref[...]

    xr = lerp(r_ref, pr_ref, mur_ref)
    xk = lerp(k_ref, pk_ref, muk_ref)
    xv = lerp(v_ref, pv_ref, muv_ref)
    xg = lerp(xg_ref, pxg_ref, muxg_ref)
    xwa = lerp(xwa_ref, pxwa_ref, muxwa_ref)
    xw = xwa[:, :DECAY_LORA]
    xa = xwa[:, DECAY_LORA:]

    h0 = _head0_mask((Tc, 2 * HEAD_DIM))

    def head_sum(x):
        s_0 = jnp.sum(jnp.where(h0, x, 0.0), axis=-1, keepdims=True)
        s_1 = jnp.sum(jnp.where(h0, 0.0, x), axis=-1, keepdims=True)
        return jnp.where(h0, s_0, s_1)

    z = w0_ref[...] + jnp.dot(jnp.tanh(xw).astype(BF16), w2_ref[...], preferred_element_type=F32)
    w_log = jnp.minimum(z, 0.0) - jnp.log(1.0 + jnp.exp(-jnp.abs(z))) - 0.5
    ld = -jnp.exp(w_log)
    za = a0_ref[...] + jnp.dot(xa.astype(BF16), a2_ref[...], preferred_element_type=F32)
    a = 1.0 / (1.0 + jnp.exp(-za))
    sg = 1.0 / (1.0 + jnp.exp(-xg))
    g = jnp.dot(sg.astype(BF16), g2_ref[...], preferred_element_type=F32)

    kk = xk * kk_ref[...]
    kk = kk * lax.rsqrt(jnp.maximum(head_sum(kk * kk), 1e-24))
    kmod = xk * (1.0 + (a - 1.0) * ka_ref[...])
    avec = -kk
    bvec = kk * a
    bonus = head_sum(xr * kmod * rk_ref[...]) * xv
    vval = xv

    if t_valid < t_total:
        row = lax.broadcasted_iota(jnp.int32, (Tc, 2 * HEAD_DIM), 0) + c * Tc
        ok = row < t_valid
        ld = jnp.where(ok, ld, 0.0)
        avec = jnp.where(ok, avec, 0.0)
        bvec = jnp.where(ok, bvec, 0.0)
        kmod_s = jnp.where(ok, kmod, 0.0)
        vval = jnp.where(ok, vval, 0.0)
    else:
        kmod_s = kmod

    C = CHUNK
    tri_r = lax.broadcasted_iota(jnp.int32, (C, C), 0)
    tri_c = lax.broadcasted_iota(jnp.int32, (C, C), 1)
    tril = jnp.where(tri_r >= tri_c, 1.0, 0.0).astype(F32)
    it = lax.broadcasted_iota(jnp.int32, (2 * C, 2 * C), 0) % C
    js = lax.broadcasted_iota(jnp.int32, (2 * C, 2 * C), 1) % C
    strict = it > js
    incl = it >= js
    eye = jnp.where(lax.broadcasted_iota(jnp.int32, (2 * C, 2 * C), 0)
                    == lax.broadcasted_iota(jnp.int32, (2 * C, 2 * C), 1), 1.0, 0.0).astype(F32)
    h0c = _head0_mask((C, 2 * HEAD_DIM))
    nt = (((1,), (1,)), ((), ()))
    tn = (((0,), (0,)), ((), ()))

    def stack(x):
        return jnp.concatenate([jnp.where(h0c, x, 0.0), jnp.where(h0c, 0.0, x)], axis=0)

    ys = []
    S = S_ref[...]
    for j in range(Tc // C):
        sl = slice(C * j, C * (j + 1))
        ldc = ld[sl]
        cs = jnp.dot(tril, ldc, precision=lax.Precision.HIGHEST, preferred_element_type=F32)
        tot = cs[C - 1:C, :]
        eg = jnp.exp(cs)
        egi = jnp.exp(-cs)
        ege = jnp.exp(cs - ldc)
        et = jnp.exp(tot - cs)
        gt = jnp.exp(tot)
        a_s = stack(avec[sl] * ege).astype(BF16)
        r_s = stack(xr[sl] * eg).astype(BF16)
        b_s = stack(bvec[sl] * egi).astype(BF16)
        k_s = stack(kmod_s[sl] * egi).astype(BF16)
        bb_s = stack(bvec[sl] * et).astype(BF16)
        kb_s = stack(kmod_s[sl] * et).astype(BF16)
        v_s = stack(vval[sl])
        v_s16 = v_s.astype(BF16)
        sc = lax.dot_general(jnp.concatenate([a_s, r_s], axis=0), jnp.concatenate([b_s, k_s], axis=0), nt,
                             preferred_element_type=F32)
        L = jnp.where(strict, sc[:2 * C, :2 * C], 0.0)
        AK = jnp.where(strict, sc[:2 * C, 2 * C:], 0.0)
        RB = jnp.where(incl, sc[2 * C:, :2 * C], 0.0)
        RK = jnp.where(incl, sc[2 * C:, 2 * C:], 0.0)
        Tm = eye + L
        P = L
        n_sq = C.bit_length() - 2
        for _ in range(n_sq):
            P16 = P.astype(BF16)
            P = jnp.dot(P16, P16, preferred_element_type=F32)
            Tm = Tm + jnp.dot(Tm.astype(BF16), P.astype(BF16), preferred_element_type=F32)
        S16 = S.astype(BF16)
        rhs = (lax.dot_general(a_s, S16, nt, preferred_element_type=F32)
               + jnp.dot(AK.astype(BF16), v_s16, preferred_element_type=F32))
        U = jnp.dot(Tm.astype(BF16), rhs.astype(BF16), preferred_element_type=F32)
        UV = jnp.concatenate([U.astype(BF16), v_s16], axis=0)
        Y = (lax.dot_general(r_s, S16, nt, preferred_element_type=F32)
             + jnp.dot(jnp.concatenate([RB, RK], axis=1).astype(BF16), UV, preferred_element_type=F32))
        ys.append(Y[:C] + Y[C:])
        S = S * gt + lax.dot_general(UV, jnp.concatenate([bb_s, kb_s], axis=0), tn, preferred_element_type=F32)
    S_ref[...] = S
    y = jnp.concatenate(ys, axis=0) if len(ys) > 1 else ys[0]

    mu = head_sum(y) * (1.0 / HEAD_DIM)
    var = head_sum(jnp.square(y - mu)) * (1.0 / HEAD_DIM)
    yn = (y - mu) * lax.rsqrt(var + GN_EPS) * lnw_ref[...] + lnb_ref[...]
    o_ref[0] = ((yn + bonus) * g).astype(BF16)

    @pl.when(c == pl.num_programs(2) - 1)
    def _():
        so_ref[0, 0] = S


def rwkv_time_mix(proj3, shift0, s0, prm, t_valid, tc):
    N, T, _ = proj3.shape
    tc = min(tc, T)
    assert T % tc == 0 and tc % CHUNK == 0
    rb = C_RWKV // 128
    pspec = lambda w, off: pl.BlockSpec((1, tc, w), lambda n, h, c: (n, c, off(h)))
    sspec = lambda w, off: pl.BlockSpec((1, 1, w), lambda n, h, c: (n, 0, off(h)))
    mspec = lambda w, off: pl.BlockSpec((1, w), lambda n, h, c: (0, off(h)))
    hspec = pl.BlockSpec((1, 128), lambda n, h, c: (0, h))
    xg_p = (C_RWKV + 3 * MIX) // GATE_PAD
    xwa_p = (C_RWKV + 3 * MIX + GATE_PAD) // 256
    xg_s = 3 * MIX // GATE_PAD
    xwa_s = (3 * MIX + GATE_PAD) // 256
    in_specs = [
        pspec(128, lambda h: rb + h), pspec(128, lambda h: rb + 12 + h), pspec(128, lambda h: rb + 24 + h),
        pspec(GATE_PAD, lambda h: xg_p), pspec(256, lambda h: xwa_p),
        sspec(128, lambda h: h), sspec(128, lambda h: 12 + h), sspec(128, lambda h: 24 + h),
        sspec(GATE_PAD, lambda h: xg_s), sspec(256, lambda h: xwa_s),
        mspec(128, lambda h: h), mspec(128, lambda h: 12 + h), mspec(128, lambda h: 24 + h),
        mspec(GATE_PAD, lambda h: xg_s), mspec(256, lambda h: xwa_s),
        hspec, hspec, hspec, hspec, hspec, hspec, hspec,
        pl.BlockSpec((DECAY_LORA, 128), lambda n, h, c: (0, h)),
        pl.BlockSpec((AAA_LORA, 128), lambda n, h, c: (0, h)),
        pl.BlockSpec((GATE_PAD, 128), lambda n, h, c: (0, h)),
        pl.BlockSpec((1, 1, 128, 128), lambda n, h, c: (n, h, 0, 0)),
    ]
    args = ([proj3] * 5 + [shift0] * 5 + [prm['mu']] * 5
            + [prm['w0'], prm['a0'], prm['k_k'], prm['k_a'], prm['r_k'], prm['ln_w'], prm['ln_b'],
               prm['w2'], prm['a2'], prm['g2'], s0])
    return pl.pallas_call(
        functools.partial(_rwkv_kernel, t_valid=t_valid, t_total=T),
        grid=(N, HEADS // 2, T // tc),
        in_specs=in_specs,
        out_specs=[pl.BlockSpec((1, tc, 128), lambda n, h, c: (n, c, h)),
                   pl.BlockSpec((1, 1, 128, 128), lambda n, h, c: (n, h, 0, 0))],
        out_shape=[jax.ShapeDtypeStruct((N, T, MIX), BF16),
                   jax.ShapeDtypeStruct((N, HEADS // 2, 128, 128), F32)],
        scratch_shapes=[pltpu.VMEM((128, 128), F32), pltpu.VMEM((1, 128), F32), pltpu.VMEM((1, 128), F32),
                        pltpu.VMEM((1, 128), F32), pltpu.VMEM((1, GATE_PAD), F32), pltpu.VMEM((1, 256), F32)],
        compiler_params=_cparams(("parallel", "parallel", "arbitrary")),
        name="rwkv7_time_mix",
    )(*args)


def _permute_rwkv_cols(t):
    pad = jnp.zeros(t.shape[:-1] + (GATE_PAD - GATE_LORA,), t.dtype)
    o = 3 * MIX
    return jnp.concatenate([t[..., :o], t[..., o + DECAY_LORA + AAA_LORA:], pad, t[..., o:o + DECAY_LORA + AAA_LORA]], axis=-1)


def _unpermute_rwkv_cols(t):
    o = 3 * MIX
    return jnp.concatenate([t[..., :o], t[..., o + GATE_PAD:], t[..., o:o + GATE_LORA]], axis=-1)


def _layer_params(l, norm_mix, w_in, rwkv_mu, rwkv_w0, rwkv_w2, rwkv_a0, rwkv_a2, rwkv_g2, rwkv_k_k, rwkv_k_a,
                  rwkv_r_k, rwkv_ln_w, rwkv_ln_b, conv_w, conv_b, conv_ln_w, conv_ln_b, w_out, norm_ffn, w_up,
                  ffn_conv_w, ffn_conv_b, w_down):
    wi = w_in[l]
    w_rwkv = _permute_rwkv_cols(wi[:, :RWKV_PROJ])
    w_att = wi[:, RWKV_PROJ:RWKV_PROJ + ATT_PROJ]
    w_conv = wi[:, RWKV_PROJ + ATT_PROJ:]
    tail = jnp.zeros((D_MODEL, NP - C_RWKV - RWKV_PAD), wi.dtype)
    row = lambda t: t.reshape(1, -1)
    return {
        'norm_mix': row(norm_mix[l]),
        'w_in': jnp.concatenate([w_conv, w_att, w_rwkv, tail], axis=1).astype(BF16),
        'mu': _permute_rwkv_cols(row(rwkv_mu[l])),
        'w0': row(rwkv_w0[l]), 'a0': row(rwkv_a0[l]), 'k_k': row(rwkv_k_k[l]), 'k_a': row(rwkv_k_a[l]),
        'r_k': row(rwkv_r_k[l]), 'ln_w': row(rwkv_ln_w[l]), 'ln_b': row(rwkv_ln_b[l]),
        'w2': rwkv_w2[l].astype(BF16), 'a2': rwkv_a2[l].astype(BF16),
        'g2': jnp.pad(rwkv_g2[l], ((0, GATE_PAD - GATE_LORA), (0, 0))).astype(BF16),
        'conv_w': jnp.pad(conv_w[l], ((0, 32 - CONV_WIDTH), (0, 0))), 'conv_b': row(conv_b[l]),
        'conv_ln_w': row(conv_ln_w[l]), 'conv_ln_b': row(conv_ln_b[l]),
        'w_out': w_out[l].astype(BF16),
        'norm_ffn': row(norm_ffn[l]),
        'w_up': w_up[l].astype(BF16),
        'ffn_conv_w': jnp.pad(ffn_conv_w[l], ((0, 8 - FFN_CONV_WIDTH), (0, 0))), 'ffn_conv_b': row(ffn_conv_b[l]),
        'w_down': w_down[l].astype(BF16),
    }


def _rope_tables(pos):
    half = HEAD_DIM // 2
    inv_freq = ROPE_THETA ** (-jnp.arange(half, dtype=F32) * 2.0 / HEAD_DIM)
    ang = pos.astype(F32)[..., None] * inv_freq
    cos, sin = jnp.cos(ang), jnp.sin(ang)
    return jnp.concatenate([cos, cos, cos, cos], axis=-1), jnp.concatenate([-sin, sin, -sin, sin], axis=-1)


def _state_to_blockdiag(s):
    N = s.shape[0]
    s = s.reshape(N, HEADS // 2, 2, HEAD_DIM, HEAD_DIM)
    z = jnp.zeros_like(s[:, :, 0])
    top = jnp.concatenate([s[:, :, 0], z], axis=-1)
    bot = jnp.concatenate([z, s[:, :, 1]], axis=-1)
    return jnp.concatenate([top, bot], axis=-2)


def _blockdiag_to_state(s):
    N = s.shape[0]
    return jnp.stack([s[:, :, :HEAD_DIM, :HEAD_DIM], s[:, :, HEAD_DIM:, HEAD_DIM:]], axis=2).reshape(
        N, HEADS, HEAD_DIM, HEAD_DIM)


def _last_rows(buf, u, t_valid):
    keep = buf.shape[1]
    if t_valid >= keep:
        return u[:, t_valid - keep:t_valid]
    return jnp.concatenate([buf[:, t_valid:], u[:, :t_valid]], axis=1)


def _trunk_layer(x2, N, T, t_valid, prm, cos_t, sin_t, carry, attn_bufs):
    shift0, wkv0, conv_buf, ffn_buf = carry
    proj = norm_matmul(x2, prm['norm_mix'], prm['w_in'], 512, 512)
    proj3 = proj.reshape(N, T, NP)

    if T % CHUNK:
        proj_r = jnp.pad(proj3, ((0, 0), (0, CHUNK - T), (0, 0)))
    else:
        proj_r = proj3
    o_rwkv, s_new = rwkv_time_mix(proj_r, _permute_rwkv_cols(shift0), _state_to_blockdiag(wkv0), prm, t_valid, 256)
    o_rwkv = o_rwkv[:, :T]
    shift_new = _unpermute_rwkv_cols(proj3[:, t_valid - 1:t_valid, C_RWKV:C_RWKV + RWKV_PAD])
    wkv_new = _blockdiag_to_state(s_new)

    q_rot, k_rot = rope(proj3, cos_t, sin_t, 256)
    v_new = proj3[:, :, C_ATT + 2 * MIX:C_ATT + 3 * MIX]
    if attn_bufs is None:
        o_att = prompt_attention(q_rot, k_rot, proj3)
        k_state, v_state = k_rot, v_new
    else:
        padr = ((0, 0), (0, 128 - T), (0, 0))
        o_att, k_state, v_state = sample_attention(q_rot, jnp.pad(k_rot, padr), jnp.pad(v_new, padr),
                                                   attn_bufs[0], attn_bufs[1], t_valid)

    u = glu(proj3, 256)
    halo0 = jnp.pad(conv_buf, ((0, 0), (CONV_HALO - (CONV_WIDTH - 1), 0), (0, 0)))
    o_conv = dwconv_ln_silu(u, halo0, prm['conv_w'], prm['conv_b'], prm['conv_ln_w'], prm['conv_ln_b'], 128)
    conv_new = _last_rows(conv_buf, u, t_valid)

    mixed = jnp.concatenate([o_rwkv, o_att, o_conv], axis=-1).reshape(N * T, D_MODEL)
    x2 = matmul_res(mixed, prm['w_out'], x2, 512, 512, D_MODEL)

    uf = norm_matmul(x2, prm['norm_ffn'], prm['w_up'], 512, 512)
    uf3 = uf.reshape(N, T, 2 * D_FF)
    fh0 = jnp.pad(ffn_buf, ((0, 0), (FFN_HALO - (FFN_CONV_WIDTH - 1), 0), (0, 0)))
    act = ffn_act(uf3, fh0, prm['ffn_conv_w'], prm['ffn_conv_b'], 512, 256)
    ffn_new = _last_rows(ffn_buf, uf3, t_valid)
    x2 = matmul_res(act.reshape(N * T, D_FF), prm['w_down'], x2, 512, 512, D_FF // 2)
    return x2, (shift_new, wkv_new, k_state, v_state, conv_new, ffn_new)


def kernel(x_prompt, x_sample, state_rwkv_shift, state_rwkv_wkv, state_attn_k, state_attn_v, state_conv, state_ffn_conv, pos_sample, norm_mix, w_in, rwkv_mu, rwkv_w0, rwkv_w2, rwkv_a0, rwkv_a2, rwkv_g2, rwkv_k_k, rwkv_k_a, rwkv_r_k, rwkv_ln_w, rwkv_ln_b, conv_w, conv_b, conv_ln_w, conv_ln_b, w_out, norm_ffn, w_up, ffn_conv_w, ffn_conv_b, w_down, norm_final):
    B, S, _ = x_prompt.shape
    NB, TS, _ = x_sample.shape
    depth = w_in.shape[0]
    win_buf = state_attn_k.shape[2]
    assert win_buf == S, "prompt key/value state is the whole rotated sequence"
    TSP = 8
    f32 = x_prompt.dtype

    xp = x_prompt.reshape(B * S, D_MODEL)
    xs = jnp.pad(x_sample, ((0, 0), (0, TSP - TS), (0, 0))).reshape(NB * TSP, D_MODEL)
    cos_p, sin_p = _rope_tables(jnp.arange(S, dtype=jnp.int32)[None])
    cos_s, sin_s = _rope_tables(jnp.pad(pos_sample, ((0, 0), (0, TSP - TS))))
    carry_p = (jnp.zeros((B, 1, RWKV_PROJ), f32), jnp.zeros((B, HEADS, HEAD_DIM, HEAD_DIM), f32),
               jnp.zeros((B, CONV_WIDTH - 1, CONV_DIM), f32), jnp.zeros((B, FFN_CONV_WIDTH - 1, 2 * D_FF), f32))
    new_p, new_s = [], []
    for l in range(depth):
        prm = _layer_params(l, norm_mix, w_in, rwkv_mu, rwkv_w0, rwkv_w2, rwkv_a0, rwkv_a2, rwkv_g2, rwkv_k_k,
                            rwkv_k_a, rwkv_r_k, rwkv_ln_w, rwkv_ln_b, conv_w, conv_b, conv_ln_w, conv_ln_b, w_out,
                            norm_ffn, w_up, ffn_conv_w, ffn_conv_b, w_down)
        xp, st_p = _trunk_layer(xp, B, S, S, prm, cos_p, sin_p, carry_p, None)
        carry_s = (state_rwkv_shift[l], state_rwkv_wkv[l], state_conv[l], state_ffn_conv[l])
        bufs = (state_attn_k[l].reshape(NB, win_buf, MIX), state_attn_v[l].reshape(NB, win_buf, MIX))
        xs, st_s = _trunk_layer(xs, NB, TSP, TS, prm, cos_s, sin_s, carry_s, bufs)
        new_p.append(st_p)
        new_s.append(st_s)

    g = norm_final.reshape(1, D_MODEL)
    y_prompt = rmsnorm(xp, g, 512).reshape(B, S, D_MODEL)
    y_sample = rmsnorm(xs, g, 512).reshape(NB, TSP, D_MODEL)[:, :TS]

    def stack(states, i, shape=None):
        t = jnp.stack([st[i] for st in states], axis=0)
        return t if shape is None else t.reshape(shape)

    kv_p = (depth, B, win_buf, HEADS, HEAD_DIM)
    kv_s = (depth, NB, win_buf, HEADS, HEAD_DIM)
    return (y_prompt, y_sample,
            stack(new_p, 0), stack(new_s, 0), stack(new_p, 1), stack(new_s, 1),
            stack(new_p, 2, kv_p), stack(new_s, 2, kv_s), stack(new_p, 3, kv_p), stack(new_s, 3, kv_s),
            stack(new_p, 4), stack(new_s, 4), stack(new_p, 5), stack(new_s, 5))
```

```python
import functools

import jax
import jax.numpy as jnp
from jax import lax
from jax.experimental import pallas as pl
from jax.experimental.pallas import tpu as pltpu

F32 = jnp.float32
BF16 = jnp.bfloat16

D_MODEL = 4096
HEAD_DIM = 64
HEADS = 24
MIX = HEADS * HEAD_DIM
CONV_DIM = 1024
DECAY_LORA = 128
AAA_LORA = 128
GATE_LORA = 480
GATE_PAD = 512
RWKV_PROJ = 3 * MIX + DECAY_LORA + AAA_LORA + GATE_LORA
RWKV_PAD = 3 * MIX + GATE_PAD + DECAY_LORA + AAA_LORA
ATT_PROJ = 3 * MIX
CONV_PROJ = 2 * CONV_DIM
CONV_WIDTH = 31
CONV_HALO = 32
D_FF = 11008
FFN_CONV_WIDTH = 3
FFN_HALO = 8
BAND = 128
DILATIONS = (1, 4, 16)
WINDOWS = (128, 512, 2048)
ROPE_THETA = 10000.0
RMS_EPS = 1e-6
LN_EPS = 1e-5
GN_EPS = 64e-5
NEG_INF = -1e30
CHUNK = 64

C_CONV = 0
C_ATT = CONV_PROJ
C_RWKV = C_ATT + ATT_PROJ
NP = 12288
M_RWKV, M_ATT, M_CONV = 0, MIX, 2 * MIX

VMEM_LIMIT = 56 * 1024 * 1024

NT = (((1,), (1,)), ((), ()))
TN = (((0,), (0,)), ((), ()))


def _cparams(sem):
    return pltpu.CompilerParams(dimension_semantics=sem, vmem_limit_bytes=VMEM_LIMIT)


def _sigmoid(x):
    return 1.0 / (1.0 + jnp.exp(-x))


def _head0_mask(shape):
    return lax.broadcasted_iota(jnp.int32, shape, 1) % (2 * HEAD_DIM) < HEAD_DIM


def _norm_matmul_kernel(x_ref, g_ref, w_ref, o_ref, h_ref):
    @pl.when(pl.program_id(1) == 0)
    def _():
        x = x_ref[...]
        ms = jnp.mean(x * x, axis=-1, keepdims=True)
        h_ref[...] = (x * lax.rsqrt(ms + RMS_EPS) * g_ref[...]).astype(BF16)

    o_ref[...] = jnp.dot(h_ref[...], w_ref[...], preferred_element_type=F32)


def norm_matmul(x, g, w, tm, tn):
    M, K = x.shape
    N = w.shape[1]
    tm = min(tm, M)
    assert M % tm == 0 and N % tn == 0
    return pl.pallas_call(
        _norm_matmul_kernel,
        grid=(M // tm, N // tn),
        in_specs=[pl.BlockSpec((tm, K), lambda i, j: (i, 0)),
                  pl.BlockSpec((1, K), lambda i, j: (0, 0)),
                  pl.BlockSpec((K, tn), lambda i, j: (0, j))],
        out_specs=pl.BlockSpec((tm, tn), lambda i, j: (i, j)),
        out_shape=jax.ShapeDtypeStruct((M, N), F32),
        scratch_shapes=[pltpu.VMEM((tm, K), BF16)],
        compiler_params=_cparams(("parallel", "arbitrary")),
        name="norm_matmul",
    )(x, g, w)


def _matmul_res_kernel(a_ref, w_ref, r_ref, o_ref, acc_ref, *, nk):
    k = pl.program_id(2)

    @pl.when(k == 0)
    def _():
        acc_ref[...] = r_ref[...]

    acc_ref[...] += jnp.dot(a_ref[...], w_ref[...], preferred_element_type=F32)

    @pl.when(k == nk - 1)
    def _():
        o_ref[...] = acc_ref[...]


def matmul_res(a, w, res, tm, tn, tk):
    M, K = a.shape
    N = w.shape[1]
    tm = min(tm, M)
    assert M % tm == 0 and N % tn == 0 and K % tk == 0
    nk = K // tk
    return pl.pallas_call(
        functools.partial(_matmul_res_kernel, nk=nk),
        grid=(M // tm, N // tn, nk),
        in_specs=[pl.BlockSpec((tm, tk), lambda i, j, k: (i, k)),
                  pl.BlockSpec((tk, tn), lambda i, j, k: (k, j)),
                  pl.BlockSpec((tm, tn), lambda i, j, k: (i, j))],
        out_specs=pl.BlockSpec((tm, tn), lambda i, j, k: (i, j)),
        out_shape=jax.ShapeDtypeStruct((M, N), F32),
        scratch_shapes=[pltpu.VMEM((tm, tn), F32)],
        compiler_params=_cparams(("parallel", "parallel", "arbitrary")),
        name="matmul_res",
    )(a, w, res)


def _rmsnorm_kernel(x_ref, g_ref, o_ref):
    x = x_ref[...]
    ms = jnp.mean(x * x, axis=-1, keepdims=True)
    o_ref[...] = x * lax.rsqrt(ms + RMS_EPS) * g_ref[...]


def rmsnorm(x, g, tm):
    M, K = x.shape
    tm = min(tm, M)
    return pl.pallas_call(
        _rmsnorm_kernel,
        grid=(M // tm,),
        in_specs=[pl.BlockSpec((tm, K), lambda i: (i, 0)), pl.BlockSpec((1, K), lambda i: (0, 0))],
        out_specs=pl.BlockSpec((tm, K), lambda i: (i, 0)),
        out_shape=jax.ShapeDtypeStruct((M, K), F32),
        compiler_params=_cparams(("parallel",)),
        name="rmsnorm",
    )(x, g)


def _rot_half(x, first_half):
    w = x.shape[1]
    return jnp.where(first_half, pltpu.roll(x, w - HEAD_DIM // 2, 1), pltpu.roll(x, HEAD_DIM // 2, 1))


def _rope_kernel(q_ref, k_ref, cos_ref, sin_ref, qo_ref, ko_ref):
    cos = jnp.concatenate([cos_ref[0]] * 4, axis=1)
    sin = jnp.concatenate([sin_ref[0]] * 4, axis=1)
    lane = lax.broadcasted_iota(jnp.int32, cos.shape, 1)
    first_half = (lane % HEAD_DIM) < (HEAD_DIM // 2)
    q = q_ref[0]
    k = k_ref[0]
    qo_ref[0] = q * cos + _rot_half(q, first_half) * sin
    ko_ref[0] = k * cos + _rot_half(k, first_half) * sin


def rope(proj3, cos_t, sin_t, tt):
    N, T, _ = proj3.shape
    tt = min(tt, T)
    tab = lambda n, i, j: (n, i, 0)
    qb = C_ATT // 512
    kb = (C_ATT + MIX) // 512
    out = jax.ShapeDtypeStruct((N, T, MIX), F32)
    return pl.pallas_call(
        _rope_kernel,
        grid=(N, T // tt, MIX // 512),
        in_specs=[pl.BlockSpec((1, tt, 512), lambda n, i, j: (n, i, qb + j)),
                  pl.BlockSpec((1, tt, 512), lambda n, i, j: (n, i, kb + j)),
                  pl.BlockSpec((1, tt, 128), tab),
                  pl.BlockSpec((1, tt, 128), tab)],
        out_specs=[pl.BlockSpec((1, tt, 512), lambda n, i, j: (n, i, j)),
                   pl.BlockSpec((1, tt, 512), lambda n, i, j: (n, i, j))],
        out_shape=[out, out],
        compiler_params=_cparams(("parallel", "parallel", "parallel")),
        name="rope",
    )(proj3, proj3, cos_t, sin_t)


def _prompt_attn_kernel(q_ref, k_ref, v_ref, cos_ref, sin_ref, mix_ref, o_ref, ko_ref, vo_ref,
                        qs_ref, acc_ref, m_ref, l_ref):
    del mix_ref
    S = q_ref.shape[1]
    B = BAND
    P = 2 * HEAD_DIM
    lane = lax.broadcasted_iota(jnp.int32, (B, P), 1)
    first_half = (lane % HEAD_DIM) < (HEAD_DIM // 2)
    h0 = _head0_mask((B, P))
    scale = HEAD_DIM ** -0.5

    def rope_rows(i, carry):
        rows = pl.ds(pl.multiple_of(i * B, B), B)
        cos = cos_ref[0, rows, :]
        sin = sin_ref[0, rows, :]
        q = q_ref[0, rows, :]
        k = k_ref[0, rows, :]
        qs_ref[rows, :] = q * cos + _rot_half(q, first_half) * sin
        ko_ref[0, rows, :] = k * cos + _rot_half(k, first_half) * sin
        vo_ref[0, rows, :] = v_ref[0, rows, :]
        return carry

    lax.fori_loop(0, S // B, rope_rows, 0)

    def attend(q, kk, vv, valid):
        ms, ls, os_ = [], [], []
        for h in range(2):
            qh = jnp.where(h0 if h == 0 else ~h0, q, 0.0).astype(BF16)
            s = lax.dot_general(qh, kk, NT, preferred_element_type=F32) * scale
            s = jnp.where(valid, s, NEG_INF)
            m = jnp.max(s, axis=-1, keepdims=True)
            e = jnp.exp(s - m)
            ms.append(m)
            ls.append(jnp.sum(e, axis=-1, keepdims=True))
            os_.append(jnp.dot(e.astype(BF16), vv, preferred_element_type=F32))
        return jnp.where(h0, ms[0], ms[1]), jnp.where(h0, ls[0], ls[1]), jnp.where(h0, os_[0], os_[1])

    def merge(rows, m_c, l_c, o_c):
        m_p = m_ref[rows, :]
        m_n = jnp.maximum(m_p, m_c)
        a_p = jnp.exp(m_p - m_n)
        a_c = jnp.exp(m_c - m_n)
        acc_ref[rows, :] = acc_ref[rows, :] * a_p + o_c * a_c
        l_ref[rows, :] = l_ref[rows, :] * a_p + l_c * a_c
        m_ref[rows, :] = m_n

    qi2 = lax.broadcasted_iota(jnp.int32, (B, 2 * B), 0)
    ki2 = lax.broadcasted_iota(jnp.int32, (B, 2 * B), 1)
    dist2 = qi2 + B - ki2
    band2 = (dist2 >= 0) & (dist2 <= B)
    qi1 = lax.broadcasted_iota(jnp.int32, (B, B), 0)
    ki1 = lax.broadcasted_iota(jnp.int32, (B, B), 1)
    causal1 = qi1 >= ki1

    def two_block_keys(cur, prev):
        kk = jnp.concatenate([ko_ref[0, prev, :], ko_ref[0, cur, :]], axis=0).astype(BF16)
        vv = jnp.concatenate([v_ref[0, prev, :], v_ref[0, cur, :]], axis=0).astype(BF16)
        return kk, vv

    def dil1(ib, carry):
        cur = pl.ds(pl.multiple_of(ib * B, B), B)
        prev = pl.ds(pl.multiple_of(jnp.maximum(ib - 1, 0) * B, B), B)
        kk, vv = two_block_keys(cur, prev)
        m_c, l_c, o_c = attend(qs_ref[cur, :], kk, vv, band2 & ((ki2 >= B) | (ib > 0)))
        acc_ref[cur, :] = o_c
        m_ref[cur, :] = m_c
        l_ref[cur, :] = l_c
        return carry

    lax.fori_loop(0, S // B, dil1, 0)

    d4 = DILATIONS[1]

    def dil4(ib, carry):
        base = pl.multiple_of(ib * (B * d4), B * d4)
        pbase = pl.multiple_of(jnp.maximum(ib - 1, 0) * (B * d4), B * d4)
        for r in range(d4):
            cur = pl.ds(base + r, B, stride=d4)
            prev = pl.ds(pbase + r, B, stride=d4)
            kk, vv = two_block_keys(cur, prev)
            m_c, l_c, o_c = attend(qs_ref[cur, :], kk, vv, band2 & ((ki2 >= B) | (ib > 0)))
            merge(cur, m_c, l_c, o_c)
        return carry

    lax.fori_loop(0, S // (B * d4), dil4, 0)

    d16 = DILATIONS[2]
    assert S == B * d16
    for r in range(d16):
        cur = pl.ds(r, B, stride=d16)
        m_c, l_c, o_c = attend(qs_ref[cur, :], ko_ref[0, cur, :].astype(BF16), v_ref[0, cur, :].astype(BF16), causal1)
        merge(cur, m_c, l_c, o_c)

    def finish(i, carry):
        rows = pl.ds(pl.multiple_of(i * B, B), B)
        o_ref[0, rows, :] = (acc_ref[rows, :] / l_ref[rows, :]).astype(BF16)
        return carry

    lax.fori_loop(0, S // B, finish, 0)


def prompt_attention(proj3, cos_t, sin_t, mixed):
    N, S, _ = proj3.shape
    qb, kb, vb = C_ATT // 128, (C_ATT + MIX) // 128, (C_ATT + 2 * MIX) // 128
    col = lambda b: pl.BlockSpec((1, S, 128), lambda n, h: (n, 0, b + h))
    tab = pl.BlockSpec((1, S, 128), lambda n, h: (0, 0, 0))
    kv = jax.ShapeDtypeStruct((N, S, MIX), F32)
    return pl.pallas_call(
        _prompt_attn_kernel,
        grid=(N, HEADS // 2),
        in_specs=[col(qb), col(kb), col(vb), tab, tab, pl.BlockSpec(memory_space=pl.ANY)],
        out_specs=[col(M_ATT // 128), col(0), col(0)],
        out_shape=[jax.ShapeDtypeStruct(mixed.shape, mixed.dtype), kv, kv],
        scratch_shapes=[pltpu.VMEM((S, 128), F32)] * 4,
        input_output_aliases={5: 0},
        compiler_params=_cparams(("parallel", "parallel")),
        name="prompt_attention",
    )(proj3, proj3, proj3, cos_t, sin_t, mixed)


def _sample_attn_kernel(q_ref, kn_ref, vn_ref, kb_ref, vb_ref, o_ref, ks_ref, vs_ref, *, t_valid):
    TQ = q_ref.shape[1]
    M = kb_ref.shape[1]
    TN_ = kn_ref.shape[1]
    h0 = _head0_mask((TQ, 2 * HEAD_DIM))
    scale = HEAD_DIM ** -0.5

    def counts(delta, in_range):
        c = jnp.zeros(delta.shape, F32)
        for win, dil in zip(WINDOWS, DILATIONS):
            c = c + jnp.where((delta % dil == 0) & (delta <= win) & in_range, 1.0, 0.0)
        return c

    tq = lax.broadcasted_iota(jnp.int32, (TQ, M), 0)
    kb_i = lax.broadcasted_iota(jnp.int32, (TQ, M), 1)
    d_buf = M + tq - kb_i
    c_buf = counts(d_buf, d_buf >= 0)
    tq2 = lax.broadcasted_iota(jnp.int32, (TQ, TN_), 0)
    tn2 = lax.broadcasted_iota(jnp.int32, (TQ, TN_), 1)
    d_new = tq2 - tn2
    c_new = counts(d_new, (d_new >= 0) & (tn2 < t_valid))

    row8 = lax.broadcasted_iota(jnp.int32, (8, 2 * HEAD_DIM), 0)
    for p in range(4):
        sl = slice(128 * p, 128 * (p + 1))
        qp = q_ref[0, :, sl]
        kb = kb_ref[0, :, sl]
        vb = vb_ref[0, :, sl]
        kn = kn_ref[0, :, sl]
        vn = vn_ref[0, :, sl]
        kb16, vb16, kn16, vn16 = kb.astype(BF16), vb.astype(BF16), kn.astype(BF16), vn.astype(BF16)
        outs = []
        for h in range(2):
            qh = jnp.where(h0 if h == 0 else ~h0, qp, 0.0).astype(BF16)
            s_b = lax.dot_general(qh, kb16, NT, preferred_element_type=F32) * scale
            s_n = lax.dot_general(qh, kn16, NT, preferred_element_type=F32) * scale
            s_b = jnp.where(c_buf > 0, s_b, NEG_INF)
            s_n = jnp.where(c_new > 0, s_n, NEG_INF)
            m = jnp.maximum(jnp.max(s_b, axis=-1, keepdims=True), jnp.max(s_n, axis=-1, keepdims=True))
            w_b = c_buf * jnp.exp(s_b - m)
            w_n = c_new * jnp.exp(s_n - m)
            l = jnp.sum(w_b, axis=-1, keepdims=True) + jnp.sum(w_n, axis=-1, keepdims=True)
            o = (jnp.dot(w_b.astype(BF16), vb16, preferred_element_type=F32)
                 + jnp.dot(w_n.astype(BF16), vn16, preferred_element_type=F32))
            outs.append(o / l)
        o_ref[0, :, sl] = jnp.where(h0, outs[0], outs[1]).astype(BF16)

        for src, new, dst in ((kb, kn, ks_ref), (vb, vn, vs_ref)):
            rolled = pltpu.roll(src, M - t_valid, 0)
            new_r = pltpu.roll(new[0:8], 8 - t_valid, 0)
            dst[0, 0:M - 8, sl] = rolled[0:M - 8]
            dst[0, M - 8:M, sl] = jnp.where(row8 >= 8 - t_valid, new_r, rolled[M - 8:M])


def sample_attention(q_rot, k_new, v_new, k_buf, v_buf, t_valid):
    N, TQ, _ = q_rot.shape
    M = k_buf.shape[1]
    TN_ = k_new.shape[1]
    assert t_valid <= 8 <= TQ
    blk = lambda r: pl.BlockSpec((1, r, 512), lambda n, j: (n, 0, j))
    return pl.pallas_call(
        functools.partial(_sample_attn_kernel, t_valid=t_valid),
        grid=(N, MIX // 512),
        in_specs=[blk(TQ), blk(TN_), blk(TN_), blk(M), blk(M)],
        out_specs=[blk(TQ), blk(M), blk(M)],
        out_shape=[jax.ShapeDtypeStruct((N, TQ, MIX), BF16),
                   jax.ShapeDtypeStruct((N, M, MIX), F32),
                   jax.ShapeDtypeStruct((N, M, MIX), F32)],
        compiler_params=_cparams(("parallel", "parallel")),
        name="sample_attention",
    )(q_rot, k_new, v_new, k_buf, v_buf)


def _glu_kernel(a_ref, g_ref, o_ref):
    o_ref[0] = a_ref[0] * _sigmoid(g_ref[0])


def glu(proj3, tt):
    N, T, _ = proj3.shape
    tt = min(tt, T)
    return pl.pallas_call(
        _glu_kernel,
        grid=(N, T // tt),
        in_specs=[pl.BlockSpec((1, tt, CONV_DIM), lambda n, i: (n, i, C_CONV // CONV_DIM)),
                  pl.BlockSpec((1, tt, CONV_DIM), lambda n, i: (n, i, C_CONV // CONV_DIM + 1))],
        out_specs=pl.BlockSpec((1, tt, CONV_DIM), lambda n, i: (n, i, 0)),
        out_shape=jax.ShapeDtypeStruct((N, T, CONV_DIM), F32),
        compiler_params=_cparams(("parallel", "parallel")),
        name="glu",
    )(proj3, proj3)


def _dwconv_kernel(u_ref, uh_ref, h0_ref, w_ref, b_ref, lw_ref, lb_ref, *rest, single_tile):
    o_ref, ext_ref, y_ref = rest[-3:]
    tt = u_ref.shape[1]
    i = pl.program_id(1)

    @pl.when(i == 0)
    def _():
        ext_ref[0:CONV_HALO, :] = h0_ref[0]

    if not single_tile:
        @pl.when(i > 0)
        def _():
            ext_ref[0:CONV_HALO, :] = uh_ref[0]

    ext_ref[CONV_HALO:, :] = u_ref[0]
    lead = CONV_HALO - (CONV_WIDTH - 1)
    rows = min(16, tt)
    for r0 in range(0, tt, rows):
        for c0 in range(0, CONV_DIM, 512):
            acc = jnp.broadcast_to(b_ref[:, c0:c0 + 512], (rows, 512))
            for j in range(CONV_WIDTH):
                acc = acc + ext_ref[r0 + lead + j:r0 + lead + j + rows, c0:c0 + 512] * w_ref[j:j + 1, c0:c0 + 512]
            y_ref[r0:r0 + rows, c0:c0 + 512] = acc
    y = y_ref[...]
    mu = jnp.mean(y, axis=-1, keepdims=True)
    var = jnp.mean(jnp.square(y - mu), axis=-1, keepdims=True)
    c = (y - mu) * lax.rsqrt(var + LN_EPS) * lw_ref[...] + lb_ref[...]
    o_ref[0] = (c * _sigmoid(c)).astype(BF16)


def dwconv_ln_silu(u, halo0, w, b, lw, lb, tt, mixed=None):
    N, T, C = u.shape
    tt = min(tt, T)
    hb = tt // CONV_HALO if tt >= CONV_HALO else 1
    in_specs = [pl.BlockSpec((1, tt, C), lambda n, i: (n, i, 0)),
                pl.BlockSpec((1, min(CONV_HALO, T), C), lambda n, i: (n, jnp.maximum(i * hb - 1, 0), 0)),
                pl.BlockSpec((1, CONV_HALO, C), lambda n, i: (n, 0, 0)),
                pl.BlockSpec((32, C), lambda n, i: (0, 0)),
                pl.BlockSpec((1, C), lambda n, i: (0, 0)),
                pl.BlockSpec((1, C), lambda n, i: (0, 0)),
                pl.BlockSpec((1, C), lambda n, i: (0, 0))]
    args = [u, u, halo0, w, b, lw, lb]
    if mixed is None:
        out_shape = jax.ShapeDtypeStruct((N, T, C), BF16)
        out_spec = pl.BlockSpec((1, tt, C), lambda n, i: (n, i, 0))
        aliases = {}
    else:
        in_specs.append(pl.BlockSpec(memory_space=pl.ANY))
        args.append(mixed)
        out_shape = jax.ShapeDtypeStruct(mixed.shape, mixed.dtype)
        out_spec = pl.BlockSpec((1, tt, C), lambda n, i: (n, i, M_CONV // C))
        aliases = {7: 0}
    return pl.pallas_call(
        functools.partial(_dwconv_kernel, single_tile=(T == tt)),
        grid=(N, T // tt),
        in_specs=in_specs,
        out_specs=out_spec,
        out_shape=out_shape,
        scratch_shapes=[pltpu.VMEM((CONV_HALO + tt, C), F32), pltpu.VMEM((tt, C), F32)],
        input_output_aliases=aliases,
        compiler_params=_cparams(("parallel", "arbitrary")),
        name="dwconv_ln_silu",
    )(*args)


def _ffn_conv_act(ext_ref, wg_ref, wv_ref, bg_ref, bv_ref, store, tt):
    lead = FFN_HALO - (FFN_CONV_WIDTH - 1)
    rows = min(32, tt)
    for r0 in range(0, tt, rows):
        ys = []
        for s, w_ref, b_ref in ((0, wg_ref, bg_ref), (1, wv_ref, bv_ref)):
            acc = jnp.broadcast_to(b_ref[...], (rows, b_ref.shape[1]))
            for j in range(FFN_CONV_WIDTH):
                acc = acc + ext_ref[s, r0 + lead + j:r0 + lead + j + rows, :] * w_ref[j:j + 1, :]
            ys.append(acc)
        gate, val = ys
        store(r0, rows, (gate * _sigmoid(gate) * val).astype(BF16))


def _ffn_act_kernel(ug_ref, uv_ref, hg_ref, hv_ref, h0g_ref, h0v_ref, wg_ref, wv_ref, bg_ref, bv_ref, o_ref, ext_ref):
    tt = ug_ref.shape[1]
    i = pl.program_id(1)

    @pl.when(i == 0)
    def _():
        ext_ref[0, 0:FFN_HALO, :] = h0g_ref[0]
        ext_ref[1, 0:FFN_HALO, :] = h0v_ref[0]

    @pl.when(i > 0)
    def _():
        ext_ref[0, 0:FFN_HALO, :] = hg_ref[0]
        ext_ref[1, 0:FFN_HALO, :] = hv_ref[0]

    ext_ref[0, FFN_HALO:, :] = ug_ref[0]
    ext_ref[1, FFN_HALO:, :] = uv_ref[0]

    def store(r0, rows, val):
        o_ref[0, r0:r0 + rows, :] = val

    _ffn_conv_act(ext_ref, wg_ref, wv_ref, bg_ref, bv_ref, store, tt)


def ffn_act(u3, halo0, w, b, tt, tf):
    N, T, F2 = u3.shape
    F = F2 // 2
    tt = min(tt, T)
    nf = F // tf
    hb = tt // FFN_HALO
    g = lambda n, i, j: (n, i, j)
    v = lambda n, i, j: (n, i, j + nf)
    hg = lambda n, i, j: (n, jnp.maximum(i * hb - 1, 0), j)
    hv = lambda n, i, j: (n, jnp.maximum(i * hb - 1, 0), j + nf)
    return pl.pallas_call(
        _ffn_act_kernel,
        grid=(N, T // tt, nf),
        in_specs=[pl.BlockSpec((1, tt, tf), g), pl.BlockSpec((1, tt, tf), v),
                  pl.BlockSpec((1, FFN_HALO, tf), hg), pl.BlockSpec((1, FFN_HALO, tf), hv),
                  pl.BlockSpec((1, FFN_HALO, tf), lambda n, i, j: (n, 0, j)),
                  pl.BlockSpec((1, FFN_HALO, tf), lambda n, i, j: (n, 0, j + nf)),
                  pl.BlockSpec((8, tf), lambda n, i, j: (0, j)), pl.BlockSpec((8, tf), lambda n, i, j: (0, j + nf)),
                  pl.BlockSpec((1, tf), lambda n, i, j: (0, j)), pl.BlockSpec((1, tf), lambda n, i, j: (0, j + nf))],
        out_specs=pl.BlockSpec((1, tt, tf), g),
        out_shape=jax.ShapeDtypeStruct((N, T, F), BF16),
        scratch_shapes=[pltpu.VMEM((2, FFN_HALO + tt, tf), F32)],
        compiler_params=_cparams(("parallel", "parallel", "parallel")),
        name="ffn_act",
    )(u3, u3, u3, u3, halo0, halo0, w, w, b, b)


def _ffn_fused_kernel(x_ref, g_ref, wg_ref, wv_ref, wd_ref, cwg_ref, cwv_ref, cbg_ref, cbv_ref, h0g_ref, h0v_ref,
                      o_ref, sg_ref, sv_ref, h_ref, ext_ref, carry_ref, act_ref, *, tiles_per_seq):
    i = pl.program_id(0)
    j = pl.program_id(1)
    tm = x_ref.shape[0]

    @pl.when(j == 0)
    def _():
        x = x_ref[...]
        ms = jnp.mean(x * x, axis=-1, keepdims=True)
        h_ref[...] = (x * lax.rsqrt(ms + RMS_EPS) * g_ref[...]).astype(BF16)
        o_ref[...] = x

    h = h_ref[...]
    ug = jnp.dot(h, wg_ref[...], preferred_element_type=F32)
    uv = jnp.dot(h, wv_ref[...], preferred_element_type=F32)
    seq_start = (i % tiles_per_seq) == 0

    @pl.when(seq_start)
    def _():
        ext_ref[0, 0:FFN_HALO, :] = h0g_ref[0]
        ext_ref[1, 0:FFN_HALO, :] = h0v_ref[0]

    @pl.when(jnp.logical_not(seq_start))
    def _():
        ext_ref[0, 0:FFN_HALO, :] = carry_ref[j, 0]
        ext_ref[1, 0:FFN_HALO, :] = carry_ref[j, 1]

    ext_ref[0, FFN_HALO:, :] = ug
    ext_ref[1, FFN_HALO:, :] = uv
    tail_g = ug[tm - FFN_HALO:, :]
    tail_v = uv[tm - FFN_HALO:, :]
    carry_ref[j, 0] = tail_g
    carry_ref[j, 1] = tail_v
    sg_ref[0] = tail_g
    sv_ref[0] = tail_v

    def store(r0, rows, val):
        act_ref[r0:r0 + rows, :] = val

    _ffn_conv_act(ext_ref, cwg_ref, cwv_ref, cbg_ref, cbv_ref, store, tm)
    o_ref[...] += jnp.dot(act_ref[...], wd_ref[...], preferred_element_type=F32)


def ffn_fused(x, g, w_up, w_down, cw, cb, halo0, seq_len, tm, tf):
    M, D = x.shape
    F = w_down.shape[0]
    assert M % tm == 0 and seq_len % tm == 0 and F % tf == 0
    nf = F // tf
    tps = seq_len // tm
    tails = jax.ShapeDtypeStruct((M // tm, FFN_HALO, F), F32)
    return pl.pallas_call(
        functools.partial(_ffn_fused_kernel, tiles_per_seq=tps),
        grid=(M // tm, nf),
        in_specs=[pl.BlockSpec((tm, D), lambda i, j: (i, 0), pipeline_mode=pl.Buffered(1)),
                  pl.BlockSpec((1, D), lambda i, j: (0, 0)),
                  pl.BlockSpec((D, tf), lambda i, j: (0, j)),
                  pl.BlockSpec((D, tf), lambda i, j: (0, j + nf)),
                  pl.BlockSpec((tf, D), lambda i, j: (j, 0)),
                  pl.BlockSpec((8, tf), lambda i, j: (0, j)), pl.BlockSpec((8, tf), lambda i, j: (0, j + nf)),
                  pl.BlockSpec((1, tf), lambda i, j: (0, j)), pl.BlockSpec((1, tf), lambda i, j: (0, j + nf)),
                  pl.BlockSpec((1, FFN_HALO, tf), lambda i, j: (i // tps, 0, j)),
                  pl.BlockSpec((1, FFN_HALO, tf), lambda i, j: (i // tps, 0, j + nf))],
        out_specs=[pl.BlockSpec((tm, D), lambda i, j: (i, 0)),
                   pl.BlockSpec((1, FFN_HALO, tf), lambda i, j: (i, 0, j)),
                   pl.BlockSpec((1, FFN_HALO, tf), lambda i, j: (i, 0, j))],
        out_shape=[jax.ShapeDtypeStruct((M, D), F32), tails, tails],
        scratch_shapes=[pltpu.VMEM((tm, D), BF16), pltpu.VMEM((2, FFN_HALO + tm, tf), F32),
                        pltpu.VMEM((nf, 2, FFN_HALO, tf), F32), pltpu.VMEM((tm, tf), BF16)],
        compiler_params=_cparams(("arbitrary", "arbitrary")),
        name="ffn_fused",
    )(x, g, w_up, w_up, w_down, cw, cw, cb, cb, halo0, halo0)


def _rwkv_kernel(r_ref, k_ref, v_ref, xg_ref, xwa_ref,
                 sr_ref, sk_ref, sv_ref, sxg_ref, sxwa_ref,
                 mur_ref, muk_ref, muv_ref, muxg_ref, muxwa_ref,
                 w0_ref, a0_ref, kk_ref, ka_ref, rk_ref, lnw_ref, lnb_ref,
                 w2_ref, a2_ref, g2_ref, s0_ref,
                 o_ref, so_ref,
                 S_ref, pr_ref, pk_ref, pv_ref, pxg_ref, pxwa_ref, *, t_valid, t_total):
    c = pl.program_id(2)
    Tc = r_ref.shape[1]

    @pl.when(c == 0)
    def _():
        S_ref[...] = s0_ref[0, 0]
        pr_ref[...] = sr_ref[0]
        pk_ref[...] = sk_ref[0]
        pv_ref[...] = sv_ref[0]
        pxg_ref[...] = sxg_ref[0]
        pxwa_ref[...] = sxwa_ref[0]

    def lerp(p_ref, prev_ref, mu_ref):
        p = p_ref[0]
        row = lax.broadcasted_iota(jnp.int32, p.shape, 0)
        shifted = jnp.where(row == 0, prev_ref[...], pltpu.roll(p, 1, 0))
        prev_ref[...] = p[Tc - 1:Tc, :]
        return p + (shifted - p) * mu_ref[...]

    xr = lerp(r_ref, pr_ref, mur_ref)
    xk = lerp(k_ref, pk_ref, muk_ref)
    xv = lerp(v_ref, pv_ref, muv_ref)
    xg = lerp(xg_ref, pxg_ref, muxg_ref)
    xwa = lerp(xwa_ref, pxwa_ref, muxwa_ref)
    xw = xwa[:, :DECAY_LORA]
    xa = xwa[:, DECAY_LORA:]

    h0 = _head0_mask((Tc, 2 * HEAD_DIM))

    def head_sum(x):
        s_0 = jnp.sum(jnp.where(h0, x, 0.0), axis=-1, keepdims=True)
        s_1 = jnp.sum(jnp.where(h0, 0.0, x), axis=-1, keepdims=True)
        return jnp.where(h0, s_0, s_1)

    def mm(x, y_):
        return jnp.dot(x, y_, preferred_element_type=F32)

    z = w0_ref[...] + mm(jnp.tanh(xw).astype(BF16), w2_ref[...])
    w_log = jnp.minimum(z, 0.0) - jnp.log(1.0 + jnp.exp(-jnp.abs(z))) - 0.5
    ld = -jnp.exp(w_log)
    a = _sigmoid(a0_ref[...] + mm(xa.astype(BF16), a2_ref[...]))
    g = mm(_sigmoid(xg).astype(BF16), g2_ref[...])

    kk = xk * kk_ref[...]
    kk = kk * lax.rsqrt(jnp.maximum(head_sum(kk * kk), 1e-24))
    kmod = xk * (1.0 + (a - 1.0) * ka_ref[...])
    avec = -kk
    bvec = kk * a
    bonus = head_sum(xr * kmod * rk_ref[...]) * xv
    vval = xv

    if t_valid < t_total:
        row = lax.broadcasted_iota(jnp.int32, (Tc, 2 * HEAD_DIM), 0) + c * Tc
        ok = row < t_valid
        ld = jnp.where(ok, ld, 0.0)
        avec = jnp.where(ok, avec, 0.0)
        bvec = jnp.where(ok, bvec, 0.0)
        kmod_s = jnp.where(ok, kmod, 0.0)
        vval = jnp.where(ok, vval, 0.0)
    else:
        kmod_s = kmod

    C = CHUNK
    tri_r = lax.broadcasted_iota(jnp.int32, (C, C), 0)
    tri_c = lax.broadcasted_iota(jnp.int32, (C, C), 1)
    tril = jnp.where(tri_r >= tri_c, 1.0, 0.0).astype(F32)
    it = lax.broadcasted_iota(jnp.int32, (2 * C, 2 * C), 0) % C
    js = lax.broadcasted_iota(jnp.int32, (2 * C, 2 * C), 1) % C
    strict = it > js
    incl = it >= js
    h0c = _head0_mask((C, 2 * HEAD_DIM))

    def stack(x):
        return jnp.concatenate([jnp.where(h0c, x, 0.0), jnp.where(h0c, 0.0, x)], axis=0)

    n_chunks = Tc // C
    grp = 2 if n_chunks % 2 == 0 else 1
    gw = 2 * C * grp
    eye_g = jnp.where(lax.broadcasted_iota(jnp.int32, (gw, gw), 0)
                      == lax.broadcasted_iota(jnp.int32, (gw, gw), 1), 1.0, 0.0).astype(F32)
    zero_blk = jnp.zeros((2 * C, 2 * C), F32)

    def bdiag(blocks):
        if len(blocks) == 1:
            return blocks[0]
        return jnp.concatenate([jnp.concatenate([blocks[0], zero_blk], axis=1),
                                jnp.concatenate([zero_blk, blocks[1]], axis=1)], axis=0)

    def cat(blocks):
        return blocks[0] if len(blocks) == 1 else jnp.concatenate(blocks, axis=0)

    a16, r32, bb16, kb16, v16, gts, Ls, AKs, RBs, RKs = ([] for _ in range(10))
    for j in range(n_chunks):
        sl = slice(C * j, C * (j + 1))
        ldc = ld[sl]
        cs = jnp.dot(tril, ldc, precision=lax.Precision.HIGHEST, preferred_element_type=F32)
        tot = cs[C - 1:C, :]
        eg = jnp.exp(cs)
        egi = jnp.exp(-cs)
        ege = jnp.exp(cs - ldc)
        et = jnp.exp(tot - cs)
        gts.append(jnp.exp(tot))
        a_s = stack(avec[sl] * ege).astype(BF16)
        r_s = stack(xr[sl] * eg)
        b_s = stack(bvec[sl] * egi).astype(BF16)
        k_s = stack(kmod_s[sl] * egi).astype(BF16)
        sc = lax.dot_general(jnp.concatenate([a_s, r_s.astype(BF16)], axis=0), jnp.concatenate([b_s, k_s], axis=0), NT,
                             preferred_element_type=F32)
        a16.append(a_s)
        r32.append(r_s)
        bb16.append(stack(bvec[sl] * et).astype(BF16))
        kb16.append(stack(kmod_s[sl] * et).astype(BF16))
        v16.append(stack(vval[sl]).astype(BF16))
        Ls.append(jnp.where(strict, sc[:2 * C, :2 * C], 0.0))
        AKs.append(jnp.where(strict, sc[:2 * C, 2 * C:], 0.0))
        RBs.append(jnp.where(incl, sc[2 * C:, :2 * C], 0.0))
        RKs.append(jnp.where(incl, sc[2 * C:, 2 * C:], 0.0))

    groups = [list(range(g0, g0 + grp)) for g0 in range(0, n_chunks, grp)]
    Ps = [bdiag([Ls[c_] for c_ in g_]) for g_ in groups]
    Tms = [eye_g + p_ for p_ in Ps]
    for _ in range(C.bit_length() - 2):
        for gi in range(len(groups)):
            P16 = Ps[gi].astype(BF16)
            Ps[gi] = mm(P16, P16)
            Tms[gi] = Tms[gi] + mm(Tms[gi].astype(BF16), Ps[gi].astype(BF16))

    RAs, YNs, Gs, Ns = {}, {}, {}, {}
    for gi, g_ in enumerate(groups):
        T16 = Tms[gi].astype(BF16)
        A2 = cat([a16[c_] for c_ in g_])
        V2 = cat([v16[c_] for c_ in g_])
        R2 = cat([r32[c_] for c_ in g_])
        RB16 = bdiag([RBs[c_] for c_ in g_]).astype(BF16)
        RK16 = bdiag([RKs[c_] for c_ in g_]).astype(BF16)
        TA16 = mm(T16, A2).astype(BF16)
        TV16 = mm(T16, mm(bdiag([AKs[c_] for c_ in g_]).astype(BF16), V2).astype(BF16)).astype(BF16)
        RA = R2 + mm(RB16, TA16)
        YN = mm(jnp.concatenate([RB16, RK16], axis=1), jnp.concatenate([TV16, V2], axis=0))
        for q, c_ in enumerate(g_):
            rs = slice(2 * C * q, 2 * C * (q + 1))
            RAs[c_] = RA[rs].astype(BF16)
            YNs[c_] = YN[rs]
            Gs[c_] = lax.dot_general(TA16[rs], bb16[c_], TN, preferred_element_type=F32).astype(BF16)
            Ns[c_] = lax.dot_general(jnp.concatenate([TV16[rs], v16[c_]], axis=0),
                                     jnp.concatenate([bb16[c_], kb16[c_]], axis=0), TN, preferred_element_type=F32)

    ys = []
    S = S_ref[...]
    for j in range(n_chunks):
        S16 = S.astype(BF16)
        Y = lax.dot_general(RAs[j], S16, NT, preferred_element_type=F32) + YNs[j]
        ys.append(Y[:C] + Y[C:])
        S = S * gts[j] + mm(S16, Gs[j]) + Ns[j]
    S_ref[...] = S
    y = cat(ys)

    mu = head_sum(y) * (1.0 / HEAD_DIM)
    var = head_sum(jnp.square(y - mu)) * (1.0 / HEAD_DIM)
    yn = (y - mu) * lax.rsqrt(var + GN_EPS) * lnw_ref[...] + lnb_ref[...]
    o_ref[0] = ((yn + bonus) * g).astype(BF16)

    @pl.when(c == pl.num_programs(2) - 1)
    def _():
        so_ref[0, 0] = S


def rwkv_time_mix(proj3, shift0, s0, prm, t_valid, tc, out_width=MIX):
    N, T, _ = proj3.shape
    tc = min(tc, T)
    assert T % tc == 0 and tc % CHUNK == 0
    rb = C_RWKV // 128
    pspec = lambda w, off: pl.BlockSpec((1, tc, w), lambda n, h, c: (n, c, off(h)))
    sspec = lambda w, off: pl.BlockSpec((1, 1, w), lambda n, h, c: (n, 0, off(h)))
    mspec = lambda w, off: pl.BlockSpec((1, w), lambda n, h, c: (0, off(h)))
    hspec = pl.BlockSpec((1, 128), lambda n, h, c: (0, h))
    xg_p = (C_RWKV + 3 * MIX) // GATE_PAD
    xwa_p = (C_RWKV + 3 * MIX + GATE_PAD) // 256
    xg_s = 3 * MIX // GATE_PAD
    xwa_s = (3 * MIX + GATE_PAD) // 256
    in_specs = [
        pspec(128, lambda h: rb + h), pspec(128, lambda h: rb + 12 + h), pspec(128, lambda h: rb + 24 + h),
        pspec(GATE_PAD, lambda h: xg_p), pspec(256, lambda h: xwa_p),
        sspec(128, lambda h: h), sspec(128, lambda h: 12 + h), sspec(128, lambda h: 24 + h),
        sspec(GATE_PAD, lambda h: xg_s), sspec(256, lambda h: xwa_s),
        mspec(128, lambda h: h), mspec(128, lambda h: 12 + h), mspec(128, lambda h: 24 + h),
        mspec(GATE_PAD, lambda h: xg_s), mspec(256, lambda h: xwa_s),
        hspec, hspec, hspec, hspec, hspec, hspec, hspec,
        pl.BlockSpec((DECAY_LORA, 128), lambda n, h, c: (0, h)),
        pl.BlockSpec((AAA_LORA, 128), lambda n, h, c: (0, h)),
        pl.BlockSpec((GATE_PAD, 128), lambda n, h, c: (0, h)),
        pl.BlockSpec((1, 1, 128, 128), lambda n, h, c: (n, h, 0, 0)),
    ]
    args = ([proj3] * 5 + [shift0] * 5 + [prm['mu']] * 5
            + [prm['w0'], prm['a0'], prm['k_k'], prm['k_a'], prm['r_k'], prm['ln_w'], prm['ln_b'],
               prm['w2'], prm['a2'], prm['g2'], s0])
    return pl.pallas_call(
        functools.partial(_rwkv_kernel, t_valid=t_valid, t_total=T),
        grid=(N, HEADS // 2, T // tc),
        in_specs=in_specs,
        out_specs=[pl.BlockSpec((1, tc, 128), lambda n, h, c: (n, c, M_RWKV // 128 + h)),
                   pl.BlockSpec((1, 1, 128, 128), lambda n, h, c: (n, h, 0, 0))],
        out_shape=[jax.ShapeDtypeStruct((N, T, out_width), BF16),
                   jax.ShapeDtypeStruct((N, HEADS // 2, 128, 128), F32)],
        scratch_shapes=[pltpu.VMEM((128, 128), F32), pltpu.VMEM((1, 128), F32), pltpu.VMEM((1, 128), F32),
                        pltpu.VMEM((1, 128), F32), pltpu.VMEM((1, GATE_PAD), F32), pltpu.VMEM((1, 256), F32)],
        compiler_params=_cparams(("parallel", "parallel", "arbitrary")),
        name="rwkv7_time_mix",
    )(*args)


def _permute_rwkv_cols(t):
    pad = jnp.zeros(t.shape[:-1] + (GATE_PAD - GATE_LORA,), t.dtype)
    o = 3 * MIX
    return jnp.concatenate([t[..., :o], t[..., o + DECAY_LORA + AAA_LORA:], pad, t[..., o:o + DECAY_LORA + AAA_LORA]], axis=-1)


def _unpermute_rwkv_cols(t):
    o = 3 * MIX
    return jnp.concatenate([t[..., :o], t[..., o + GATE_PAD:], t[..., o:o + GATE_LORA]], axis=-1)


def _layer_params(l, norm_mix, w_in, rwkv_mu, rwkv_w0, rwkv_w2, rwkv_a0, rwkv_a2, rwkv_g2, rwkv_k_k, rwkv_k_a,
                  rwkv_r_k, rwkv_ln_w, rwkv_ln_b, conv_w, conv_b, conv_ln_w, conv_ln_b, w_out, norm_ffn, w_up,
                  ffn_conv_w, ffn_conv_b, w_down):
    wi = w_in[l]
    w_rwkv = _permute_rwkv_cols(wi[:, :RWKV_PROJ])
    w_att = wi[:, RWKV_PROJ:RWKV_PROJ + ATT_PROJ]
    w_conv = wi[:, RWKV_PROJ + ATT_PROJ:]
    tail = jnp.zeros((D_MODEL, NP - C_RWKV - RWKV_PAD), wi.dtype)
    row = lambda t: t.reshape(1, -1)
    return {
        'norm_mix': row(norm_mix[l]),
        'w_in': jnp.concatenate([w_conv, w_att, w_rwkv, tail], axis=1).astype(BF16),
        'mu': _permute_rwkv_cols(row(rwkv_mu[l])),
        'w0': row(rwkv_w0[l]), 'a0': row(rwkv_a0[l]), 'k_k': row(rwkv_k_k[l]), 'k_a': row(rwkv_k_a[l]),
        'r_k': row(rwkv_r_k[l]), 'ln_w': row(rwkv_ln_w[l]), 'ln_b': row(rwkv_ln_b[l]),
        'w2': rwkv_w2[l].astype(BF16), 'a2': rwkv_a2[l].astype(BF16),
        'g2': jnp.pad(rwkv_g2[l], ((0, GATE_PAD - GATE_LORA), (0, 0))).astype(BF16),
        'conv_w': jnp.pad(conv_w[l], ((0, 32 - CONV_WIDTH), (0, 0))), 'conv_b': row(conv_b[l]),
        'conv_ln_w': row(conv_ln_w[l]), 'conv_ln_b': row(conv_ln_b[l]),
        'w_out': w_out[l].astype(BF16),
        'norm_ffn': row(norm_ffn[l]),
        'w_up': w_up[l].astype(BF16),
        'ffn_conv_w': jnp.pad(ffn_conv_w[l], ((0, 8 - FFN_CONV_WIDTH), (0, 0))), 'ffn_conv_b': row(ffn_conv_b[l]),
        'w_down': w_down[l].astype(BF16),
    }


def _rope_tables(pos):
    half = HEAD_DIM // 2
    inv_freq = ROPE_THETA ** (-jnp.arange(half, dtype=F32) * 2.0 / HEAD_DIM)
    ang = pos.astype(F32)[..., None] * inv_freq
    cos, sin = jnp.cos(ang), jnp.sin(ang)
    return jnp.concatenate([cos, cos, cos, cos], axis=-1), jnp.concatenate([-sin, sin, -sin, sin], axis=-1)


def _state_to_blockdiag(s):
    N = s.shape[0]
    s = s.reshape(N, HEADS // 2, 2, HEAD_DIM, HEAD_DIM)
    z = jnp.zeros_like(s[:, :, 0])
    top = jnp.concatenate([s[:, :, 0], z], axis=-1)
    bot = jnp.concatenate([z, s[:, :, 1]], axis=-1)
    return jnp.concatenate([top, bot], axis=-2)


def _blockdiag_to_state(s):
    N = s.shape[0]
    return jnp.stack([s[:, :, :HEAD_DIM, :HEAD_DIM], s[:, :, HEAD_DIM:, HEAD_DIM:]], axis=2).reshape(
        N, HEADS, HEAD_DIM, HEAD_DIM)


def _last_rows(buf, u, t_valid):
    keep = buf.shape[1]
    if t_valid >= keep:
        return u[:, t_valid - keep:t_valid]
    return jnp.concatenate([buf[:, t_valid:], u[:, :t_valid]], axis=1)


def _front_pad(buf, rows):
    return jnp.pad(buf, ((0, 0), (rows - buf.shape[1], 0), (0, 0)))


def _prompt_layer(x2, N, S, prm, cos_t, sin_t):
    f32 = x2.dtype
    proj3 = norm_matmul(x2, prm['norm_mix'], prm['w_in'], 512, 512).reshape(N, S, NP)

    mixed, s_new = rwkv_time_mix(proj3, jnp.zeros((N, 1, RWKV_PAD), f32), jnp.zeros((N, HEADS // 2, 128, 128), f32),
                                 prm, S, 256, out_width=D_MODEL)
    shift_new = _unpermute_rwkv_cols(proj3[:, S - 1:S, C_RWKV:C_RWKV + RWKV_PAD])

    mixed, k_state, v_state = prompt_attention(proj3, cos_t, sin_t, mixed)

    u = glu(proj3, 256)
    mixed = dwconv_ln_silu(u, jnp.zeros((N, CONV_HALO, CONV_DIM), f32), prm['conv_w'], prm['conv_b'],
                           prm['conv_ln_w'], prm['conv_ln_b'], 128, mixed=mixed)
    conv_new = u[:, S - (CONV_WIDTH - 1):]

    x2 = matmul_res(mixed.reshape(N * S, D_MODEL), prm['w_out'], x2, 512, 512, D_MODEL)

    tm = 512
    x2, tail_g, tail_v = ffn_fused(x2, prm['norm_ffn'], prm['w_up'], prm['w_down'], prm['ffn_conv_w'], prm['ffn_conv_b'],
                                   jnp.zeros((N, FFN_HALO, 2 * D_FF), f32), S, tm, 256)
    last = slice(S // tm - 1, None, S // tm)
    keep = FFN_CONV_WIDTH - 1
    ffn_new = jnp.concatenate([tail_g[last, FFN_HALO - keep:], tail_v[last, FFN_HALO - keep:]], axis=-1)
    return x2, (shift_new, _blockdiag_to_state(s_new), k_state, v_state, conv_new, ffn_new)


def _sample_layer(x2, N, T, t_valid, prm, cos_t, sin_t, carry, attn_bufs):
    shift0, wkv0, conv_buf, ffn_buf = carry
    proj3 = norm_matmul(x2, prm['norm_mix'], prm['w_in'], 512, 512).reshape(N, T, NP)

    proj_r = jnp.pad(proj3, ((0, 0), (0, CHUNK - T), (0, 0)))
    o_rwkv, s_new = rwkv_time_mix(proj_r, _permute_rwkv_cols(shift0), _state_to_blockdiag(wkv0), prm, t_valid, CHUNK)
    o_rwkv = o_rwkv[:, :T]
    shift_new = _unpermute_rwkv_cols(proj3[:, t_valid - 1:t_valid, C_RWKV:C_RWKV + RWKV_PAD])

    q_rot, k_rot = rope(proj3, cos_t, sin_t, 256)
    v_new = proj3[:, :, C_ATT + 2 * MIX:C_ATT + 3 * MIX]
    padr = ((0, 0), (0, 128 - T), (0, 0))
    o_att, k_state, v_state = sample_attention(q_rot, jnp.pad(k_rot, padr), jnp.pad(v_new, padr),
                                               attn_bufs[0], attn_bufs[1], t_valid)

    u = glu(proj3, 256)
    o_conv = dwconv_ln_silu(u, _front_pad(conv_buf, CONV_HALO), prm['conv_w'], prm['conv_b'],
                            prm['conv_ln_w'], prm['conv_ln_b'], 128)
    conv_new = _last_rows(conv_buf, u, t_valid)

    mixed = jnp.concatenate([o_rwkv, o_att, o_conv], axis=-1).reshape(N * T, D_MODEL)
    x2 = matmul_res(mixed, prm['w_out'], x2, 512, 512, D_MODEL)

    uf3 = norm_matmul(x2, prm['norm_ffn'], prm['w_up'], 512, 512).reshape(N, T, 2 * D_FF)
    act = ffn_act(uf3, _front_pad(ffn_buf, FFN_HALO), prm['ffn_conv_w'], prm['ffn_conv_b'], 512, D_FF // 2)
    ffn_new = _last_rows(ffn_buf, uf3, t_valid)
    x2 = matmul_res(act.reshape(N * T, D_FF), prm['w_down'], x2, 512, 512, D_FF // 2)
    return x2, (shift_new, _blockdiag_to_state(s_new), k_state, v_state, conv_new, ffn_new)


def kernel(x_prompt, x_sample, state_rwkv_shift, state_rwkv_wkv, state_attn_k, state_attn_v, state_conv, state_ffn_conv, pos_sample, norm_mix, w_in, rwkv_mu, rwkv_w0, rwkv_w2, rwkv_a0, rwkv_a2, rwkv_g2, rwkv_k_k, rwkv_k_a, rwkv_r_k, rwkv_ln_w, rwkv_ln_b, conv_w, conv_b, conv_ln_w, conv_ln_b, w_out, norm_ffn, w_up, ffn_conv_w, ffn_conv_b, w_down, norm_final):
    B, S, _ = x_prompt.shape
    NB, TS, _ = x_sample.shape
    depth = w_in.shape[0]
    win_buf = state_attn_k.shape[2]
    assert win_buf == S, "prompt key/value state is the whole rotated sequence"
    TSP = 8

    xp = x_prompt.reshape(B * S, D_MODEL)
    xs = jnp.pad(x_sample, ((0, 0), (0, TSP - TS), (0, 0))).reshape(NB * TSP, D_MODEL)
    cos_p, sin_p = _rope_tables(jnp.arange(S, dtype=jnp.int32)[None])
    cos_s, sin_s = _rope_tables(jnp.pad(pos_sample, ((0, 0), (0, TSP - TS))))
    new_p, new_s = [], []
    for l in range(depth):
        prm = _layer_params(l, norm_mix, w_in, rwkv_mu, rwkv_w0, rwkv_w2, rwkv_a0, rwkv_a2, rwkv_g2, rwkv_k_k,
                            rwkv_k_a, rwkv_r_k, rwkv_ln_w, rwkv_ln_b, conv_w, conv_b, conv_ln_w, conv_ln_b, w_out,
                            norm_ffn, w_up, ffn_conv_w, ffn_conv_b, w_down)
        xp, st_p = _prompt_layer(xp, B, S, prm, cos_p, sin_p)
        carry_s = (state_rwkv_shift[l], state_rwkv_wkv[l], state_conv[l], state_ffn_conv[l])
        bufs = (state_attn_k[l].reshape(NB, win_buf, MIX), state_attn_v[l].reshape(NB, win_buf, MIX))
        xs, st_s = _sample_layer(xs, NB, TSP, TS, prm, cos_s, sin_s, carry_s, bufs)
        new_p.append(st_p)
        new_s.append(st_s)

    g = norm_final.reshape(1, D_MODEL)
    y_prompt = rmsnorm(xp, g, 512).reshape(B, S, D_MODEL)
    y_sample = rmsnorm(xs, g, 512).reshape(NB, TSP, D_MODEL)[:, :TS]

    def stack(states, i, shape=None):
        t = jnp.stack([st[i] for st in states], axis=0)
        return t if shape is None else t.reshape(shape)

    kv_p = (depth, B, win_buf, HEADS, HEAD_DIM)
    kv_s = (depth, NB, win_buf, HEADS, HEAD_DIM)
    return (y_prompt, y_sample,
            stack(new_p, 0), stack(new_s, 0), stack(new_p, 1), stack(new_s, 1),
            stack(new_p, 2, kv_p), stack(new_s, 2, kv_s), stack(new_p, 3, kv_p), stack(new_s, 3, kv_s),
            stack(new_p, 4), stack(new_s, 4), stack(new_p, 5), stack(new_s, 5))
```

```python
import functools

import jax
import jax.numpy as jnp
from jax import lax
from jax.experimental import pallas as pl
from jax.experimental.pallas import tpu as pltpu

F32 = jnp.float32
BF16 = jnp.bfloat16

D_MODEL = 4096
HEAD_DIM = 64
HEADS = 24
MIX = HEADS * HEAD_DIM
CONV_DIM = 1024
DECAY_LORA = 128
AAA_LORA = 128
GATE_LORA = 480
GATE_PAD = 512
RWKV_PROJ = 3 * MIX + DECAY_LORA + AAA_LORA + GATE_LORA
RWKV_PAD = 3 * MIX + GATE_PAD + DECAY_LORA + AAA_LORA
ATT_PROJ = 3 * MIX
CONV_PROJ = 2 * CONV_DIM
CONV_WIDTH = 31
CONV_HALO = 32
D_FF = 11008
FFN_CONV_WIDTH = 3
FFN_HALO = 8
BAND = 128
DILATIONS = (1, 4, 16)
WINDOWS = (128, 512, 2048)
ROPE_THETA = 10000.0
RMS_EPS = 1e-6
LN_EPS = 1e-5
GN_EPS = 64e-5
NEG_INF = -1e30
CHUNK = 64

C_CONV = 0
C_ATT = CONV_PROJ
C_RWKV = C_ATT + ATT_PROJ
NP = 12288
M_RWKV, M_ATT, M_CONV = 0, MIX, 2 * MIX

VMEM_LIMIT = 56 * 1024 * 1024

NT = (((1,), (1,)), ((), ()))
TN = (((0,), (0,)), ((), ()))


def _cparams(sem):
    return pltpu.CompilerParams(dimension_semantics=sem, vmem_limit_bytes=VMEM_LIMIT)


def _sigmoid(x):
    return 1.0 / (1.0 + jnp.exp(-x))


def _head0_mask(shape):
    return lax.broadcasted_iota(jnp.int32, shape, 1) % (2 * HEAD_DIM) < HEAD_DIM


def _norm_matmul_kernel(x_ref, g_ref, w_ref, o_ref, h_ref):
    @pl.when(pl.program_id(1) == 0)
    def _():
        x = x_ref[...]
        ms = jnp.mean(x * x, axis=-1, keepdims=True)
        h_ref[...] = (x * lax.rsqrt(ms + RMS_EPS) * g_ref[...]).astype(BF16)

    o_ref[...] = jnp.dot(h_ref[...], w_ref[...], preferred_element_type=F32)


def norm_matmul(x, g, w, tm, tn, single_buffer_x=False):
    M, K = x.shape
    N = w.shape[1]
    tm = min(tm, M)
    assert M % tm == 0 and N % tn == 0
    x_mode = dict(pipeline_mode=pl.Buffered(1)) if single_buffer_x else {}
    return pl.pallas_call(
        _norm_matmul_kernel,
        grid=(M // tm, N // tn),
        in_specs=[pl.BlockSpec((tm, K), lambda i, j: (i, 0), **x_mode),
                  pl.BlockSpec((1, K), lambda i, j: (0, 0)),
                  pl.BlockSpec((K, tn), lambda i, j: (0, j))],
        out_specs=pl.BlockSpec((tm, tn), lambda i, j: (i, j)),
        out_shape=jax.ShapeDtypeStruct((M, N), F32),
        scratch_shapes=[pltpu.VMEM((tm, K), BF16)],
        compiler_params=_cparams(("parallel", "arbitrary")),
        name="norm_matmul",
    )(x, g, w)


def _matmul_res_kernel(a_ref, w_ref, r_ref, o_ref, acc_ref, *, nk):
    k = pl.program_id(2)

    @pl.when(k == 0)
    def _():
        acc_ref[...] = r_ref[...]

    acc_ref[...] += jnp.dot(a_ref[...], w_ref[...], preferred_element_type=F32)

    @pl.when(k == nk - 1)
    def _():
        o_ref[...] = acc_ref[...]


def matmul_res(a, w, res, tm, tn, tk):
    M, K = a.shape
    N = w.shape[1]
    tm = min(tm, M)
    assert M % tm == 0 and N % tn == 0 and K % tk == 0
    nk = K // tk
    return pl.pallas_call(
        functools.partial(_matmul_res_kernel, nk=nk),
        grid=(M // tm, N // tn, nk),
        in_specs=[pl.BlockSpec((tm, tk), lambda i, j, k: (i, k)),
                  pl.BlockSpec((tk, tn), lambda i, j, k: (k, j)),
                  pl.BlockSpec((tm, tn), lambda i, j, k: (i, j))],
        out_specs=pl.BlockSpec((tm, tn), lambda i, j, k: (i, j)),
        out_shape=jax.ShapeDtypeStruct((M, N), F32),
        scratch_shapes=[pltpu.VMEM((tm, tn), F32)],
        compiler_params=_cparams(("parallel", "parallel", "arbitrary")),
        name="matmul_res",
    )(a, w, res)


def _rmsnorm_kernel(x_ref, g_ref, o_ref):
    x = x_ref[...]
    ms = jnp.mean(x * x, axis=-1, keepdims=True)
    o_ref[...] = x * lax.rsqrt(ms + RMS_EPS) * g_ref[...]


def rmsnorm(x, g, tm):
    M, K = x.shape
    tm = min(tm, M)
    return pl.pallas_call(
        _rmsnorm_kernel,
        grid=(M // tm,),
        in_specs=[pl.BlockSpec((tm, K), lambda i: (i, 0)), pl.BlockSpec((1, K), lambda i: (0, 0))],
        out_specs=pl.BlockSpec((tm, K), lambda i: (i, 0)),
        out_shape=jax.ShapeDtypeStruct((M, K), F32),
        compiler_params=_cparams(("parallel",)),
        name="rmsnorm",
    )(x, g)


def _rot_half(x, first_half):
    w = x.shape[1]
    return jnp.where(first_half, pltpu.roll(x, w - HEAD_DIM // 2, 1), pltpu.roll(x, HEAD_DIM // 2, 1))


def _rope_kernel(q_ref, k_ref, cos_ref, sin_ref, qo_ref, ko_ref):
    cos = jnp.concatenate([cos_ref[0]] * 4, axis=1)
    sin = jnp.concatenate([sin_ref[0]] * 4, axis=1)
    lane = lax.broadcasted_iota(jnp.int32, cos.shape, 1)
    first_half = (lane % HEAD_DIM) < (HEAD_DIM // 2)
    q = q_ref[0]
    k = k_ref[0]
    qo_ref[0] = q * cos + _rot_half(q, first_half) * sin
    ko_ref[0] = k * cos + _rot_half(k, first_half) * sin


def rope(proj3, cos_t, sin_t, tt):
    N, T, _ = proj3.shape
    tt = min(tt, T)
    tab = lambda n, i, j: (n, i, 0)
    qb = C_ATT // 512
    kb = (C_ATT + MIX) // 512
    out = jax.ShapeDtypeStruct((N, T, MIX), F32)
    return pl.pallas_call(
        _rope_kernel,
        grid=(N, T // tt, MIX // 512),
        in_specs=[pl.BlockSpec((1, tt, 512), lambda n, i, j: (n, i, qb + j)),
                  pl.BlockSpec((1, tt, 512), lambda n, i, j: (n, i, kb + j)),
                  pl.BlockSpec((1, tt, 128), tab),
                  pl.BlockSpec((1, tt, 128), tab)],
        out_specs=[pl.BlockSpec((1, tt, 512), lambda n, i, j: (n, i, j)),
                   pl.BlockSpec((1, tt, 512), lambda n, i, j: (n, i, j))],
        out_shape=[out, out],
        compiler_params=_cparams(("parallel", "parallel", "parallel")),
        name="rope",
    )(proj3, proj3, cos_t, sin_t)


def _prompt_attn_kernel(*refs):
    q_ref, k_ref, v_ref, cos_ref, sin_ref = refs[:5]
    o_ref, ko_all_ref, vo_all_ref, qs_ref, acc_ref, m_ref, l_ref = refs[-7:]
    ko_ref = ko_all_ref.at[0]
    vo_ref = vo_all_ref.at[0]
    S = q_ref.shape[1]
    B = BAND
    P = 2 * HEAD_DIM
    lane = lax.broadcasted_iota(jnp.int32, (B, P), 1)
    first_half = (lane % HEAD_DIM) < (HEAD_DIM // 2)
    h0 = _head0_mask((B, P))
    scale = HEAD_DIM ** -0.5

    def rope_rows(i, carry):
        rows = pl.ds(pl.multiple_of(i * B, B), B)
        cos = cos_ref[0, rows, :]
        sin = sin_ref[0, rows, :]
        q = q_ref[0, rows, :]
        k = k_ref[0, rows, :]
        qs_ref[rows, :] = q * cos + _rot_half(q, first_half) * sin
        ko_ref[0, rows, :] = k * cos + _rot_half(k, first_half) * sin
        vo_ref[0, rows, :] = v_ref[0, rows, :]
        return carry

    lax.fori_loop(0, S // B, rope_rows, 0)

    def attend(q, kk, vv, valid):
        ms, ls, os_ = [], [], []
        for h in range(2):
            qh = jnp.where(h0 if h == 0 else ~h0, q, 0.0).astype(BF16)
            s = lax.dot_general(qh, kk, NT, preferred_element_type=F32) * scale
            s = jnp.where(valid, s, NEG_INF)
            m = jnp.max(s, axis=-1, keepdims=True)
            e = jnp.exp(s - m)
            ms.append(m)
            ls.append(jnp.sum(e, axis=-1, keepdims=True))
            os_.append(jnp.dot(e.astype(BF16), vv, preferred_element_type=F32))
        return jnp.where(h0, ms[0], ms[1]), jnp.where(h0, ls[0], ls[1]), jnp.where(h0, os_[0], os_[1])

    def merge(rows, m_c, l_c, o_c):
        m_p = m_ref[rows, :]
        m_n = jnp.maximum(m_p, m_c)
        a_p = jnp.exp(m_p - m_n)
        a_c = jnp.exp(m_c - m_n)
        acc_ref[rows, :] = acc_ref[rows, :] * a_p + o_c * a_c
        l_ref[rows, :] = l_ref[rows, :] * a_p + l_c * a_c
        m_ref[rows, :] = m_n

    qi2 = lax.broadcasted_iota(jnp.int32, (B, 2 * B), 0)
    ki2 = lax.broadcasted_iota(jnp.int32, (B, 2 * B), 1)
    dist2 = qi2 + B - ki2
    band2 = (dist2 >= 0) & (dist2 <= B)
    qi1 = lax.broadcasted_iota(jnp.int32, (B, B), 0)
    ki1 = lax.broadcasted_iota(jnp.int32, (B, B), 1)
    causal1 = qi1 >= ki1

    def two_block_keys(cur, prev):
        kk = jnp.concatenate([ko_ref[0, prev, :], ko_ref[0, cur, :]], axis=0).astype(BF16)
        vv = jnp.concatenate([v_ref[0, prev, :], v_ref[0, cur, :]], axis=0).astype(BF16)
        return kk, vv

    d16 = DILATIONS[2]
    assert S == B * d16
    for r in range(d16):
        cur = pl.ds(r, B, stride=d16)
        m_c, l_c, o_c = attend(qs_ref[cur, :], ko_ref[0, cur, :].astype(BF16), v_ref[0, cur, :].astype(BF16), causal1)
        acc_ref[cur, :] = o_c
        m_ref[cur, :] = m_c
        l_ref[cur, :] = l_c

    d4 = DILATIONS[1]

    def dil4(ib, carry):
        base = pl.multiple_of(ib * (B * d4), B * d4)
        pbase = pl.multiple_of(jnp.maximum(ib - 1, 0) * (B * d4), B * d4)
        for r in range(d4):
            cur = pl.ds(base + r, B, stride=d4)
            prev = pl.ds(pbase + r, B, stride=d4)
            kk, vv = two_block_keys(cur, prev)
            m_c, l_c, o_c = attend(qs_ref[cur, :], kk, vv, band2 & ((ki2 >= B) | (ib > 0)))
            merge(cur, m_c, l_c, o_c)
        return carry

    lax.fori_loop(0, S // (B * d4), dil4, 0)

    unroll = 4

    def dil1(it, carry):
        for u in range(unroll):
            ib = it * unroll + u
            cur = pl.ds(pl.multiple_of(ib * B, B), B)
            prev = pl.ds(pl.multiple_of(jnp.maximum(ib - 1, 0) * B, B), B)
            kk, vv = two_block_keys(cur, prev)
            m_c, l_c, o_c = attend(qs_ref[cur, :], kk, vv, band2 & ((ki2 >= B) | (ib > 0)))
            m_p = m_ref[cur, :]
            m_n = jnp.maximum(m_p, m_c)
            a_p = jnp.exp(m_p - m_n)
            a_c = jnp.exp(m_c - m_n)
            acc = acc_ref[cur, :] * a_p + o_c * a_c
            l_n = l_ref[cur, :] * a_p + l_c * a_c
            o_ref[0, cur, :] = (acc / l_n).astype(BF16)
        return carry

    lax.fori_loop(0, S // (B * unroll), dil1, 0)


def prompt_attention(proj3, cos_t, sin_t, mixed, layer, depth, kv_all=None):
    N, S, _ = proj3.shape
    qb, kb, vb = C_ATT // 128, (C_ATT + MIX) // 128, (C_ATT + 2 * MIX) // 128
    col = lambda b: pl.BlockSpec((1, S, 128), lambda n, h: (n, 0, b + h))
    tab = pl.BlockSpec((1, S, 128), lambda n, h: (0, 0, 0))
    anyspec = pl.BlockSpec(memory_space=pl.ANY)
    kv = jax.ShapeDtypeStruct((depth, N, S, MIX), F32)
    kv_spec = pl.BlockSpec((1, 1, S, 128), lambda n, h: (layer, n, 0, h))
    args = [proj3, proj3, proj3, cos_t, sin_t, mixed]
    in_specs = [col(qb), col(kb), col(vb), tab, tab, anyspec]
    aliases = {5: 0}
    if kv_all is not None:
        args += list(kv_all)
        in_specs += [anyspec, anyspec]
        aliases.update({6: 1, 7: 2})
    return pl.pallas_call(
        _prompt_attn_kernel,
        grid=(N, HEADS // 2),
        in_specs=in_specs,
        out_specs=[col(M_ATT // 128), kv_spec, kv_spec],
        out_shape=[jax.ShapeDtypeStruct(mixed.shape, mixed.dtype), kv, kv],
        scratch_shapes=[pltpu.VMEM((S, 128), F32)] * 4,
        input_output_aliases=aliases,
        compiler_params=_cparams(("parallel", "parallel")),
        name="prompt_attention",
    )(*args)


def _sample_attn_kernel(*refs, t_valid):
    q_ref, kn_ref, vn_ref, kb_ref, vb_ref = refs[:5]
    o_ref, ks_all_ref, vs_all_ref = refs[-3:]
    ks_ref = ks_all_ref.at[0]
    vs_ref = vs_all_ref.at[0]
    TQ = q_ref.shape[1]
    M = kb_ref.shape[1]
    TN_ = kn_ref.shape[1]
    h0 = _head0_mask((TQ, 2 * HEAD_DIM))
    scale = HEAD_DIM ** -0.5

    def counts(delta, in_range):
        c = jnp.zeros(delta.shape, F32)
        for win, dil in zip(WINDOWS, DILATIONS):
            c = c + jnp.where((delta % dil == 0) & (delta <= win) & in_range, 1.0, 0.0)
        return c

    tq = lax.broadcasted_iota(jnp.int32, (TQ, M), 0)
    kb_i = lax.broadcasted_iota(jnp.int32, (TQ, M), 1)
    d_buf = M + tq - kb_i
    c_buf = counts(d_buf, d_buf >= 0)
    tq2 = lax.broadcasted_iota(jnp.int32, (TQ, TN_), 0)
    tn2 = lax.broadcasted_iota(jnp.int32, (TQ, TN_), 1)
    d_new = tq2 - tn2
    c_new = counts(d_new, (d_new >= 0) & (tn2 < t_valid))

    row8 = lax.broadcasted_iota(jnp.int32, (8, 2 * HEAD_DIM), 0)
    for p in range(4):
        sl = slice(128 * p, 128 * (p + 1))
        qp = q_ref[0, :, sl]
        kb = kb_ref[0, :, sl]
        vb = vb_ref[0, :, sl]
        kn = kn_ref[0, :, sl]
        vn = vn_ref[0, :, sl]
        kb16, vb16, kn16, vn16 = kb.astype(BF16), vb.astype(BF16), kn.astype(BF16), vn.astype(BF16)
        outs = []
        for h in range(2):
            qh = jnp.where(h0 if h == 0 else ~h0, qp, 0.0).astype(BF16)
            s_b = lax.dot_general(qh, kb16, NT, preferred_element_type=F32) * scale
            s_n = lax.dot_general(qh, kn16, NT, preferred_element_type=F32) * scale
            s_b = jnp.where(c_buf > 0, s_b, NEG_INF)
            s_n = jnp.where(c_new > 0, s_n, NEG_INF)
            m = jnp.maximum(jnp.max(s_b, axis=-1, keepdims=True), jnp.max(s_n, axis=-1, keepdims=True))
            w_b = c_buf * jnp.exp(s_b - m)
            w_n = c_new * jnp.exp(s_n - m)
            l = jnp.sum(w_b, axis=-1, keepdims=True) + jnp.sum(w_n, axis=-1, keepdims=True)
            o = (jnp.dot(w_b.astype(BF16), vb16, preferred_element_type=F32)
                 + jnp.dot(w_n.astype(BF16), vn16, preferred_element_type=F32))
            outs.append(o / l)
        o_ref[0, :, sl] = jnp.where(h0, outs[0], outs[1]).astype(BF16)

        for src, new, dst in ((kb, kn, ks_ref), (vb, vn, vs_ref)):
            rolled = pltpu.roll(src, M - t_valid, 0)
            new_r = pltpu.roll(new[0:8], 8 - t_valid, 0)
            dst[0, 0:M - 8, sl] = rolled[0:M - 8]
            dst[0, M - 8:M, sl] = jnp.where(row8 >= 8 - t_valid, new_r, rolled[M - 8:M])


def sample_attention(q_rot, k_new, v_new, k_buf, v_buf, t_valid, layer, depth, kv_all=None):
    N, TQ, _ = q_rot.shape
    M = k_buf.shape[1]
    TN_ = k_new.shape[1]
    assert t_valid <= 8 <= TQ
    blk = lambda r: pl.BlockSpec((1, r, 512), lambda n, j: (n, 0, j))
    kv = jax.ShapeDtypeStruct((depth, N, M, MIX), F32)
    kv_spec = pl.BlockSpec((1, 1, M, 512), lambda n, j: (layer, n, 0, j))
    args = [q_rot, k_new, v_new, k_buf, v_buf]
    in_specs = [blk(TQ), blk(TN_), blk(TN_), blk(M), blk(M)]
    aliases = {}
    if kv_all is not None:
        args += list(kv_all)
        in_specs += [pl.BlockSpec(memory_space=pl.ANY)] * 2
        aliases = {5: 1, 6: 2}
    return pl.pallas_call(
        functools.partial(_sample_attn_kernel, t_valid=t_valid),
        grid=(N, MIX // 512),
        in_specs=in_specs,
        out_specs=[blk(TQ), kv_spec, kv_spec],
        out_shape=[jax.ShapeDtypeStruct((N, TQ, MIX), BF16), kv, kv],
        input_output_aliases=aliases,
        compiler_params=_cparams(("parallel", "parallel")),
        name="sample_attention",
    )(*args)


def _glu_kernel(a_ref, g_ref, o_ref):
    o_ref[0] = a_ref[0] * _sigmoid(g_ref[0])


def glu(proj3, tt):
    N, T, _ = proj3.shape
    tt = min(tt, T)
    return pl.pallas_call(
        _glu_kernel,
        grid=(N, T // tt),
        in_specs=[pl.BlockSpec((1, tt, CONV_DIM), lambda n, i: (n, i, C_CONV // CONV_DIM)),
                  pl.BlockSpec((1, tt, CONV_DIM), lambda n, i: (n, i, C_CONV // CONV_DIM + 1))],
        out_specs=pl.BlockSpec((1, tt, CONV_DIM), lambda n, i: (n, i, 0)),
        out_shape=jax.ShapeDtypeStruct((N, T, CONV_DIM), F32),
        compiler_params=_cparams(("parallel", "parallel")),
        name="glu",
    )(proj3, proj3)


def _dwconv_kernel(u_ref, uh_ref, h0_ref, w_ref, b_ref, lw_ref, lb_ref, *rest, single_tile):
    o_ref, ext_ref, y_ref = rest[-3:]
    tt = u_ref.shape[1]
    i = pl.program_id(1)

    @pl.when(i == 0)
    def _():
        ext_ref[0:CONV_HALO, :] = h0_ref[0]

    if not single_tile:
        @pl.when(i > 0)
        def _():
            ext_ref[0:CONV_HALO, :] = uh_ref[0]

    ext_ref[CONV_HALO:, :] = u_ref[0]
    lead = CONV_HALO - (CONV_WIDTH - 1)
    rows = min(16, tt)
    for r0 in range(0, tt, rows):
        for c0 in range(0, CONV_DIM, 512):
            acc = jnp.broadcast_to(b_ref[:, c0:c0 + 512], (rows, 512))
            for j in range(CONV_WIDTH):
                acc = acc + ext_ref[r0 + lead + j:r0 + lead + j + rows, c0:c0 + 512] * w_ref[j:j + 1, c0:c0 + 512]
            y_ref[r0:r0 + rows, c0:c0 + 512] = acc
    y = y_ref[...]
    mu = jnp.mean(y, axis=-1, keepdims=True)
    var = jnp.mean(jnp.square(y - mu), axis=-1, keepdims=True)
    c = (y - mu) * lax.rsqrt(var + LN_EPS) * lw_ref[...] + lb_ref[...]
    o_ref[0] = (c * _sigmoid(c)).astype(BF16)


def dwconv_ln_silu(u, halo0, w, b, lw, lb, tt, mixed=None):
    N, T, C = u.shape
    tt = min(tt, T)
    hb = tt // CONV_HALO if tt >= CONV_HALO else 1
    in_specs = [pl.BlockSpec((1, tt, C), lambda n, i: (n, i, 0)),
                pl.BlockSpec((1, min(CONV_HALO, T), C), lambda n, i: (n, jnp.maximum(i * hb - 1, 0), 0)),
                pl.BlockSpec((1, CONV_HALO, C), lambda n, i: (n, 0, 0)),
                pl.BlockSpec((32, C), lambda n, i: (0, 0)),
                pl.BlockSpec((1, C), lambda n, i: (0, 0)),
                pl.BlockSpec((1, C), lambda n, i: (0, 0)),
                pl.BlockSpec((1, C), lambda n, i: (0, 0))]
    args = [u, u, halo0, w, b, lw, lb]
    if mixed is None:
        out_shape = jax.ShapeDtypeStruct((N, T, C), BF16)
        out_spec = pl.BlockSpec((1, tt, C), lambda n, i: (n, i, 0))
        aliases = {}
    else:
        in_specs.append(pl.BlockSpec(memory_space=pl.ANY))
        args.append(mixed)
        out_shape = jax.ShapeDtypeStruct(mixed.shape, mixed.dtype)
        out_spec = pl.BlockSpec((1, tt, C), lambda n, i: (n, i, M_CONV // C))
        aliases = {7: 0}
    return pl.pallas_call(
        functools.partial(_dwconv_kernel, single_tile=(T == tt)),
        grid=(N, T // tt),
        in_specs=in_specs,
        out_specs=out_spec,
        out_shape=out_shape,
        scratch_shapes=[pltpu.VMEM((CONV_HALO + tt, C), F32), pltpu.VMEM((tt, C), F32)],
        input_output_aliases=aliases,
        compiler_params=_cparams(("parallel", "arbitrary")),
        name="dwconv_ln_silu",
    )(*args)


def _ffn_conv_act(ext_ref, wg_ref, wv_ref, bg_ref, bv_ref, store, tt):
    lead = FFN_HALO - (FFN_CONV_WIDTH - 1)
    rows = min(32, tt)
    for r0 in range(0, tt, rows):
        ys = []
        for s, w_ref, b_ref in ((0, wg_ref, bg_ref), (1, wv_ref, bv_ref)):
            acc = jnp.broadcast_to(b_ref[...], (rows, b_ref.shape[1]))
            for j in range(FFN_CONV_WIDTH):
                acc = acc + ext_ref[s, r0 + lead + j:r0 + lead + j + rows, :] * w_ref[j:j + 1, :]
            ys.append(acc)
        gate, val = ys
        store(r0, rows, (gate * _sigmoid(gate) * val).astype(BF16))


def _ffn_act_kernel(ug_ref, uv_ref, hg_ref, hv_ref, h0g_ref, h0v_ref, wg_ref, wv_ref, bg_ref, bv_ref, o_ref, ext_ref):
    tt = ug_ref.shape[1]
    i = pl.program_id(1)

    @pl.when(i == 0)
    def _():
        ext_ref[0, 0:FFN_HALO, :] = h0g_ref[0]
        ext_ref[1, 0:FFN_HALO, :] = h0v_ref[0]

    @pl.when(i > 0)
    def _():
        ext_ref[0, 0:FFN_HALO, :] = hg_ref[0]
        ext_ref[1, 0:FFN_HALO, :] = hv_ref[0]

    ext_ref[0, FFN_HALO:, :] = ug_ref[0]
    ext_ref[1, FFN_HALO:, :] = uv_ref[0]

    def store(r0, rows, val):
        o_ref[0, r0:r0 + rows, :] = val

    _ffn_conv_act(ext_ref, wg_ref, wv_ref, bg_ref, bv_ref, store, tt)


def ffn_act(u3, halo0, w, b, tt, tf):
    N, T, F2 = u3.shape
    F = F2 // 2
    tt = min(tt, T)
    nf = F // tf
    hb = tt // FFN_HALO
    g = lambda n, i, j: (n, i, j)
    v = lambda n, i, j: (n, i, j + nf)
    hg = lambda n, i, j: (n, jnp.maximum(i * hb - 1, 0), j)
    hv = lambda n, i, j: (n, jnp.maximum(i * hb - 1, 0), j + nf)
    return pl.pallas_call(
        _ffn_act_kernel,
        grid=(N, T // tt, nf),
        in_specs=[pl.BlockSpec((1, tt, tf), g), pl.BlockSpec((1, tt, tf), v),
                  pl.BlockSpec((1, FFN_HALO, tf), hg), pl.BlockSpec((1, FFN_HALO, tf), hv),
                  pl.BlockSpec((1, FFN_HALO, tf), lambda n, i, j: (n, 0, j)),
                  pl.BlockSpec((1, FFN_HALO, tf), lambda n, i, j: (n, 0, j + nf)),
                  pl.BlockSpec((8, tf), lambda n, i, j: (0, j)), pl.BlockSpec((8, tf), lambda n, i, j: (0, j + nf)),
                  pl.BlockSpec((1, tf), lambda n, i, j: (0, j)), pl.BlockSpec((1, tf), lambda n, i, j: (0, j + nf))],
        out_specs=pl.BlockSpec((1, tt, tf), g),
        out_shape=jax.ShapeDtypeStruct((N, T, F), BF16),
        scratch_shapes=[pltpu.VMEM((2, FFN_HALO + tt, tf), F32)],
        compiler_params=_cparams(("parallel", "parallel", "parallel")),
        name="ffn_act",
    )(u3, u3, u3, u3, halo0, halo0, w, w, b, b)


def _ffn_fused_kernel(x_ref, g_ref, wg_ref, wv_ref, wd_ref, cwg_ref, cwv_ref, cbg_ref, cbv_ref, h0g_ref, h0v_ref,
                      o_ref, st_ref, h_ref, ext_ref, carry_ref, act_ref, *, tiles_per_seq):
    i = pl.program_id(0)
    j = pl.program_id(1)
    tm = x_ref.shape[0]

    @pl.when(j == 0)
    def _():
        x = x_ref[...]
        ms = jnp.mean(x * x, axis=-1, keepdims=True)
        h_ref[...] = (x * lax.rsqrt(ms + RMS_EPS) * g_ref[...]).astype(BF16)
        o_ref[...] = x

    h = h_ref[...]
    ug = jnp.dot(h, wg_ref[...], preferred_element_type=F32)
    uv = jnp.dot(h, wv_ref[...], preferred_element_type=F32)
    seq_start = (i % tiles_per_seq) == 0
    ext_ref[0, 0:FFN_HALO, :] = jnp.where(seq_start, h0g_ref[0], carry_ref[j, 0])
    ext_ref[1, 0:FFN_HALO, :] = jnp.where(seq_start, h0v_ref[0], carry_ref[j, 1])
    ext_ref[0, FFN_HALO:, :] = ug
    ext_ref[1, FFN_HALO:, :] = uv
    carry_ref[j, 0] = ug[tm - FFN_HALO:, :]
    carry_ref[j, 1] = uv[tm - FFN_HALO:, :]

    def store(r0, rows, val):
        act_ref[r0:r0 + rows, :] = val

    _ffn_conv_act(ext_ref, cwg_ref, cwv_ref, cbg_ref, cbv_ref, store, tm)
    o_ref[...] += jnp.dot(act_ref[...], wd_ref[...], preferred_element_type=F32)

    @pl.when((j == pl.num_programs(1) - 1) & (i % tiles_per_seq == tiles_per_seq - 1))
    def _():
        st_ref[0] = carry_ref[...]


def ffn_fused(x, g, w_up, w_down, cw, cb, halo0, seq_len, tm, tf):
    M, D = x.shape
    F = w_down.shape[0]
    assert M % tm == 0 and seq_len % tm == 0 and F % tf == 0
    nf = F // tf
    tps = seq_len // tm
    gate = lambda i, j: (0, j)
    val = lambda i, j: (0, j + nf)
    return pl.pallas_call(
        functools.partial(_ffn_fused_kernel, tiles_per_seq=tps),
        grid=(M // tm, nf),
        in_specs=[pl.BlockSpec((tm, D), lambda i, j: (i, 0), pipeline_mode=pl.Buffered(1)),
                  pl.BlockSpec((1, D), lambda i, j: (0, 0)),
                  pl.BlockSpec((D, tf), gate),
                  pl.BlockSpec((D, tf), val),
                  pl.BlockSpec((tf, D), lambda i, j: (j, 0)),
                  pl.BlockSpec((8, tf), gate), pl.BlockSpec((8, tf), val),
                  pl.BlockSpec((1, tf), gate), pl.BlockSpec((1, tf), val),
                  pl.BlockSpec((1, FFN_HALO, tf), lambda i, j: (i // tps, 0, j)),
                  pl.BlockSpec((1, FFN_HALO, tf), lambda i, j: (i // tps, 0, j + nf))],
        out_specs=[pl.BlockSpec((tm, D), lambda i, j: (i, 0)),
                   pl.BlockSpec((1, nf, 2, FFN_HALO, tf), lambda i, j: (i // tps, 0, 0, 0, 0))],
        out_shape=[jax.ShapeDtypeStruct((M, D), F32),
                   jax.ShapeDtypeStruct((M // seq_len, nf, 2, FFN_HALO, tf), F32)],
        scratch_shapes=[pltpu.VMEM((tm, D), BF16), pltpu.VMEM((2, FFN_HALO + tm, tf), F32),
                        pltpu.VMEM((nf, 2, FFN_HALO, tf), F32), pltpu.VMEM((tm, tf), BF16)],
        compiler_params=_cparams(("arbitrary", "arbitrary")),
        name="ffn_fused",
    )(x, g, w_up, w_up, w_down, cw, cw, cb, cb, halo0, halo0)


def _rwkv_kernel(r_ref, k_ref, v_ref, xg_ref, xwa_ref,
                 sr_ref, sk_ref, sv_ref, sxg_ref, sxwa_ref,
                 mur_ref, muk_ref, muv_ref, muxg_ref, muxwa_ref,
                 w0_ref, a0_ref, kk_ref, ka_ref, rk_ref, lnw_ref, lnb_ref,
                 w2_ref, a2_ref, g2_ref, s0_ref,
                 o_ref, so_ref,
                 S_ref, pr_ref, pk_ref, pv_ref, pxg_ref, pxwa_ref, *, t_valid, t_total):
    c = pl.program_id(2)
    Tc = r_ref.shape[1]
    PW = 2 * HEAD_DIM
    n_pairs = r_ref.shape[2] // PW

    @pl.when(c == 0)
    def _():
        S_ref[...] = s0_ref[0]
        pr_ref[...] = sr_ref[0]
        pk_ref[...] = sk_ref[0]
        pv_ref[...] = sv_ref[0]
        pxg_ref[...] = sxg_ref[0]
        pxwa_ref[...] = sxwa_ref[0]

    def lerp(p_ref, prev_ref, mu_ref):
        p = p_ref[0]
        row = lax.broadcasted_iota(jnp.int32, p.shape, 0)
        shifted = jnp.where(row == 0, prev_ref[...], pltpu.roll(p, 1, 0))
        prev_ref[...] = p[Tc - 1:Tc, :]
        return p + (shifted - p) * mu_ref[...]

    xr = lerp(r_ref, pr_ref, mur_ref)
    xk = lerp(k_ref, pk_ref, muk_ref)
    xv = lerp(v_ref, pv_ref, muv_ref)
    xg = lerp(xg_ref, pxg_ref, muxg_ref)
    xwa = lerp(xwa_ref, pxwa_ref, muxwa_ref)
    xw = xwa[:, :DECAY_LORA]
    xa = xwa[:, DECAY_LORA:]

    h0p = _head0_mask((Tc, PW))

    def head_sum(x):
        parts = []
        for p in range(n_pairs):
            xp = x[:, PW * p:PW * (p + 1)]
            s_0 = jnp.sum(jnp.where(h0p, xp, 0.0), axis=-1, keepdims=True)
            s_1 = jnp.sum(jnp.where(h0p, 0.0, xp), axis=-1, keepdims=True)
            parts.append(jnp.where(h0p, s_0, s_1))
        return parts[0] if n_pairs == 1 else jnp.concatenate(parts, axis=1)

    def mm(x, y_):
        return jnp.dot(x, y_, preferred_element_type=F32)

    z = w0_ref[...] + mm(jnp.tanh(xw).astype(BF16), w2_ref[...])
    w_log = jnp.minimum(z, 0.0) - jnp.log(1.0 + jnp.exp(-jnp.abs(z))) - 0.5
    ld = -jnp.exp(w_log)
    a = _sigmoid(a0_ref[...] + mm(xa.astype(BF16), a2_ref[...]))
    g = mm(_sigmoid(xg).astype(BF16), g2_ref[...])

    kk = xk * kk_ref[...]
    kk = kk * lax.rsqrt(jnp.maximum(head_sum(kk * kk), 1e-24))
    kmod = xk * (1.0 + (a - 1.0) * ka_ref[...])
    avec = -kk
    bvec = kk * a
    bonus = head_sum(xr * kmod * rk_ref[...]) * xv
    vval = xv

    if t_valid < t_total:
        row = lax.broadcasted_iota(jnp.int32, (Tc, n_pairs * PW), 0) + c * Tc
        ok = row < t_valid
        ld = jnp.where(ok, ld, 0.0)
        avec = jnp.where(ok, avec, 0.0)
        bvec = jnp.where(ok, bvec, 0.0)
        kmod_s = jnp.where(ok, kmod, 0.0)
        vval = jnp.where(ok, vval, 0.0)
    else:
        kmod_s = kmod

    C = CHUNK
    tri_r = lax.broadcasted_iota(jnp.int32, (C, C), 0)
    tri_c = lax.broadcasted_iota(jnp.int32, (C, C), 1)
    tril = jnp.where(tri_r >= tri_c, 1.0, 0.0).astype(F32)
    it = lax.broadcasted_iota(jnp.int32, (2 * C, 2 * C), 0) % C
    js = lax.broadcasted_iota(jnp.int32, (2 * C, 2 * C), 1) % C
    strict = it > js
    incl = it >= js
    h0c = _head0_mask((C, 2 * HEAD_DIM))

    def stack(x):
        return jnp.concatenate([jnp.where(h0c, x, 0.0), jnp.where(h0c, 0.0, x)], axis=0)

    n_chunks = Tc // C
    grp = 2 if n_chunks % 2 == 0 else 1
    gw = 2 * C * grp
    eye_g = jnp.where(lax.broadcasted_iota(jnp.int32, (gw, gw), 0)
                      == lax.broadcasted_iota(jnp.int32, (gw, gw), 1), 1.0, 0.0).astype(F32)
    zero_blk = jnp.zeros((2 * C, 2 * C), F32)

    def bdiag(blocks):
        if len(blocks) == 1:
            return blocks[0]
        return jnp.concatenate([jnp.concatenate([blocks[0], zero_blk], axis=1),
                                jnp.concatenate([zero_blk, blocks[1]], axis=1)], axis=0)

    def cat(blocks):
        return blocks[0] if len(blocks) == 1 else jnp.concatenate(blocks, axis=0)

    a16, r32, bb16, kb16, v16, gts, Ls, AKs, RBs, RKs = ({} for _ in range(10))
    units = [(p, j) for j in range(n_chunks) for p in range(n_pairs)]
    for p, j in units:
        sl = (slice(C * j, C * (j + 1)), slice(PW * p, PW * (p + 1)))
        ldc = ld[sl]
        cs = jnp.dot(tril, ldc, precision=lax.Precision.HIGHEST, preferred_element_type=F32)
        tot = cs[C - 1:C, :]
        eg = jnp.exp(cs)
        egi = jnp.exp(-cs)
        ege = jnp.exp(cs - ldc)
        et = jnp.exp(tot - cs)
        gts[p, j] = jnp.exp(tot)
        a_s = stack(avec[sl] * ege).astype(BF16)
        r_s = stack(xr[sl] * eg)
        b_s = stack(bvec[sl] * egi).astype(BF16)
        k_s = stack(kmod_s[sl] * egi).astype(BF16)
        sc = lax.dot_general(jnp.concatenate([a_s, r_s.astype(BF16)], axis=0), jnp.concatenate([b_s, k_s], axis=0), NT,
                             preferred_element_type=F32)
        a16[p, j] = a_s
        r32[p, j] = r_s
        bb16[p, j] = stack(bvec[sl] * et).astype(BF16)
        kb16[p, j] = stack(kmod_s[sl] * et).astype(BF16)
        v16[p, j] = stack(vval[sl]).astype(BF16)
        Ls[p, j] = jnp.where(strict, sc[:2 * C, :2 * C], 0.0)
        AKs[p, j] = jnp.where(strict, sc[:2 * C, 2 * C:], 0.0)
        RBs[p, j] = jnp.where(incl, sc[2 * C:, :2 * C], 0.0)
        RKs[p, j] = jnp.where(incl, sc[2 * C:, 2 * C:], 0.0)

    groups = [[(p, j) for j in range(g0, g0 + grp)] for g0 in range(0, n_chunks, grp) for p in range(n_pairs)]
    Ps = [bdiag([Ls[c_] for c_ in g_]) for g_ in groups]
    Tms = [eye_g + p_ for p_ in Ps]
    for _ in range(C.bit_length() - 2):
        for gi in range(len(groups)):
            P16 = Ps[gi].astype(BF16)
            Ps[gi] = mm(P16, P16)
            Tms[gi] = Tms[gi] + mm(Tms[gi].astype(BF16), Ps[gi].astype(BF16))

    RAs, YNs, Gs, Ns = {}, {}, {}, {}
    for gi, g_ in enumerate(groups):
        T16 = Tms[gi].astype(BF16)
        A2 = cat([a16[c_] for c_ in g_])
        V2 = cat([v16[c_] for c_ in g_])
        R2 = cat([r32[c_] for c_ in g_])
        RB16 = bdiag([RBs[c_] for c_ in g_]).astype(BF16)
        RK16 = bdiag([RKs[c_] for c_ in g_]).astype(BF16)
        TA16 = mm(T16, A2).astype(BF16)
        TV16 = mm(T16, mm(bdiag([AKs[c_] for c_ in g_]).astype(BF16), V2).astype(BF16)).astype(BF16)
        RA = R2 + mm(RB16, TA16)
        YN = mm(jnp.concatenate([RB16, RK16], axis=1), jnp.concatenate([TV16, V2], axis=0))
        for q, c_ in enumerate(g_):
            rs = slice(2 * C * q, 2 * C * (q + 1))
            RAs[c_] = RA[rs].astype(BF16)
            YNs[c_] = YN[rs]
            Gs[c_] = lax.dot_general(TA16[rs], bb16[c_], TN, preferred_element_type=F32).astype(BF16)
            Ns[c_] = lax.dot_general(jnp.concatenate([TV16[rs], v16[c_]], axis=0),
                                     jnp.concatenate([bb16[c_], kb16[c_]], axis=0), TN, preferred_element_type=F32)

    ys = {}
    Ss = [S_ref[p] for p in range(n_pairs)]
    for p, j in units:
        S16 = Ss[p].astype(BF16)
        Y = lax.dot_general(RAs[p, j], S16, NT, preferred_element_type=F32) + YNs[p, j]
        ys[p, j] = Y[:C] + Y[C:]
        Ss[p] = Ss[p] * gts[p, j] + mm(S16, Gs[p, j]) + Ns[p, j]
    for p in range(n_pairs):
        S_ref[p] = Ss[p]
    y_pairs = [cat([ys[p, j] for j in range(n_chunks)]) for p in range(n_pairs)]
    y = y_pairs[0] if n_pairs == 1 else jnp.concatenate(y_pairs, axis=1)

    mu = head_sum(y) * (1.0 / HEAD_DIM)
    var = head_sum(jnp.square(y - mu)) * (1.0 / HEAD_DIM)
    yn = (y - mu) * lax.rsqrt(var + GN_EPS) * lnw_ref[...] + lnb_ref[...]
    o_ref[0] = ((yn + bonus) * g).astype(BF16)

    @pl.when(c == pl.num_programs(2) - 1)
    def _():
        for p in range(n_pairs):
            so_ref[0, p] = Ss[p]


def rwkv_time_mix(proj3, shift0, s0, prm, t_valid, tc, out_width=MIX, pairs=2):
    N, T, _ = proj3.shape
    tc = min(tc, T)
    W = 2 * HEAD_DIM * pairs
    assert T % tc == 0 and tc % CHUNK == 0 and MIX % W == 0 and C_RWKV % W == 0
    rb = C_RWKV // W
    nb = MIX // W
    pspec = lambda w, off: pl.BlockSpec((1, tc, w), lambda n, h, c: (n, c, off(h)))
    sspec = lambda w, off: pl.BlockSpec((1, 1, w), lambda n, h, c: (n, 0, off(h)))
    mspec = lambda w, off: pl.BlockSpec((1, w), lambda n, h, c: (0, off(h)))
    hspec = pl.BlockSpec((1, W), lambda n, h, c: (0, h))
    xg_p = (C_RWKV + 3 * MIX) // GATE_PAD
    xwa_p = (C_RWKV + 3 * MIX + GATE_PAD) // 256
    xg_s = 3 * MIX // GATE_PAD
    xwa_s = (3 * MIX + GATE_PAD) // 256
    in_specs = [
        pspec(W, lambda h: rb + h), pspec(W, lambda h: rb + nb + h), pspec(W, lambda h: rb + 2 * nb + h),
        pspec(GATE_PAD, lambda h: xg_p), pspec(256, lambda h: xwa_p),
        sspec(W, lambda h: h), sspec(W, lambda h: nb + h), sspec(W, lambda h: 2 * nb + h),
        sspec(GATE_PAD, lambda h: xg_s), sspec(256, lambda h: xwa_s),
        mspec(W, lambda h: h), mspec(W, lambda h: nb + h), mspec(W, lambda h: 2 * nb + h),
        mspec(GATE_PAD, lambda h: xg_s), mspec(256, lambda h: xwa_s),
        hspec, hspec, hspec, hspec, hspec, hspec, hspec,
        pl.BlockSpec((DECAY_LORA, W), lambda n, h, c: (0, h)),
        pl.BlockSpec((AAA_LORA, W), lambda n, h, c: (0, h)),
        pl.BlockSpec((GATE_PAD, W), lambda n, h, c: (0, h)),
        pl.BlockSpec((1, pairs, 128, 128), lambda n, h, c: (n, h, 0, 0)),
    ]
    args = ([proj3] * 5 + [shift0] * 5 + [prm['mu']] * 5
            + [prm['w0'], prm['a0'], prm['k_k'], prm['k_a'], prm['r_k'], prm['ln_w'], prm['ln_b'],
               prm['w2'], prm['a2'], prm['g2'], s0])
    return pl.pallas_call(
        functools.partial(_rwkv_kernel, t_valid=t_valid, t_total=T),
        grid=(N, nb, T // tc),
        in_specs=in_specs,
        out_specs=[pl.BlockSpec((1, tc, W), lambda n, h, c: (n, c, M_RWKV // W + h)),
                   pl.BlockSpec((1, pairs, 128, 128), lambda n, h, c: (n, h, 0, 0))],
        out_shape=[jax.ShapeDtypeStruct((N, T, out_width), BF16),
                   jax.ShapeDtypeStruct((N, HEADS // 2, 128, 128), F32)],
        scratch_shapes=[pltpu.VMEM((pairs, 128, 128), F32), pltpu.VMEM((1, W), F32), pltpu.VMEM((1, W), F32),
                        pltpu.VMEM((1, W), F32), pltpu.VMEM((1, GATE_PAD), F32), pltpu.VMEM((1, 256), F32)],
        compiler_params=_cparams(("parallel", "parallel", "arbitrary")),
        name="rwkv7_time_mix",
    )(*args)


def _permute_rwkv_cols(t):
    pad = jnp.zeros(t.shape[:-1] + (GATE_PAD - GATE_LORA,), t.dtype)
    o = 3 * MIX
    return jnp.concatenate([t[..., :o], t[..., o + DECAY_LORA + AAA_LORA:], pad, t[..., o:o + DECAY_LORA + AAA_LORA]], axis=-1)


def _unpermute_rwkv_cols(t):
    o = 3 * MIX
    return jnp.concatenate([t[..., :o], t[..., o + GATE_PAD:], t[..., o:o + GATE_LORA]], axis=-1)


def _layer_params(l, norm_mix, w_in, rwkv_mu, rwkv_w0, rwkv_w2, rwkv_a0, rwkv_a2, rwkv_g2, rwkv_k_k, rwkv_k_a,
                  rwkv_r_k, rwkv_ln_w, rwkv_ln_b, conv_w, conv_b, conv_ln_w, conv_ln_b, w_out, norm_ffn, w_up,
                  ffn_conv_w, ffn_conv_b, w_down):
    wi = w_in[l]
    w_rwkv = _permute_rwkv_cols(wi[:, :RWKV_PROJ])
    w_att = wi[:, RWKV_PROJ:RWKV_PROJ + ATT_PROJ]
    w_conv = wi[:, RWKV_PROJ + ATT_PROJ:]
    tail = jnp.zeros((D_MODEL, NP - C_RWKV - RWKV_PAD), wi.dtype)
    row = lambda t: t.reshape(1, -1)
    return {
        'norm_mix': row(norm_mix[l]),
        'w_in': jnp.concatenate([w_conv, w_att, w_rwkv, tail], axis=1).astype(BF16),
        'mu': _permute_rwkv_cols(row(rwkv_mu[l])),
        'w0': row(rwkv_w0[l]), 'a0': row(rwkv_a0[l]), 'k_k': row(rwkv_k_k[l]), 'k_a': row(rwkv_k_a[l]),
        'r_k': row(rwkv_r_k[l]), 'ln_w': row(rwkv_ln_w[l]), 'ln_b': row(rwkv_ln_b[l]),
        'w2': rwkv_w2[l].astype(BF16), 'a2': rwkv_a2[l].astype(BF16),
        'g2': jnp.pad(rwkv_g2[l], ((0, GATE_PAD - GATE_LORA), (0, 0))).astype(BF16),
        'conv_w': jnp.pad(conv_w[l], ((0, 32 - CONV_WIDTH), (0, 0))), 'conv_b': row(conv_b[l]),
        'conv_ln_w': row(conv_ln_w[l]), 'conv_ln_b': row(conv_ln_b[l]),
        'w_out': w_out[l].astype(BF16),
        'norm_ffn': row(norm_ffn[l]),
        'w_up': w_up[l].astype(BF16),
        'ffn_conv_w': jnp.pad(ffn_conv_w[l], ((0, 8 - FFN_CONV_WIDTH), (0, 0))), 'ffn_conv_b': row(ffn_conv_b[l]),
        'w_down': w_down[l].astype(BF16),
    }


def _rope_tables(pos):
    half = HEAD_DIM // 2
    inv_freq = ROPE_THETA ** (-jnp.arange(half, dtype=F32) * 2.0 / HEAD_DIM)
    ang = pos.astype(F32)[..., None] * inv_freq
    cos, sin = jnp.cos(ang), jnp.sin(ang)
    return jnp.concatenate([cos, cos, cos, cos], axis=-1), jnp.concatenate([-sin, sin, -sin, sin], axis=-1)


def _state_to_blockdiag(s):
    N = s.shape[0]
    s = s.reshape(N, HEADS // 2, 2, HEAD_DIM, HEAD_DIM)
    z = jnp.zeros_like(s[:, :, 0])
    top = jnp.concatenate([s[:, :, 0], z], axis=-1)
    bot = jnp.concatenate([z, s[:, :, 1]], axis=-1)
    return jnp.concatenate([top, bot], axis=-2)


def _blockdiag_to_state(s):
    N = s.shape[0]
    return jnp.stack([s[:, :, :HEAD_DIM, :HEAD_DIM], s[:, :, HEAD_DIM:, HEAD_DIM:]], axis=2).reshape(
        N, HEADS, HEAD_DIM, HEAD_DIM)


def _last_rows(buf, u, t_valid):
    keep = buf.shape[1]
    if t_valid >= keep:
        return u[:, t_valid - keep:t_valid]
    return jnp.concatenate([buf[:, t_valid:], u[:, :t_valid]], axis=1)


def _front_pad(buf, rows):
    return jnp.pad(buf, ((0, 0), (rows - buf.shape[1], 0), (0, 0)))


def _prompt_layer(x2, N, S, prm, cos_t, sin_t, layer, depth, kv_all):
    f32 = x2.dtype
    proj3 = norm_matmul(x2, prm['norm_mix'], prm['w_in'], 1024, 512, single_buffer_x=True).reshape(N, S, NP)

    mixed, s_new = rwkv_time_mix(proj3, jnp.zeros((N, 1, RWKV_PAD), f32), jnp.zeros((N, HEADS // 2, 128, 128), f32),
                                 prm, S, 256, out_width=D_MODEL)
    shift_new = _unpermute_rwkv_cols(proj3[:, S - 1:S, C_RWKV:C_RWKV + RWKV_PAD])

    mixed, k_all, v_all = prompt_attention(proj3, cos_t, sin_t, mixed, layer, depth, kv_all)

    u = glu(proj3, 256)
    mixed = dwconv_ln_silu(u, jnp.zeros((N, CONV_HALO, CONV_DIM), f32), prm['conv_w'], prm['conv_b'],
                           prm['conv_ln_w'], prm['conv_ln_b'], 128, mixed=mixed)
    conv_new = u[:, S - (CONV_WIDTH - 1):]

    x2 = matmul_res(mixed.reshape(N * S, D_MODEL), prm['w_out'], x2, 512, 512, D_MODEL)

    x2, tails = ffn_fused(x2, prm['norm_ffn'], prm['w_up'], prm['w_down'], prm['ffn_conv_w'], prm['ffn_conv_b'],
                          jnp.zeros((N, FFN_HALO, 2 * D_FF), f32), S, 512, 256)
    keep = FFN_CONV_WIDTH - 1
    ffn_new = jnp.transpose(tails[:, :, :, FFN_HALO - keep:, :], (0, 3, 2, 1, 4)).reshape(N, keep, 2 * D_FF)
    return x2, (shift_new, _blockdiag_to_state(s_new), conv_new, ffn_new), (k_all, v_all)


def _sample_layer(x2, N, T, t_valid, prm, cos_t, sin_t, carry, attn_bufs, layer, depth, kv_all):
    shift0, wkv0, conv_buf, ffn_buf = carry
    proj3 = norm_matmul(x2, prm['norm_mix'], prm['w_in'], 512, 512).reshape(N, T, NP)

    proj_r = jnp.pad(proj3, ((0, 0), (0, CHUNK - T), (0, 0)))
    o_rwkv, s_new = rwkv_time_mix(proj_r, _permute_rwkv_cols(shift0), _state_to_blockdiag(wkv0), prm, t_valid, CHUNK)
    o_rwkv = o_rwkv[:, :T]
    shift_new = _unpermute_rwkv_cols(proj3[:, t_valid - 1:t_valid, C_RWKV:C_RWKV + RWKV_PAD])

    q_rot, k_rot = rope(proj3, cos_t, sin_t, 256)
    v_new = proj3[:, :, C_ATT + 2 * MIX:C_ATT + 3 * MIX]
    padr = ((0, 0), (0, 128 - T), (0, 0))
    o_att, k_all, v_all = sample_attention(q_rot, jnp.pad(k_rot, padr), jnp.pad(v_new, padr),
                                           attn_bufs[0], attn_bufs[1], t_valid, layer, depth, kv_all)

    u = glu(proj3, 256)
    o_conv = dwconv_ln_silu(u, _front_pad(conv_buf, CONV_HALO), prm['conv_w'], prm['conv_b'],
                            prm['conv_ln_w'], prm['conv_ln_b'], 128)
    conv_new = _last_rows(conv_buf, u, t_valid)

    mixed = jnp.concatenate([o_rwkv, o_att, o_conv], axis=-1).reshape(N * T, D_MODEL)
    x2 = matmul_res(mixed, prm['w_out'], x2, 512, 512, D_MODEL)

    uf3 = norm_matmul(x2, prm['norm_ffn'], prm['w_up'], 512, 512).reshape(N, T, 2 * D_FF)
    act = ffn_act(uf3, _front_pad(ffn_buf, FFN_HALO), prm['ffn_conv_w'], prm['ffn_conv_b'], 512, D_FF // 2)
    ffn_new = _last_rows(ffn_buf, uf3, t_valid)
    x2 = matmul_res(act.reshape(N * T, D_FF), prm['w_down'], x2, 512, 512, D_FF // 2)
    return x2, (shift_new, _blockdiag_to_state(s_new), conv_new, ffn_new), (k_all, v_all)


def kernel(x_prompt, x_sample, state_rwkv_shift, state_rwkv_wkv, state_attn_k, state_attn_v, state_conv, state_ffn_conv, pos_sample, norm_mix, w_in, rwkv_mu, rwkv_w0, rwkv_w2, rwkv_a0, rwkv_a2, rwkv_g2, rwkv_k_k, rwkv_k_a, rwkv_r_k, rwkv_ln_w, rwkv_ln_b, conv_w, conv_b, conv_ln_w, conv_ln_b, w_out, norm_ffn, w_up, ffn_conv_w, ffn_conv_b, w_down, norm_final):
    B, S, _ = x_prompt.shape
    NB, TS, _ = x_sample.shape
    depth = w_in.shape[0]
    win_buf = state_attn_k.shape[2]
    assert win_buf == S, "prompt key/value state is the whole rotated sequence"
    TSP = 8

    xp = x_prompt.reshape(B * S, D_MODEL)
    xs = jnp.pad(x_sample, ((0, 0), (0, TSP - TS), (0, 0))).reshape(NB * TSP, D_MODEL)
    cos_p, sin_p = _rope_tables(jnp.arange(S, dtype=jnp.int32)[None])
    cos_s, sin_s = _rope_tables(jnp.pad(pos_sample, ((0, 0), (0, TSP - TS))))
    new_p, new_s = [], []
    kv_p, kv_s = None, None
    for l in range(depth):
        prm = _layer_params(l, norm_mix, w_in, rwkv_mu, rwkv_w0, rwkv_w2, rwkv_a0, rwkv_a2, rwkv_g2, rwkv_k_k,
                            rwkv_k_a, rwkv_r_k, rwkv_ln_w, rwkv_ln_b, conv_w, conv_b, conv_ln_w, conv_ln_b, w_out,
                            norm_ffn, w_up, ffn_conv_w, ffn_conv_b, w_down)
        xp, st_p, kv_p = _prompt_layer(xp, B, S, prm, cos_p, sin_p, l, depth, kv_p)
        carry_s = (state_rwkv_shift[l], state_rwkv_wkv[l], state_conv[l], state_ffn_conv[l])
        bufs = (state_attn_k[l].reshape(NB, win_buf, MIX), state_attn_v[l].reshape(NB, win_buf, MIX))
        xs, st_s, kv_s = _sample_layer(xs, NB, TSP, TS, prm, cos_s, sin_s, carry_s, bufs, l, depth, kv_s)
        new_p.append(st_p)
        new_s.append(st_s)

    g = norm_final.reshape(1, D_MODEL)
    y_prompt = rmsnorm(xp, g, 512).reshape(B, S, D_MODEL)
    y_sample = rmsnorm(xs, g, 512).reshape(NB, TSP, D_MODEL)[:, :TS]

    def stack(states, i):
        return jnp.stack([st[i] for st in states], axis=0)

    heads_p = (depth, B, win_buf, HEADS, HEAD_DIM)
    heads_s = (depth, NB, win_buf, HEADS, HEAD_DIM)
    return (y_prompt, y_sample,
            stack(new_p, 0), stack(new_s, 0), stack(new_p, 1), stack(new_s, 1),
            kv_p[0].reshape(heads_p), kv_s[0].reshape(heads_s), kv_p[1].reshape(heads_p), kv_s[1].reshape(heads_s),
            stack(new_p, 2), stack(new_s, 2), stack(new_p, 3), stack(new_s, 3))
```

```python
import functools

import jax
import jax.numpy as jnp
from jax import lax
from jax.experimental import pallas as pl
from jax.experimental.pallas import tpu as pltpu

F32 = jnp.float32
BF16 = jnp.bfloat16

D_MODEL = 4096
HEAD_DIM = 64
HEADS = 24
MIX = HEADS * HEAD_DIM
CONV_DIM = 1024
DECAY_LORA = 128
AAA_LORA = 128
GATE_LORA = 480
GATE_PAD = 512
RWKV_PROJ = 3 * MIX + DECAY_LORA + AAA_LORA + GATE_LORA
RWKV_PAD = 3 * MIX + DECAY_LORA + AAA_LORA + GATE_PAD
ATT_PROJ = 3 * MIX
CONV_PROJ = 2 * CONV_DIM
CONV_WIDTH = 31
CONV_HALO = 32
D_FF = 11008
FFN_CONV_WIDTH = 3
FFN_HALO = 8
BAND = 128
DILATIONS = (1, 4, 16)
WINDOWS = (128, 512, 2048)
ROPE_THETA = 10000.0
RMS_EPS = 1e-6
LN_EPS = 1e-5
GN_EPS = 64e-5
NEG_INF = -1e30
CHUNK = 64

C_RWKV = 0
C_ATT = RWKV_PAD
C_CONV = C_ATT + ATT_PROJ
NP = 12288
LORA_IN = DECAY_LORA + AAA_LORA + GATE_PAD
M_RWKV, M_ATT, M_CONV = 0, MIX, 2 * MIX

VMEM_LIMIT = 56 * 1024 * 1024

NT = (((1,), (1,)), ((), ()))
TN = (((0,), (0,)), ((), ()))


def _cparams(sem):
    return pltpu.CompilerParams(dimension_semantics=sem, vmem_limit_bytes=VMEM_LIMIT)


def _sigmoid(x):
    return 1.0 / (1.0 + jnp.exp(-x))


def _head0_mask(shape):
    return lax.broadcasted_iota(jnp.int32, shape, 1) % (2 * HEAD_DIM) < HEAD_DIM


def _norm_matmul_kernel(x_ref, g_ref, w_ref, o_ref, h_ref):
    @pl.when(pl.program_id(1) == 0)
    def _():
        x = x_ref[...]
        ms = jnp.mean(x * x, axis=-1, keepdims=True)
        h_ref[...] = (x * lax.rsqrt(ms + RMS_EPS) * g_ref[...]).astype(BF16)

    o_ref[...] = jnp.dot(h_ref[...], w_ref[...], preferred_element_type=F32)


def norm_matmul(x, g, w, tm, tn, single_buffer_x=False):
    M, K = x.shape
    N = w.shape[1]
    tm = min(tm, M)
    assert M % tm == 0 and N % tn == 0
    x_mode = dict(pipeline_mode=pl.Buffered(1)) if single_buffer_x else {}
    return pl.pallas_call(
        _norm_matmul_kernel,
        grid=(M // tm, N // tn),
        in_specs=[pl.BlockSpec((tm, K), lambda i, j: (i, 0), **x_mode),
                  pl.BlockSpec((1, K), lambda i, j: (0, 0)),
                  pl.BlockSpec((K, tn), lambda i, j: (0, j))],
        out_specs=pl.BlockSpec((tm, tn), lambda i, j: (i, j)),
        out_shape=jax.ShapeDtypeStruct((M, N), F32),
        scratch_shapes=[pltpu.VMEM((tm, K), BF16)],
        compiler_params=_cparams(("parallel", "arbitrary")),
        name="norm_matmul",
    )(x, g, w)


def _matmul_res_kernel(a_ref, w_ref, r_ref, o_ref, acc_ref, *, nk):
    k = pl.program_id(2)

    @pl.when(k == 0)
    def _():
        acc_ref[...] = r_ref[...]

    acc_ref[...] += jnp.dot(a_ref[...], w_ref[...], preferred_element_type=F32)

    @pl.when(k == nk - 1)
    def _():
        o_ref[...] = acc_ref[...]


def matmul_res(a, w, res, tm, tn, tk):
    M, K = a.shape
    N = w.shape[1]
    tm = min(tm, M)
    assert M % tm == 0 and N % tn == 0 and K % tk == 0
    nk = K // tk
    return pl.pallas_call(
        functools.partial(_matmul_res_kernel, nk=nk),
        grid=(M // tm, N // tn, nk),
        in_specs=[pl.BlockSpec((tm, tk), lambda i, j, k: (i, k)),
                  pl.BlockSpec((tk, tn), lambda i, j, k: (k, j)),
                  pl.BlockSpec((tm, tn), lambda i, j, k: (i, j))],
        out_specs=pl.BlockSpec((tm, tn), lambda i, j, k: (i, j)),
        out_shape=jax.ShapeDtypeStruct((M, N), F32),
        scratch_shapes=[pltpu.VMEM((tm, tn), F32)],
        compiler_params=_cparams(("parallel", "parallel", "arbitrary")),
        name="matmul_res",
    )(a, w, res)


def _rmsnorm_kernel(x_ref, g_ref, o_ref):
    x = x_ref[...]
    ms = jnp.mean(x * x, axis=-1, keepdims=True)
    o_ref[...] = x * lax.rsqrt(ms + RMS_EPS) * g_ref[...]


def rmsnorm(x, g, tm):
    M, K = x.shape
    tm = min(tm, M)
    return pl.pallas_call(
        _rmsnorm_kernel,
        grid=(M // tm,),
        in_specs=[pl.BlockSpec((tm, K), lambda i: (i, 0)), pl.BlockSpec((1, K), lambda i: (0, 0))],
        out_specs=pl.BlockSpec((tm, K), lambda i: (i, 0)),
        out_shape=jax.ShapeDtypeStruct((M, K), F32),
        compiler_params=_cparams(("parallel",)),
        name="rmsnorm",
    )(x, g)


def _rot_half(x, first_half):
    w = x.shape[1]
    return jnp.where(first_half, pltpu.roll(x, w - HEAD_DIM // 2, 1), pltpu.roll(x, HEAD_DIM // 2, 1))


ROPE_BLOCK = 256


def _rope_kernel(q_ref, k_ref, cos_ref, sin_ref, qo_ref, ko_ref):
    reps = ROPE_BLOCK // (2 * HEAD_DIM)
    cos = jnp.concatenate([cos_ref[0]] * reps, axis=1)
    sin = jnp.concatenate([sin_ref[0]] * reps, axis=1)
    lane = lax.broadcasted_iota(jnp.int32, cos.shape, 1)
    first_half = (lane % HEAD_DIM) < (HEAD_DIM // 2)
    q = q_ref[0]
    k = k_ref[0]
    qo_ref[0] = q * cos + _rot_half(q, first_half) * sin
    ko_ref[0] = k * cos + _rot_half(k, first_half) * sin


def rope(proj3, cos_t, sin_t, tt):
    N, T, _ = proj3.shape
    tt = min(tt, T)
    tab = lambda n, i, j: (n, i, 0)
    rw = ROPE_BLOCK
    assert C_ATT % rw == 0 and MIX % rw == 0
    qb = C_ATT // rw
    kb = (C_ATT + MIX) // rw
    out = jax.ShapeDtypeStruct((N, T, MIX), F32)
    return pl.pallas_call(
        _rope_kernel,
        grid=(N, T // tt, MIX // rw),
        in_specs=[pl.BlockSpec((1, tt, rw), lambda n, i, j: (n, i, qb + j)),
                  pl.BlockSpec((1, tt, rw), lambda n, i, j: (n, i, kb + j)),
                  pl.BlockSpec((1, tt, 128), tab),
                  pl.BlockSpec((1, tt, 128), tab)],
        out_specs=[pl.BlockSpec((1, tt, rw), lambda n, i, j: (n, i, j)),
                   pl.BlockSpec((1, tt, rw), lambda n, i, j: (n, i, j))],
        out_shape=[out, out],
        compiler_params=_cparams(("parallel", "parallel", "parallel")),
        name="rope",
    )(proj3, proj3, cos_t, sin_t)


def _prompt_attn_kernel(*refs):
    q_ref, k_ref, v_ref, cos_ref, sin_ref = refs[:5]
    o_ref, ko_all_ref, vo_all_ref, qs_ref, acc_ref, m_ref, l_ref = refs[-7:]
    ko_ref = ko_all_ref.at[0]
    vo_ref = vo_all_ref.at[0]
    S = q_ref.shape[1]
    B = BAND
    P = 2 * HEAD_DIM
    lane = lax.broadcasted_iota(jnp.int32, (B, P), 1)
    first_half = (lane % HEAD_DIM) < (HEAD_DIM // 2)
    h0 = _head0_mask((B, P))
    scale = HEAD_DIM ** -0.5

    def rope_rows(i, carry):
        rows = pl.ds(pl.multiple_of(i * B, B), B)
        cos = cos_ref[0, rows, :]
        sin = sin_ref[0, rows, :]
        q = q_ref[0, rows, :]
        k = k_ref[0, rows, :]
        qs_ref[rows, :] = q * cos + _rot_half(q, first_half) * sin
        ko_ref[0, rows, :] = k * cos + _rot_half(k, first_half) * sin
        vo_ref[0, rows, :] = v_ref[0, rows, :]
        return carry

    lax.fori_loop(0, S // B, rope_rows, 0)

    def attend(q, kk, vv, valid):
        ms, ls, os_ = [], [], []
        for h in range(2):
            qh = jnp.where(h0 if h == 0 else ~h0, q, 0.0).astype(BF16)
            s = lax.dot_general(qh, kk, NT, preferred_element_type=F32) * scale
            s = jnp.where(valid, s, NEG_INF)
            m = jnp.max(s, axis=-1, keepdims=True)
            e = jnp.exp(s - m)
            ms.append(m)
            ls.append(jnp.sum(e, axis=-1, keepdims=True))
            os_.append(jnp.dot(e.astype(BF16), vv, preferred_element_type=F32))
        return jnp.where(h0, ms[0], ms[1]), jnp.where(h0, ls[0], ls[1]), jnp.where(h0, os_[0], os_[1])

    def merge(rows, m_c, l_c, o_c):
        m_p = m_ref[rows, :]
        m_n = jnp.maximum(m_p, m_c)
        a_p = jnp.exp(m_p - m_n)
        a_c = jnp.exp(m_c - m_n)
        acc_ref[rows, :] = acc_ref[rows, :] * a_p + o_c * a_c
        l_ref[rows, :] = l_ref[rows, :] * a_p + l_c * a_c
        m_ref[rows, :] = m_n

    qi2 = lax.broadcasted_iota(jnp.int32, (B, 2 * B), 0)
    ki2 = lax.broadcasted_iota(jnp.int32, (B, 2 * B), 1)
    dist2 = qi2 + B - ki2
    band2 = (dist2 >= 0) & (dist2 <= B)
    qi1 = lax.broadcasted_iota(jnp.int32, (B, B), 0)
    ki1 = lax.broadcasted_iota(jnp.int32, (B, B), 1)
    causal1 = qi1 >= ki1

    def two_block_keys(cur, prev):
        kk = jnp.concatenate([ko_ref[0, prev, :], ko_ref[0, cur, :]], axis=0).astype(BF16)
        vv = jnp.concatenate([v_ref[0, prev, :], v_ref[0, cur, :]], axis=0).astype(BF16)
        return kk, vv

    d16 = DILATIONS[2]
    assert S == B * d16
    per_iter = 4

    def dil16(it, carry):
        for u in range(per_iter):
            cur = pl.ds(it * per_iter + u, B, stride=d16)
            m_c, l_c, o_c = attend(qs_ref[cur, :], ko_ref[0, cur, :].astype(BF16), v_ref[0, cur, :].astype(BF16), causal1)
            acc_ref[cur, :] = o_c
            m_ref[cur, :] = m_c
            l_ref[cur, :] = l_c
        return carry

    lax.fori_loop(0, d16 // per_iter, dil16, 0)

    d4 = DILATIONS[1]

    def dil4(ib, carry):
        base = pl.multiple_of(ib * (B * d4), B * d4)
        pbase = pl.multiple_of(jnp.maximum(ib - 1, 0) * (B * d4), B * d4)
        for r in range(d4):
            cur = pl.ds(base + r, B, stride=d4)
            prev = pl.ds(pbase + r, B, stride=d4)
            kk, vv = two_block_keys(cur, prev)
            m_c, l_c, o_c = attend(qs_ref[cur, :], kk, vv, band2 & ((ki2 >= B) | (ib > 0)))
            merge(cur, m_c, l_c, o_c)
        return carry

    lax.fori_loop(0, S // (B * d4), dil4, 0)

    unroll = 4

    def dil1(it, carry):
        for u in range(unroll):
            ib = it * unroll + u
            cur = pl.ds(pl.multiple_of(ib * B, B), B)
            prev = pl.ds(pl.multiple_of(jnp.maximum(ib - 1, 0) * B, B), B)
            kk, vv = two_block_keys(cur, prev)
            m_c, l_c, o_c = attend(qs_ref[cur, :], kk, vv, band2 & ((ki2 >= B) | (ib > 0)))
            m_p = m_ref[cur, :]
            m_n = jnp.maximum(m_p, m_c)
            a_p = jnp.exp(m_p - m_n)
            a_c = jnp.exp(m_c - m_n)
            acc = acc_ref[cur, :] * a_p + o_c * a_c
            l_n = l_ref[cur, :] * a_p + l_c * a_c
            o_ref[0, cur, :] = (acc / l_n).astype(BF16)
        return carry

    lax.fori_loop(0, S // (B * unroll), dil1, 0)


def prompt_attention(proj3, cos_t, sin_t, mixed, layer, depth, kv_all=None):
    N, S, _ = proj3.shape
    qb, kb, vb = C_ATT // 128, (C_ATT + MIX) // 128, (C_ATT + 2 * MIX) // 128
    col = lambda b: pl.BlockSpec((1, S, 128), lambda n, h: (n, 0, b + h))
    tab = pl.BlockSpec((1, S, 128), lambda n, h: (0, 0, 0))
    anyspec = pl.BlockSpec(memory_space=pl.ANY)
    kv = jax.ShapeDtypeStruct((depth, N, S, MIX), F32)
    kv_spec = pl.BlockSpec((1, 1, S, 128), lambda n, h: (layer, n, 0, h))
    args = [proj3, proj3, proj3, cos_t, sin_t, mixed]
    in_specs = [col(qb), col(kb), col(vb), tab, tab, anyspec]
    aliases = {5: 0}
    if kv_all is not None:
        args += list(kv_all)
        in_specs += [anyspec, anyspec]
        aliases.update({6: 1, 7: 2})
    return pl.pallas_call(
        _prompt_attn_kernel,
        grid=(N, HEADS // 2),
        in_specs=in_specs,
        out_specs=[col(M_ATT // 128), kv_spec, kv_spec],
        out_shape=[jax.ShapeDtypeStruct(mixed.shape, mixed.dtype), kv, kv],
        scratch_shapes=[pltpu.VMEM((S, 128), F32)] * 4,
        input_output_aliases=aliases,
        compiler_params=_cparams(("parallel", "parallel")),
        name="prompt_attention",
    )(*args)


def _sample_attn_kernel(*refs, t_valid):
    q_ref, kn_ref, vn_ref, kb_ref, vb_ref = refs[:5]
    o_ref, ks_all_ref, vs_all_ref = refs[-3:]
    ks_ref = ks_all_ref.at[0]
    vs_ref = vs_all_ref.at[0]
    TQ = q_ref.shape[1]
    M = kb_ref.shape[1]
    TN_ = kn_ref.shape[1]
    h0 = _head0_mask((TQ, 2 * HEAD_DIM))
    scale = HEAD_DIM ** -0.5

    def counts(delta, in_range):
        c = jnp.zeros(delta.shape, F32)
        for win, dil in zip(WINDOWS, DILATIONS):
            c = c + jnp.where((delta % dil == 0) & (delta <= win) & in_range, 1.0, 0.0)
        return c

    tq = lax.broadcasted_iota(jnp.int32, (TQ, M), 0)
    kb_i = lax.broadcasted_iota(jnp.int32, (TQ, M), 1)
    d_buf = M + tq - kb_i
    c_buf = counts(d_buf, d_buf >= 0)
    tq2 = lax.broadcasted_iota(jnp.int32, (TQ, TN_), 0)
    tn2 = lax.broadcasted_iota(jnp.int32, (TQ, TN_), 1)
    d_new = tq2 - tn2
    c_new = counts(d_new, (d_new >= 0) & (tn2 < t_valid))

    row8 = lax.broadcasted_iota(jnp.int32, (8, 2 * HEAD_DIM), 0)
    for p in range(4):
        sl = slice(128 * p, 128 * (p + 1))
        qp = q_ref[0, :, sl]
        kb = kb_ref[0, :, sl]
        vb = vb_ref[0, :, sl]
        kn = kn_ref[0, :, sl]
        vn = vn_ref[0, :, sl]
        kb16, vb16, kn16, vn16 = kb.astype(BF16), vb.astype(BF16), kn.astype(BF16), vn.astype(BF16)
        outs = []
        for h in range(2):
            qh = jnp.where(h0 if h == 0 else ~h0, qp, 0.0).astype(BF16)
            s_b = lax.dot_general(qh, kb16, NT, preferred_element_type=F32) * scale
            s_n = lax.dot_general(qh, kn16, NT, preferred_element_type=F32) * scale
            s_b = jnp.where(c_buf > 0, s_b, NEG_INF)
            s_n = jnp.where(c_new > 0, s_n, NEG_INF)
            m = jnp.maximum(jnp.max(s_b, axis=-1, keepdims=True), jnp.max(s_n, axis=-1, keepdims=True))
            w_b = c_buf * jnp.exp(s_b - m)
            w_n = c_new * jnp.exp(s_n - m)
            l = jnp.sum(w_b, axis=-1, keepdims=True) + jnp.sum(w_n, axis=-1, keepdims=True)
            o = (jnp.dot(w_b.astype(BF16), vb16, preferred_element_type=F32)
                 + jnp.dot(w_n.astype(BF16), vn16, preferred_element_type=F32))
            outs.append(o / l)
        o_ref[0, :, sl] = jnp.where(h0, outs[0], outs[1]).astype(BF16)

        for src, new, dst in ((kb, kn, ks_ref), (vb, vn, vs_ref)):
            rolled = pltpu.roll(src, M - t_valid, 0)
            new_r = pltpu.roll(new[0:8], 8 - t_valid, 0)
            dst[0, 0:M - 8, sl] = rolled[0:M - 8]
            dst[0, M - 8:M, sl] = jnp.where(row8 >= 8 - t_valid, new_r, rolled[M - 8:M])


def sample_attention(q_rot, k_new, v_new, k_buf, v_buf, t_valid, layer, depth, kv_all=None):
    N, TQ, _ = q_rot.shape
    M = k_buf.shape[1]
    TN_ = k_new.shape[1]
    assert t_valid <= 8 <= TQ
    blk = lambda r: pl.BlockSpec((1, r, 512), lambda n, j: (n, 0, j))
    kv = jax.ShapeDtypeStruct((depth, N, M, MIX), F32)
    kv_spec = pl.BlockSpec((1, 1, M, 512), lambda n, j: (layer, n, 0, j))
    args = [q_rot, k_new, v_new, k_buf, v_buf]
    in_specs = [blk(TQ), blk(TN_), blk(TN_), blk(M), blk(M)]
    aliases = {}
    if kv_all is not None:
        args += list(kv_all)
        in_specs += [pl.BlockSpec(memory_space=pl.ANY)] * 2
        aliases = {5: 1, 6: 2}
    return pl.pallas_call(
        functools.partial(_sample_attn_kernel, t_valid=t_valid),
        grid=(N, MIX // 512),
        in_specs=in_specs,
        out_specs=[blk(TQ), kv_spec, kv_spec],
        out_shape=[jax.ShapeDtypeStruct((N, TQ, MIX), BF16), kv, kv],
        input_output_aliases=aliases,
        compiler_params=_cparams(("parallel", "parallel")),
        name="sample_attention",
    )(*args)


def _glu_kernel(a_ref, g_ref, o_ref):
    o_ref[0] = a_ref[0] * _sigmoid(g_ref[0])


def glu(proj3, tt):
    N, T, _ = proj3.shape
    tt = min(tt, T)
    gw = 256
    assert C_CONV % gw == 0 and CONV_DIM % gw == 0
    ab = C_CONV // gw
    gb = (C_CONV + CONV_DIM) // gw
    return pl.pallas_call(
        _glu_kernel,
        grid=(N, T // tt, CONV_DIM // gw),
        in_specs=[pl.BlockSpec((1, tt, gw), lambda n, i, j: (n, i, ab + j)),
                  pl.BlockSpec((1, tt, gw), lambda n, i, j: (n, i, gb + j))],
        out_specs=pl.BlockSpec((1, tt, gw), lambda n, i, j: (n, i, j)),
        out_shape=jax.ShapeDtypeStruct((N, T, CONV_DIM), F32),
        compiler_params=_cparams(("parallel", "parallel", "parallel")),
        name="glu",
    )(proj3, proj3)


def _dwconv_kernel(u_ref, uh_ref, h0_ref, w_ref, b_ref, lw_ref, lb_ref, *rest, single_tile):
    o_ref, ext_ref, y_ref = rest[-3:]
    tt = u_ref.shape[1]
    i = pl.program_id(1)

    @pl.when(i == 0)
    def _():
        ext_ref[0:CONV_HALO, :] = h0_ref[0]

    if not single_tile:
        @pl.when(i > 0)
        def _():
            ext_ref[0:CONV_HALO, :] = uh_ref[0]

    ext_ref[CONV_HALO:, :] = u_ref[0]
    lead = CONV_HALO - (CONV_WIDTH - 1)
    rows = min(16, tt)
    for r0 in range(0, tt, rows):
        for c0 in range(0, CONV_DIM, 512):
            acc = jnp.broadcast_to(b_ref[:, c0:c0 + 512], (rows, 512))
            for j in range(CONV_WIDTH):
                acc = acc + ext_ref[r0 + lead + j:r0 + lead + j + rows, c0:c0 + 512] * w_ref[j:j + 1, c0:c0 + 512]
            y_ref[r0:r0 + rows, c0:c0 + 512] = acc
    y = y_ref[...]
    mu = jnp.mean(y, axis=-1, keepdims=True)
    var = jnp.mean(jnp.square(y - mu), axis=-1, keepdims=True)
    c = (y - mu) * lax.rsqrt(var + LN_EPS) * lw_ref[...] + lb_ref[...]
    o_ref[0] = (c * _sigmoid(c)).astype(BF16)


def dwconv_ln_silu(u, halo0, w, b, lw, lb, tt, mixed=None):
    N, T, C = u.shape
    tt = min(tt, T)
    hb = tt // CONV_HALO if tt >= CONV_HALO else 1
    in_specs = [pl.BlockSpec((1, tt, C), lambda n, i: (n, i, 0)),
                pl.BlockSpec((1, min(CONV_HALO, T), C), lambda n, i: (n, jnp.maximum(i * hb - 1, 0), 0)),
                pl.BlockSpec((1, CONV_HALO, C), lambda n, i: (n, 0, 0)),
                pl.BlockSpec((32, C), lambda n, i: (0, 0)),
                pl.BlockSpec((1, C), lambda n, i: (0, 0)),
                pl.BlockSpec((1, C), lambda n, i: (0, 0)),
                pl.BlockSpec((1, C), lambda n, i: (0, 0))]
    args = [u, u, halo0, w, b, lw, lb]
    if mixed is None:
        out_shape = jax.ShapeDtypeStruct((N, T, C), BF16)
        out_spec = pl.BlockSpec((1, tt, C), lambda n, i: (n, i, 0))
        aliases = {}
    else:
        in_specs.append(pl.BlockSpec(memory_space=pl.ANY))
        args.append(mixed)
        out_shape = jax.ShapeDtypeStruct(mixed.shape, mixed.dtype)
        out_spec = pl.BlockSpec((1, tt, C), lambda n, i: (n, i, M_CONV // C))
        aliases = {7: 0}
    return pl.pallas_call(
        functools.partial(_dwconv_kernel, single_tile=(T == tt)),
        grid=(N, T // tt),
        in_specs=in_specs,
        out_specs=out_spec,
        out_shape=out_shape,
        scratch_shapes=[pltpu.VMEM((CONV_HALO + tt, C), F32), pltpu.VMEM((tt, C), F32)],
        input_output_aliases=aliases,
        compiler_params=_cparams(("parallel", "arbitrary")),
        name="dwconv_ln_silu",
    )(*args)


def _ffn_conv_act(ext_ref, wg_ref, wv_ref, bg_ref, bv_ref, store, tt):
    lead = FFN_HALO - (FFN_CONV_WIDTH - 1)
    rows = min(32, tt)
    for r0 in range(0, tt, rows):
        ys = []
        for s, w_ref, b_ref in ((0, wg_ref, bg_ref), (1, wv_ref, bv_ref)):
            acc = jnp.broadcast_to(b_ref[...], (rows, b_ref.shape[1]))
            for j in range(FFN_CONV_WIDTH):
                acc = acc + ext_ref[s, r0 + lead + j:r0 + lead + j + rows, :] * w_ref[j:j + 1, :]
            ys.append(acc)
        gate, val = ys
        store(r0, rows, (gate * _sigmoid(gate) * val).astype(BF16))


def _ffn_act_kernel(ug_ref, uv_ref, hg_ref, hv_ref, h0g_ref, h0v_ref, wg_ref, wv_ref, bg_ref, bv_ref, o_ref, ext_ref):
    tt = ug_ref.shape[1]
    i = pl.program_id(1)

    @pl.when(i == 0)
    def _():
        ext_ref[0, 0:FFN_HALO, :] = h0g_ref[0]
        ext_ref[1, 0:FFN_HALO, :] = h0v_ref[0]

    @pl.when(i > 0)
    def _():
        ext_ref[0, 0:FFN_HALO, :] = hg_ref[0]
        ext_ref[1, 0:FFN_HALO, :] = hv_ref[0]

    ext_ref[0, FFN_HALO:, :] = ug_ref[0]
    ext_ref[1, FFN_HALO:, :] = uv_ref[0]

    def store(r0, rows, val):
        o_ref[0, r0:r0 + rows, :] = val

    _ffn_conv_act(ext_ref, wg_ref, wv_ref, bg_ref, bv_ref, store, tt)


def ffn_act(u3, halo0, w, b, tt, tf):
    N, T, F2 = u3.shape
    F = F2 // 2
    tt = min(tt, T)
    nf = F // tf
    hb = tt // FFN_HALO
    g = lambda n, i, j: (n, i, j)
    v = lambda n, i, j: (n, i, j + nf)
    hg = lambda n, i, j: (n, jnp.maximum(i * hb - 1, 0), j)
    hv = lambda n, i, j: (n, jnp.maximum(i * hb - 1, 0), j + nf)
    return pl.pallas_call(
        _ffn_act_kernel,
        grid=(N, T // tt, nf),
        in_specs=[pl.BlockSpec((1, tt, tf), g), pl.BlockSpec((1, tt, tf), v),
                  pl.BlockSpec((1, FFN_HALO, tf), hg), pl.BlockSpec((1, FFN_HALO, tf), hv),
                  pl.BlockSpec((1, FFN_HALO, tf), lambda n, i, j: (n, 0, j)),
                  pl.BlockSpec((1, FFN_HALO, tf), lambda n, i, j: (n, 0, j + nf)),
                  pl.BlockSpec((8, tf), lambda n, i, j: (0, j)), pl.BlockSpec((8, tf), lambda n, i, j: (0, j + nf)),
                  pl.BlockSpec((1, tf), lambda n, i, j: (0, j)), pl.BlockSpec((1, tf), lambda n, i, j: (0, j + nf))],
        out_specs=pl.BlockSpec((1, tt, tf), g),
        out_shape=jax.ShapeDtypeStruct((N, T, F), BF16),
        scratch_shapes=[pltpu.VMEM((2, FFN_HALO + tt, tf), F32)],
        compiler_params=_cparams(("parallel", "parallel", "parallel")),
        name="ffn_act",
    )(u3, u3, u3, u3, halo0, halo0, w, w, b, b)


def _ffn_fused_kernel(x_ref, g_ref, wg_ref, wv_ref, wd_ref, cwg_ref, cwv_ref, cbg_ref, cbv_ref, h0g_ref, h0v_ref,
                      o_ref, st_ref, h_ref, ext_ref, carry_ref, act_ref, *, tiles_per_seq):
    i = pl.program_id(0)
    j = pl.program_id(1)
    tm = x_ref.shape[0]

    @pl.when(j == 0)
    def _():
        x = x_ref[...]
        ms = jnp.mean(x * x, axis=-1, keepdims=True)
        h_ref[...] = (x * lax.rsqrt(ms + RMS_EPS) * g_ref[...]).astype(BF16)
        o_ref[...] = x

    h = h_ref[...]
    ug = jnp.dot(h, wg_ref[...], preferred_element_type=F32)
    uv = jnp.dot(h, wv_ref[...], preferred_element_type=F32)
    seq_start = (i % tiles_per_seq) == 0
    ext_ref[0, 0:FFN_HALO, :] = jnp.where(seq_start, h0g_ref[0], carry_ref[j, 0])
    ext_ref[1, 0:FFN_HALO, :] = jnp.where(seq_start, h0v_ref[0], carry_ref[j, 1])
    ext_ref[0, FFN_HALO:, :] = ug
    ext_ref[1, FFN_HALO:, :] = uv
    carry_ref[j, 0] = ug[tm - FFN_HALO:, :]
    carry_ref[j, 1] = uv[tm - FFN_HALO:, :]

    def store(r0, rows, val):
        act_ref[r0:r0 + rows, :] = val

    _ffn_conv_act(ext_ref, cwg_ref, cwv_ref, cbg_ref, cbv_ref, store, tm)
    o_ref[...] += jnp.dot(act_ref[...], wd_ref[...], preferred_element_type=F32)

    @pl.when((j == pl.num_programs(1) - 1) & (i % tiles_per_seq == tiles_per_seq - 1))
    def _():
        st_ref[0] = carry_ref[...]


def ffn_fused(x, g, w_up, w_down, cw, cb, halo0, seq_len, tm, tf):
    M, D = x.shape
    F = w_down.shape[0]
    assert M % tm == 0 and seq_len % tm == 0 and F % tf == 0
    nf = F // tf
    tps = seq_len // tm
    gate = lambda i, j: (0, j)
    val = lambda i, j: (0, j + nf)
    return pl.pallas_call(
        functools.partial(_ffn_fused_kernel, tiles_per_seq=tps),
        grid=(M // tm, nf),
        in_specs=[pl.BlockSpec((tm, D), lambda i, j: (i, 0), pipeline_mode=pl.Buffered(1)),
                  pl.BlockSpec((1, D), lambda i, j: (0, 0)),
                  pl.BlockSpec((D, tf), gate),
                  pl.BlockSpec((D, tf), val),
                  pl.BlockSpec((tf, D), lambda i, j: (j, 0)),
                  pl.BlockSpec((8, tf), gate), pl.BlockSpec((8, tf), val),
                  pl.BlockSpec((1, tf), gate), pl.BlockSpec((1, tf), val),
                  pl.BlockSpec((1, FFN_HALO, tf), lambda i, j: (i // tps, 0, j)),
                  pl.BlockSpec((1, FFN_HALO, tf), lambda i, j: (i // tps, 0, j + nf))],
        out_specs=[pl.BlockSpec((tm, D), lambda i, j: (i, 0)),
                   pl.BlockSpec((1, nf, 2, FFN_HALO, tf), lambda i, j: (i // tps, 0, 0, 0, 0))],
        out_shape=[jax.ShapeDtypeStruct((M, D), F32),
                   jax.ShapeDtypeStruct((M // seq_len, nf, 2, FFN_HALO, tf), F32)],
        scratch_shapes=[pltpu.VMEM((tm, D), BF16), pltpu.VMEM((2, FFN_HALO + tm, tf), F32),
                        pltpu.VMEM((nf, 2, FFN_HALO, tf), F32), pltpu.VMEM((tm, tf), BF16)],
        compiler_params=_cparams(("arbitrary", "arbitrary")),
        name="ffn_fused",
    )(x, g, w_up, w_up, w_down, cw, cw, cb, cb, halo0, halo0)


def _rwkv_kernel(r_ref, k_ref, v_ref, lo_ref,
                 sr_ref, sk_ref, sv_ref, slo_ref,
                 mur_ref, muk_ref, muv_ref, mulo_ref,
                 w0_ref, a0_ref, kk_ref, ka_ref, rk_ref, lnw_ref, lnb_ref,
                 w2_ref, a2_ref, g2_ref, s0_ref,
                 o_ref, so_ref,
                 S_ref, pr_ref, pk_ref, pv_ref, plo_ref, *, t_valid, t_total):
    c = pl.program_id(2)
    Tc = r_ref.shape[1]
    PW = 2 * HEAD_DIM
    n_pairs = r_ref.shape[2] // PW

    @pl.when(c == 0)
    def _():
        S_ref[...] = s0_ref[0]
        pr_ref[...] = sr_ref[0]
        pk_ref[...] = sk_ref[0]
        pv_ref[...] = sv_ref[0]
        plo_ref[...] = slo_ref[0]

    def lerp(p_ref, prev_ref, mu_ref):
        p = p_ref[0]
        row = lax.broadcasted_iota(jnp.int32, p.shape, 0)
        shifted = jnp.where(row == 0, prev_ref[...], pltpu.roll(p, 1, 0))
        prev_ref[...] = p[Tc - 1:Tc, :]
        return p + (shifted - p) * mu_ref[...]

    xr = lerp(r_ref, pr_ref, mur_ref)
    xk = lerp(k_ref, pk_ref, muk_ref)
    xv = lerp(v_ref, pv_ref, muv_ref)
    xlo = lerp(lo_ref, plo_ref, mulo_ref)
    xw = xlo[:, :DECAY_LORA]
    xa = xlo[:, DECAY_LORA:DECAY_LORA + AAA_LORA]
    xg = xlo[:, DECAY_LORA + AAA_LORA:]

    h0p = _head0_mask((Tc, PW))

    def head_sum(x):
        parts = []
        for p in range(n_pairs):
            xp = x[:, PW * p:PW * (p + 1)]
            s_0 = jnp.sum(jnp.where(h0p, xp, 0.0), axis=-1, keepdims=True)
            s_1 = jnp.sum(jnp.where(h0p, 0.0, xp), axis=-1, keepdims=True)
            parts.append(jnp.where(h0p, s_0, s_1))
        return parts[0] if n_pairs == 1 else jnp.concatenate(parts, axis=1)

    def mm(x, y_):
        return jnp.dot(x, y_, preferred_element_type=F32)

    z = w0_ref[...] + mm(jnp.tanh(xw).astype(BF16), w2_ref[...])
    w_log = jnp.minimum(z, 0.0) - jnp.log(1.0 + jnp.exp(-jnp.abs(z))) - 0.5
    ld = -jnp.exp(w_log)
    a = _sigmoid(a0_ref[...] + mm(xa.astype(BF16), a2_ref[...]))
    g = mm(_sigmoid(xg).astype(BF16), g2_ref[...])

    kk = xk * kk_ref[...]
    kk = kk * lax.rsqrt(jnp.maximum(head_sum(kk * kk), 1e-24))
    kmod = xk * (1.0 + (a - 1.0) * ka_ref[...])
    avec = -kk
    bvec = kk * a
    bonus = head_sum(xr * kmod * rk_ref[...]) * xv
    vval = xv

    if t_valid < t_total:
        row = lax.broadcasted_iota(jnp.int32, (Tc, n_pairs * PW), 0) + c * Tc
        ok = row < t_valid
        ld = jnp.where(ok, ld, 0.0)
        avec = jnp.where(ok, avec, 0.0)
        bvec = jnp.where(ok, bvec, 0.0)
        kmod_s = jnp.where(ok, kmod, 0.0)
        vval = jnp.where(ok, vval, 0.0)
    else:
        kmod_s = kmod

    C = CHUNK
    tri_r = lax.broadcasted_iota(jnp.int32, (C, C), 0)
    tri_c = lax.broadcasted_iota(jnp.int32, (C, C), 1)
    tril = jnp.where(tri_r >= tri_c, 1.0, 0.0).astype(F32)
    it = lax.broadcasted_iota(jnp.int32, (2 * C, 2 * C), 0) % C
    js = lax.broadcasted_iota(jnp.int32, (2 * C, 2 * C), 1) % C
    strict = it > js
    incl = it >= js
    h0c = _head0_mask((C, 2 * HEAD_DIM))

    def stack(x):
        return jnp.concatenate([jnp.where(h0c, x, 0.0), jnp.where(h0c, 0.0, x)], axis=0)

    n_chunks = Tc // C
    grp = 2 if n_chunks % 2 == 0 else 1
    gw = 2 * C * grp
    eye_g = jnp.where(lax.broadcasted_iota(jnp.int32, (gw, gw), 0)
                      == lax.broadcasted_iota(jnp.int32, (gw, gw), 1), 1.0, 0.0).astype(F32)
    zero_blk = jnp.zeros((2 * C, 2 * C), F32)

    def bdiag(blocks):
        if len(blocks) == 1:
            return blocks[0]
        return jnp.concatenate([jnp.concatenate([blocks[0], zero_blk], axis=1),
                                jnp.concatenate([zero_blk, blocks[1]], axis=1)], axis=0)

    def cat(blocks):
        return blocks[0] if len(blocks) == 1 else jnp.concatenate(blocks, axis=0)

    a16, r32, bb16, kb16, v16, gts, Ls, AKs, RBs, RKs = ({} for _ in range(10))
    units = [(p, j) for j in range(n_chunks) for p in range(n_pairs)]
    for p, j in units:
        sl = (slice(C * j, C * (j + 1)), slice(PW * p, PW * (p + 1)))
        ldc = ld[sl]
        cs = jnp.dot(tril, ldc, precision=lax.Precision.HIGHEST, preferred_element_type=F32)
        tot = cs[C - 1:C, :]
        eg = jnp.exp(cs)
        egi = jnp.exp(-cs)
        ege = jnp.exp(cs - ldc)
        et = jnp.exp(tot - cs)
        gts[p, j] = jnp.exp(tot)
        a_s = stack(avec[sl] * ege).astype(BF16)
        r_s = stack(xr[sl] * eg)
        b_s = stack(bvec[sl] * egi).astype(BF16)
        k_s = stack(kmod_s[sl] * egi).astype(BF16)
        sc = lax.dot_general(jnp.concatenate([a_s, r_s.astype(BF16)], axis=0), jnp.concatenate([b_s, k_s], axis=0), NT,
                             preferred_element_type=F32)
        a16[p, j] = a_s
        r32[p, j] = r_s
        bb16[p, j] = stack(bvec[sl] * et).astype(BF16)
        kb16[p, j] = stack(kmod_s[sl] * et).astype(BF16)
        v16[p, j] = stack(vval[sl]).astype(BF16)
        Ls[p, j] = jnp.where(strict, sc[:2 * C, :2 * C], 0.0)
        AKs[p, j] = jnp.where(strict, sc[:2 * C, 2 * C:], 0.0)
        RBs[p, j] = jnp.where(incl, sc[2 * C:, :2 * C], 0.0)
        RKs[p, j] = jnp.where(incl, sc[2 * C:, 2 * C:], 0.0)

    groups = [[(p, j) for j in range(g0, g0 + grp)] for g0 in range(0, n_chunks, grp) for p in range(n_pairs)]
    Ps = [bdiag([Ls[c_] for c_ in g_]) for g_ in groups]
    Tms = [eye_g + p_ for p_ in Ps]
    for _ in range(C.bit_length() - 2):
        for gi in range(len(groups)):
            P16 = Ps[gi].astype(BF16)
            Ps[gi] = mm(P16, P16)
            Tms[gi] = Tms[gi] + mm(Tms[gi].astype(BF16), Ps[gi].astype(BF16))

    RAs, YNs, Gs, Ns = {}, {}, {}, {}
    for gi, g_ in enumerate(groups):
        T16 = Tms[gi].astype(BF16)
        A2 = cat([a16[c_] for c_ in g_])
        V2 = cat([v16[c_] for c_ in g_])
        R2 = cat([r32[c_] for c_ in g_])
        RB16 = bdiag([RBs[c_] for c_ in g_]).astype(BF16)
        RK16 = bdiag([RKs[c_] for c_ in g_]).astype(BF16)
        TA16 = mm(T16, A2).astype(BF16)
        TV16 = mm(T16, mm(bdiag([AKs[c_] for c_ in g_]).astype(BF16), V2).astype(BF16)).astype(BF16)
        RA = R2 + mm(RB16, TA16)
        YN = mm(jnp.concatenate([RB16, RK16], axis=1), jnp.concatenate([TV16, V2], axis=0))
        for q, c_ in enumerate(g_):
            rs = slice(2 * C * q, 2 * C * (q + 1))
            RAs[c_] = RA[rs].astype(BF16)
            YNs[c_] = YN[rs]
            Gs[c_] = lax.dot_general(TA16[rs], bb16[c_], TN, preferred_element_type=F32).astype(BF16)
            Ns[c_] = lax.dot_general(jnp.concatenate([TV16[rs], v16[c_]], axis=0),
                                     jnp.concatenate([bb16[c_], kb16[c_]], axis=0), TN, preferred_element_type=F32)

    ys = {}
    Ss = [S_ref[p] for p in range(n_pairs)]
    for p, j in units:
        S16 = Ss[p].astype(BF16)
        Y = lax.dot_general(RAs[p, j], S16, NT, preferred_element_type=F32) + YNs[p, j]
        ys[p, j] = Y[:C] + Y[C:]
        Ss[p] = Ss[p] * gts[p, j] + mm(S16, Gs[p, j]) + Ns[p, j]
    for p in range(n_pairs):
        S_ref[p] = Ss[p]
    y_pairs = [cat([ys[p, j] for j in range(n_chunks)]) for p in range(n_pairs)]
    y = y_pairs[0] if n_pairs == 1 else jnp.concatenate(y_pairs, axis=1)

    mu = head_sum(y) * (1.0 / HEAD_DIM)
    var = head_sum(jnp.square(y - mu)) * (1.0 / HEAD_DIM)
    yn = (y - mu) * lax.rsqrt(var + GN_EPS) * lnw_ref[...] + lnb_ref[...]
    o_ref[0] = ((yn + bonus) * g).astype(BF16)

    @pl.when(c == pl.num_programs(2) - 1)
    def _():
        for p in range(n_pairs):
            so_ref[0, p] = Ss[p]


def rwkv_time_mix(proj3, shift0, s0, prm, t_valid, tc, out_width=MIX, pairs=2):
    N, T, _ = proj3.shape
    tc = min(tc, T)
    W = 2 * HEAD_DIM * pairs
    assert T % tc == 0 and tc % CHUNK == 0 and MIX % W == 0 and C_RWKV % W == 0
    rb = C_RWKV // W
    nb = MIX // W
    pspec = lambda w, off: pl.BlockSpec((1, tc, w), lambda n, h, c: (n, c, off(h)))
    sspec = lambda w, off: pl.BlockSpec((1, 1, w), lambda n, h, c: (n, 0, off(h)))
    mspec = lambda w, off: pl.BlockSpec((1, w), lambda n, h, c: (0, off(h)))
    hspec = pl.BlockSpec((1, W), lambda n, h, c: (0, h))
    assert (C_RWKV + 3 * MIX) % LORA_IN == 0
    lo_p = (C_RWKV + 3 * MIX) // LORA_IN
    lo_s = 3 * MIX // LORA_IN
    in_specs = [
        pspec(W, lambda h: rb + h), pspec(W, lambda h: rb + nb + h), pspec(W, lambda h: rb + 2 * nb + h),
        pspec(LORA_IN, lambda h: lo_p),
        sspec(W, lambda h: h), sspec(W, lambda h: nb + h), sspec(W, lambda h: 2 * nb + h),
        sspec(LORA_IN, lambda h: lo_s),
        mspec(W, lambda h: h), mspec(W, lambda h: nb + h), mspec(W, lambda h: 2 * nb + h),
        mspec(LORA_IN, lambda h: lo_s),
        hspec, hspec, hspec, hspec, hspec, hspec, hspec,
        pl.BlockSpec((DECAY_LORA, W), lambda n, h, c: (0, h)),
        pl.BlockSpec((AAA_LORA, W), lambda n, h, c: (0, h)),
        pl.BlockSpec((GATE_PAD, W), lambda n, h, c: (0, h)),
        pl.BlockSpec((1, pairs, 128, 128), lambda n, h, c: (n, h, 0, 0)),
    ]
    args = ([proj3] * 4 + [shift0] * 4 + [prm['mu']] * 4
            + [prm['w0'], prm['a0'], prm['k_k'], prm['k_a'], prm['r_k'], prm['ln_w'], prm['ln_b'],
               prm['w2'], prm['a2'], prm['g2'], s0])
    return pl.pallas_call(
        functools.partial(_rwkv_kernel, t_valid=t_valid, t_total=T),
        grid=(N, nb, T // tc),
        in_specs=in_specs,
        out_specs=[pl.BlockSpec((1, tc, W), lambda n, h, c: (n, c, M_RWKV // W + h)),
                   pl.BlockSpec((1, pairs, 128, 128), lambda n, h, c: (n, h, 0, 0))],
        out_shape=[jax.ShapeDtypeStruct((N, T, out_width), BF16),
                   jax.ShapeDtypeStruct((N, HEADS // 2, 128, 128), F32)],
        scratch_shapes=[pltpu.VMEM((pairs, 128, 128), F32), pltpu.VMEM((1, W), F32), pltpu.VMEM((1, W), F32),
                        pltpu.VMEM((1, W), F32), pltpu.VMEM((1, LORA_IN), F32)],
        compiler_params=_cparams(("parallel", "parallel", "arbitrary")),
        name="rwkv7_time_mix",
    )(*args)


def _pad_rwkv_cols(t):
    return jnp.pad(t, [(0, 0)] * (t.ndim - 1) + [(0, RWKV_PAD - RWKV_PROJ)])


PREP_BLOCK = 512


def _prep_w_in_kernel(cur_ref, prev_ref, o_ref):
    j = pl.program_id(1)
    cur = cur_ref[...]
    lane = lax.broadcasted_iota(jnp.int32, cur.shape, 1)
    col = lane + j * PREP_BLOCK
    shift = RWKV_PAD - RWKV_PROJ
    shifted = jnp.where(lane < shift, pltpu.roll(prev_ref[...], shift, 1), pltpu.roll(cur, shift, 1))
    in_tail = (col >= RWKV_PAD) & (col < C_CONV + CONV_PROJ)
    o_ref[...] = jnp.where(col < RWKV_PROJ, cur, jnp.where(in_tail, shifted, 0.0)).astype(BF16)


def prep_w_in(w_all, layer):
    _, K, n_src = w_all.shape
    pb = PREP_BLOCK
    last = pl.cdiv(n_src, pb) - 1
    return pl.pallas_call(
        _prep_w_in_kernel,
        grid=(K // pb, NP // pb),
        in_specs=[pl.BlockSpec((None, pb, pb), lambda i, j: (layer, i, jnp.minimum(j, last))),
                  pl.BlockSpec((None, pb, pb), lambda i, j: (layer, i, jnp.clip(j - 1, 0, last)))],
        out_specs=pl.BlockSpec((pb, pb), lambda i, j: (i, j)),
        out_shape=jax.ShapeDtypeStruct((K, NP), BF16),
        compiler_params=_cparams(("parallel", "parallel")),
        name="prep_w_in",
    )(w_all, w_all)


def _layer_params(l, norm_mix, w_in, rwkv_mu, rwkv_w0, rwkv_w2, rwkv_a0, rwkv_a2, rwkv_g2, rwkv_k_k, rwkv_k_a,
                  rwkv_r_k, rwkv_ln_w, rwkv_ln_b, conv_w, conv_b, conv_ln_w, conv_ln_b, w_out, norm_ffn, w_up,
                  ffn_conv_w, ffn_conv_b, w_down):
    row = lambda t: t.reshape(1, -1)
    return {
        'norm_mix': row(norm_mix[l]),
        'w_in': prep_w_in(w_in, l),
        'mu': _pad_rwkv_cols(row(rwkv_mu[l])),
        'w0': row(rwkv_w0[l]), 'a0': row(rwkv_a0[l]), 'k_k': row(rwkv_k_k[l]), 'k_a': row(rwkv_k_a[l]),
        'r_k': row(rwkv_r_k[l]), 'ln_w': row(rwkv_ln_w[l]), 'ln_b': row(rwkv_ln_b[l]),
        'w2': rwkv_w2[l].astype(BF16), 'a2': rwkv_a2[l].astype(BF16),
        'g2': jnp.pad(rwkv_g2[l], ((0, GATE_PAD - GATE_LORA), (0, 0))).astype(BF16),
        'conv_w': jnp.pad(conv_w[l], ((0, 32 - CONV_WIDTH), (0, 0))), 'conv_b': row(conv_b[l]),
        'conv_ln_w': row(conv_ln_w[l]), 'conv_ln_b': row(conv_ln_b[l]),
        'w_out': w_out[l].astype(BF16),
        'norm_ffn': row(norm_ffn[l]),
        'w_up': w_up[l].astype(BF16),
        'ffn_conv_w': jnp.pad(ffn_conv_w[l], ((0, 8 - FFN_CONV_WIDTH), (0, 0))), 'ffn_conv_b': row(ffn_conv_b[l]),
        'w_down': w_down[l].astype(BF16),
    }


def _rope_tables(pos):
    half = HEAD_DIM // 2
    inv_freq = ROPE_THETA ** (-jnp.arange(half, dtype=F32) * 2.0 / HEAD_DIM)
    ang = pos.astype(F32)[..., None] * inv_freq
    cos, sin = jnp.cos(ang), jnp.sin(ang)
    return jnp.concatenate([cos, cos, cos, cos], axis=-1), jnp.concatenate([-sin, sin, -sin, sin], axis=-1)


def _state_to_blockdiag(s):
    N = s.shape[0]
    s = s.reshape(N, HEADS // 2, 2, HEAD_DIM, HEAD_DIM)
    z = jnp.zeros_like(s[:, :, 0])
    top = jnp.concatenate([s[:, :, 0], z], axis=-1)
    bot = jnp.concatenate([z, s[:, :, 1]], axis=-1)
    return jnp.concatenate([top, bot], axis=-2)


def _blockdiag_to_state(s):
    N = s.shape[0]
    return jnp.stack([s[:, :, :HEAD_DIM, :HEAD_DIM], s[:, :, HEAD_DIM:, HEAD_DIM:]], axis=2).reshape(
        N, HEADS, HEAD_DIM, HEAD_DIM)


def _last_rows(buf, u, t_valid):
    keep = buf.shape[1]
    if t_valid >= keep:
        return u[:, t_valid - keep:t_valid]
    return jnp.concatenate([buf[:, t_valid:], u[:, :t_valid]], axis=1)


def _front_pad(buf, rows):
    return jnp.pad(buf, ((0, 0), (rows - buf.shape[1], 0), (0, 0)))


def _prompt_layer(x2, N, S, prm, cos_t, sin_t, layer, depth, kv_all):
    f32 = x2.dtype
    proj3 = norm_matmul(x2, prm['norm_mix'], prm['w_in'], 1024, 512, single_buffer_x=True).reshape(N, S, NP)

    mixed, s_new = rwkv_time_mix(proj3, jnp.zeros((N, 1, RWKV_PAD), f32), jnp.zeros((N, HEADS // 2, 128, 128), f32),
                                 prm, S, 256, out_width=D_MODEL, pairs=4)
    shift_new = proj3[:, S - 1:S, C_RWKV:C_RWKV + RWKV_PROJ]

    mixed, k_all, v_all = prompt_attention(proj3, cos_t, sin_t, mixed, layer, depth, kv_all)

    u = glu(proj3, 256)
    mixed = dwconv_ln_silu(u, jnp.zeros((N, CONV_HALO, CONV_DIM), f32), prm['conv_w'], prm['conv_b'],
                           prm['conv_ln_w'], prm['conv_ln_b'], 128, mixed=mixed)
    conv_new = u[:, S - (CONV_WIDTH - 1):]

    x2 = matmul_res(mixed.reshape(N * S, D_MODEL), prm['w_out'], x2, 512, 512, D_MODEL)

    x2, tails = ffn_fused(x2, prm['norm_ffn'], prm['w_up'], prm['w_down'], prm['ffn_conv_w'], prm['ffn_conv_b'],
                          jnp.zeros((N, FFN_HALO, 2 * D_FF), f32), S, 512, 256)
    keep = FFN_CONV_WIDTH - 1
    ffn_new = jnp.transpose(tails[:, :, :, FFN_HALO - keep:, :], (0, 3, 2, 1, 4)).reshape(N, keep, 2 * D_FF)
    return x2, (shift_new, _blockdiag_to_state(s_new), conv_new, ffn_new), (k_all, v_all)


def _sample_layer(x2, N, T, t_valid, prm, cos_t, sin_t, carry, attn_bufs, layer, depth, kv_all):
    shift0, wkv0, conv_buf, ffn_buf = carry
    proj3 = norm_matmul(x2, prm['norm_mix'], prm['w_in'], 512, 512).reshape(N, T, NP)

    proj_r = jnp.pad(proj3, ((0, 0), (0, CHUNK - T), (0, 0)))
    o_rwkv, s_new = rwkv_time_mix(proj_r, _pad_rwkv_cols(shift0), _state_to_blockdiag(wkv0), prm, t_valid, CHUNK)
    o_rwkv = o_rwkv[:, :T]
    shift_new = proj3[:, t_valid - 1:t_valid, C_RWKV:C_RWKV + RWKV_PROJ]

    q_rot, k_rot = rope(proj3, cos_t, sin_t, 256)
    v_new = proj3[:, :, C_ATT + 2 * MIX:C_ATT + 3 * MIX]
    padr = ((0, 0), (0, 128 - T), (0, 0))
    o_att, k_all, v_all = sample_attention(q_rot, jnp.pad(k_rot, padr), jnp.pad(v_new, padr),
                                           attn_bufs[0], attn_bufs[1], t_valid, layer, depth, kv_all)

    u = glu(proj3, 256)
    o_conv = dwconv_ln_silu(u, _front_pad(conv_buf, CONV_HALO), prm['conv_w'], prm['conv_b'],
                            prm['conv_ln_w'], prm['conv_ln_b'], 128)
    conv_new = _last_rows(conv_buf, u, t_valid)

    mixed = jnp.concatenate([o_rwkv, o_att, o_conv], axis=-1).reshape(N * T, D_MODEL)
    x2 = matmul_res(mixed, prm['w_out'], x2, 512, 512, D_MODEL)

    uf3 = norm_matmul(x2, prm['norm_ffn'], prm['w_up'], 512, 512).reshape(N, T, 2 * D_FF)
    act = ffn_act(uf3, _front_pad(ffn_buf, FFN_HALO), prm['ffn_conv_w'], prm['ffn_conv_b'], 512, D_FF // 2)
    ffn_new = _last_rows(ffn_buf, uf3, t_valid)
    x2 = matmul_res(act.reshape(N * T, D_FF), prm['w_down'], x2, 512, 512, D_FF // 2)
    return x2, (shift_new, _blockdiag_to_state(s_new), conv_new, ffn_new), (k_all, v_all)


def kernel(x_prompt, x_sample, state_rwkv_shift, state_rwkv_wkv, state_attn_k, state_attn_v, state_conv, state_ffn_conv, pos_sample, norm_mix, w_in, rwkv_mu, rwkv_w0, rwkv_w2, rwkv_a0, rwkv_a2, rwkv_g2, rwkv_k_k, rwkv_k_a, rwkv_r_k, rwkv_ln_w, rwkv_ln_b, conv_w, conv_b, conv_ln_w, conv_ln_b, w_out, norm_ffn, w_up, ffn_conv_w, ffn_conv_b, w_down, norm_final):
    B, S, _ = x_prompt.shape
    NB, TS, _ = x_sample.shape
    depth = w_in.shape[0]
    win_buf = state_attn_k.shape[2]
    assert win_buf == S, "prompt key/value state is the whole rotated sequence"
    TSP = 8

    xp = x_prompt.reshape(B * S, D_MODEL)
    xs = jnp.pad(x_sample, ((0, 0), (0, TSP - TS), (0, 0))).reshape(NB * TSP, D_MODEL)
    cos_p, sin_p = _rope_tables(jnp.arange(S, dtype=jnp.int32)[None])
    cos_s, sin_s = _rope_tables(jnp.pad(pos_sample, ((0, 0), (0, TSP - TS))))
    new_p, new_s = [], []
    kv_p, kv_s = None, None
    for l in range(depth):
        prm = _layer_params(l, norm_mix, w_in, rwkv_mu, rwkv_w0, rwkv_w2, rwkv_a0, rwkv_a2, rwkv_g2, rwkv_k_k,
                            rwkv_k_a, rwkv_r_k, rwkv_ln_w, rwkv_ln_b, conv_w, conv_b, conv_ln_w, conv_ln_b, w_out,
                            norm_ffn, w_up, ffn_conv_w, ffn_conv_b, w_down)
        xp, st_p, kv_p = _prompt_layer(xp, B, S, prm, cos_p, sin_p, l, depth, kv_p)
        carry_s = (state_rwkv_shift[l], state_rwkv_wkv[l], state_conv[l], state_ffn_conv[l])
        bufs = (state_attn_k[l].reshape(NB, win_buf, MIX), state_attn_v[l].reshape(NB, win_buf, MIX))
        xs, st_s, kv_s = _sample_layer(xs, NB, TSP, TS, prm, cos_s, sin_s, carry_s, bufs, l, depth, kv_s)
        new_p.append(st_p)
        new_s.append(st_s)

    g = norm_final.reshape(1, D_MODEL)
    y_prompt = rmsnorm(xp, g, 512).reshape(B, S, D_MODEL)
    y_sample = rmsnorm(xs, g, 512).reshape(NB, TSP, D_MODEL)[:, :TS]

    def stack(states, i):
        return jnp.stack([st[i] for st in states], axis=0)

    heads_p = (depth, B, win_buf, HEADS, HEAD_DIM)
    heads_s = (depth, NB, win_buf, HEADS, HEAD_DIM)
    return (y_prompt, y_sample,
            stack(new_p, 0), stack(new_s, 0), stack(new_p, 1), stack(new_s, 1),
            kv_p[0].reshape(heads_p), kv_s[0].reshape(heads_s), kv_p[1].reshape(heads_p), kv_s[1].reshape(heads_s),
            stack(new_p, 2), stack(new_s, 2), stack(new_p, 3), stack(new_s, 3))
```

```python
import functools

import jax
import jax.numpy as jnp
from jax import lax
from jax.experimental import pallas as pl
from jax.experimental.pallas import tpu as pltpu

F32 = jnp.float32
BF16 = jnp.bfloat16

D_MODEL = 4096
HEAD_DIM = 64
HEADS = 24
MIX = HEADS * HEAD_DIM
CONV_DIM = 1024
DECAY_LORA = 128
AAA_LORA = 128
GATE_LORA = 480
GATE_PAD = 512
RWKV_PROJ = 3 * MIX + DECAY_LORA + AAA_LORA + GATE_LORA
RWKV_PAD = 3 * MIX + DECAY_LORA + AAA_LORA + GATE_PAD
ATT_PROJ = 3 * MIX
CONV_PROJ = 2 * CONV_DIM
CONV_WIDTH = 31
CONV_HALO = 32
D_FF = 11008
FFN_CONV_WIDTH = 3
FFN_HALO = 8
BAND = 128
DILATIONS = (1, 4, 16)
WINDOWS = (128, 512, 2048)
ROPE_THETA = 10000.0
RMS_EPS = 1e-6
LN_EPS = 1e-5
GN_EPS = 64e-5
NEG_INF = -1e30
CHUNK = 64

C_RWKV = 0
C_ATT = RWKV_PAD
C_CONV = C_ATT + ATT_PROJ
NP = 12288
LORA_IN = DECAY_LORA + AAA_LORA + GATE_PAD
M_RWKV, M_ATT, M_CONV = 0, MIX, 2 * MIX

VMEM_LIMIT = 56 * 1024 * 1024

NT = (((1,), (1,)), ((), ()))
TN = (((0,), (0,)), ((), ()))


def _cparams(sem):
    return pltpu.CompilerParams(dimension_semantics=sem, vmem_limit_bytes=VMEM_LIMIT)


def _sigmoid(x):
    return 1.0 / (1.0 + jnp.exp(-x))


def _head0_mask(shape):
    return lax.broadcasted_iota(jnp.int32, shape, 1) % (2 * HEAD_DIM) < HEAD_DIM


def _norm_matmul_kernel(x_ref, g_ref, w_ref, o_ref, h_ref):
    @pl.when(pl.program_id(1) == 0)
    def _():
        x = x_ref[...]
        ms = jnp.mean(x * x, axis=-1, keepdims=True)
        h_ref[...] = (x * lax.rsqrt(ms + RMS_EPS) * g_ref[...]).astype(BF16)

    o_ref[...] = jnp.dot(h_ref[...], w_ref[...], preferred_element_type=F32)


def norm_matmul(x, g, w, layer, tm, tn, single_buffer_x=False):
    M, K = x.shape
    N = w.shape[2]
    tm = min(tm, M)
    assert M % tm == 0 and N % tn == 0
    x_mode = dict(pipeline_mode=pl.Buffered(1)) if single_buffer_x else {}
    return pl.pallas_call(
        _norm_matmul_kernel,
        grid=(M // tm, N // tn),
        in_specs=[pl.BlockSpec((tm, K), lambda i, j: (i, 0), **x_mode),
                  pl.BlockSpec((1, K), lambda i, j: (0, 0)),
                  pl.BlockSpec((None, K, tn), lambda i, j: (layer, 0, j))],
        out_specs=pl.BlockSpec((tm, tn), lambda i, j: (i, j)),
        out_shape=jax.ShapeDtypeStruct((M, N), F32),
        scratch_shapes=[pltpu.VMEM((tm, K), BF16)],
        compiler_params=_cparams(("parallel", "arbitrary")),
        name="norm_matmul",
    )(x, g, w)


def _matmul_res_kernel(a_ref, w_ref, r_ref, o_ref, acc_ref, *, nk):
    k = pl.program_id(2)

    @pl.when(k == 0)
    def _():
        acc_ref[...] = r_ref[...]

    acc_ref[...] += jnp.dot(a_ref[...], w_ref[...], preferred_element_type=F32)

    @pl.when(k == nk - 1)
    def _():
        o_ref[...] = acc_ref[...]


def matmul_res(a, w, layer, res, tm, tn, tk):
    M, K = a.shape
    N = w.shape[2]
    tm = min(tm, M)
    assert M % tm == 0 and N % tn == 0 and K % tk == 0
    nk = K // tk
    return pl.pallas_call(
        functools.partial(_matmul_res_kernel, nk=nk),
        grid=(M // tm, N // tn, nk),
        in_specs=[pl.BlockSpec((tm, tk), lambda i, j, k: (i, k)),
                  pl.BlockSpec((None, tk, tn), lambda i, j, k: (layer, k, j)),
                  pl.BlockSpec((tm, tn), lambda i, j, k: (i, j))],
        out_specs=pl.BlockSpec((tm, tn), lambda i, j, k: (i, j)),
        out_shape=jax.ShapeDtypeStruct((M, N), F32),
        scratch_shapes=[pltpu.VMEM((tm, tn), F32)],
        compiler_params=_cparams(("parallel", "parallel", "arbitrary")),
        name="matmul_res",
    )(a, w, res)


def _rmsnorm_kernel(x_ref, g_ref, o_ref):
    x = x_ref[...]
    ms = jnp.mean(x * x, axis=-1, keepdims=True)
    o_ref[...] = x * lax.rsqrt(ms + RMS_EPS) * g_ref[...]


def rmsnorm(x, g, tm):
    M, K = x.shape
    tm = min(tm, M)
    return pl.pallas_call(
        _rmsnorm_kernel,
        grid=(M // tm,),
        in_specs=[pl.BlockSpec((tm, K), lambda i: (i, 0)), pl.BlockSpec((1, K), lambda i: (0, 0))],
        out_specs=pl.BlockSpec((tm, K), lambda i: (i, 0)),
        out_shape=jax.ShapeDtypeStruct((M, K), F32),
        compiler_params=_cparams(("parallel",)),
        name="rmsnorm",
    )(x, g)


def _rot_half(x, first_half):
    w = x.shape[1]
    return jnp.where(first_half, pltpu.roll(x, w - HEAD_DIM // 2, 1), pltpu.roll(x, HEAD_DIM // 2, 1))


ROPE_BLOCK = 256


def _rope_kernel(q_ref, k_ref, cos_ref, sin_ref, qo_ref, ko_ref):
    reps = ROPE_BLOCK // (2 * HEAD_DIM)
    cos = jnp.concatenate([cos_ref[0]] * reps, axis=1)
    sin = jnp.concatenate([sin_ref[0]] * reps, axis=1)
    lane = lax.broadcasted_iota(jnp.int32, cos.shape, 1)
    first_half = (lane % HEAD_DIM) < (HEAD_DIM // 2)
    q = q_ref[0]
    k = k_ref[0]
    qo_ref[0] = q * cos + _rot_half(q, first_half) * sin
    ko_ref[0] = k * cos + _rot_half(k, first_half) * sin


def rope(proj3, cos_t, sin_t, tt):
    N, T, _ = proj3.shape
    tt = min(tt, T)
    tab = lambda n, i, j: (n, i, 0)
    rw = ROPE_BLOCK
    assert C_ATT % rw == 0 and MIX % rw == 0
    qb = C_ATT // rw
    kb = (C_ATT + MIX) // rw
    out = jax.ShapeDtypeStruct((N, T, MIX), F32)
    return pl.pallas_call(
        _rope_kernel,
        grid=(N, T // tt, MIX // rw),
        in_specs=[pl.BlockSpec((1, tt, rw), lambda n, i, j: (n, i, qb + j)),
                  pl.BlockSpec((1, tt, rw), lambda n, i, j: (n, i, kb + j)),
                  pl.BlockSpec((1, tt, 128), tab),
                  pl.BlockSpec((1, tt, 128), tab)],
        out_specs=[pl.BlockSpec((1, tt, rw), lambda n, i, j: (n, i, j)),
                   pl.BlockSpec((1, tt, rw), lambda n, i, j: (n, i, j))],
        out_shape=[out, out],
        compiler_params=_cparams(("parallel", "parallel", "parallel")),
        name="rope",
    )(proj3, proj3, cos_t, sin_t)


def _prompt_attn_kernel(*refs):
    q_ref, k_ref, v_ref, cos_ref, sin_ref = refs[:5]
    o_ref, ko_all_ref, vo_all_ref, qs_ref, acc_ref, m_ref, l_ref = refs[-7:]
    ko_ref = ko_all_ref.at[0]
    vo_ref = vo_all_ref.at[0]
    S = q_ref.shape[1]
    B = BAND
    P = 2 * HEAD_DIM
    lane = lax.broadcasted_iota(jnp.int32, (B, P), 1)
    first_half = (lane % HEAD_DIM) < (HEAD_DIM // 2)
    h0 = _head0_mask((B, P))
    scale = HEAD_DIM ** -0.5

    def rope_rows(i, carry):
        rows = pl.ds(pl.multiple_of(i * B, B), B)
        cos = cos_ref[0, rows, :]
        sin = sin_ref[0, rows, :]
        q = q_ref[0, rows, :]
        k = k_ref[0, rows, :]
        qs_ref[rows, :] = q * cos + _rot_half(q, first_half) * sin
        ko_ref[0, rows, :] = k * cos + _rot_half(k, first_half) * sin
        vo_ref[0, rows, :] = v_ref[0, rows, :]
        return carry

    lax.fori_loop(0, S // B, rope_rows, 0)

    def attend(q, kk, vv, valid):
        ms, ls, os_ = [], [], []
        for h in range(2):
            qh = jnp.where(h0 if h == 0 else ~h0, q, 0.0).astype(BF16)
            s = lax.dot_general(qh, kk, NT, preferred_element_type=F32) * scale
            s = jnp.where(valid, s, NEG_INF)
            m = jnp.max(s, axis=-1, keepdims=True)
            e = jnp.exp(s - m)
            ms.append(m)
            ls.append(jnp.sum(e, axis=-1, keepdims=True))
            os_.append(jnp.dot(e.astype(BF16), vv, preferred_element_type=F32))
        return jnp.where(h0, ms[0], ms[1]), jnp.where(h0, ls[0], ls[1]), jnp.where(h0, os_[0], os_[1])

    def merge(rows, m_c, l_c, o_c):
        m_p = m_ref[rows, :]
        m_n = jnp.maximum(m_p, m_c)
        a_p = jnp.exp(m_p - m_n)
        a_c = jnp.exp(m_c - m_n)
        acc_ref[rows, :] = acc_ref[rows, :] * a_p + o_c * a_c
        l_ref[rows, :] = l_ref[rows, :] * a_p + l_c * a_c
        m_ref[rows, :] = m_n

    qi2 = lax.broadcasted_iota(jnp.int32, (B, 2 * B), 0)
    ki2 = lax.broadcasted_iota(jnp.int32, (B, 2 * B), 1)
    dist2 = qi2 + B - ki2
    band2 = (dist2 >= 0) & (dist2 <= B)
    qi1 = lax.broadcasted_iota(jnp.int32, (B, B), 0)
    ki1 = lax.broadcasted_iota(jnp.int32, (B, B), 1)
    causal1 = qi1 >= ki1

    def two_block_keys(cur, prev):
        kk = jnp.concatenate([ko_ref[0, prev, :], ko_ref[0, cur, :]], axis=0).astype(BF16)
        vv = jnp.concatenate([v_ref[0, prev, :], v_ref[0, cur, :]], axis=0).astype(BF16)
        return kk, vv

    d16 = DILATIONS[2]
    assert S == B * d16
    per_iter = 4

    def dil16(it, carry):
        for u in range(per_iter):
            cur = pl.ds(it * per_iter + u, B, stride=d16)
            m_c, l_c, o_c = attend(qs_ref[cur, :], ko_ref[0, cur, :].astype(BF16), v_ref[0, cur, :].astype(BF16), causal1)
            acc_ref[cur, :] = o_c
            m_ref[cur, :] = m_c
            l_ref[cur, :] = l_c
        return carry

    lax.fori_loop(0, d16 // per_iter, dil16, 0)

    d4 = DILATIONS[1]

    def dil4(ib, carry):
        base = pl.multiple_of(ib * (B * d4), B * d4)
        pbase = pl.multiple_of(jnp.maximum(ib - 1, 0) * (B * d4), B * d4)
        for r in range(d4):
            cur = pl.ds(base + r, B, stride=d4)
            prev = pl.ds(pbase + r, B, stride=d4)
            kk, vv = two_block_keys(cur, prev)
            m_c, l_c, o_c = attend(qs_ref[cur, :], kk, vv, band2 & ((ki2 >= B) | (ib > 0)))
            merge(cur, m_c, l_c, o_c)
        return carry

    lax.fori_loop(0, S // (B * d4), dil4, 0)

    unroll = 4

    def dil1(it, carry):
        for u in range(unroll):
            ib = it * unroll + u
            cur = pl.ds(pl.multiple_of(ib * B, B), B)
            prev = pl.ds(pl.multiple_of(jnp.maximum(ib - 1, 0) * B, B), B)
            kk, vv = two_block_keys(cur, prev)
            m_c, l_c, o_c = attend(qs_ref[cur, :], kk, vv, band2 & ((ki2 >= B) | (ib > 0)))
            m_p = m_ref[cur, :]
            m_n = jnp.maximum(m_p, m_c)
            a_p = jnp.exp(m_p - m_n)
            a_c = jnp.exp(m_c - m_n)
            acc = acc_ref[cur, :] * a_p + o_c * a_c
            l_n = l_ref[cur, :] * a_p + l_c * a_c
            o_ref[0, cur, :] = (acc / l_n).astype(BF16)
        return carry

    lax.fori_loop(0, S // (B * unroll), dil1, 0)


def prompt_attention(proj3, cos_t, sin_t, mixed, layer, depth, kv_all=None):
    N, S, _ = proj3.shape
    qb, kb, vb = C_ATT // 128, (C_ATT + MIX) // 128, (C_ATT + 2 * MIX) // 128
    col = lambda b: pl.BlockSpec((1, S, 128), lambda n, h: (n, 0, b + h))
    tab = pl.BlockSpec((1, S, 128), lambda n, h: (0, 0, 0))
    anyspec = pl.BlockSpec(memory_space=pl.ANY)
    kv = jax.ShapeDtypeStruct((depth, N, S, MIX), F32)
    kv_spec = pl.BlockSpec((1, 1, S, 128), lambda n, h: (layer, n, 0, h))
    args = [proj3, proj3, proj3, cos_t, sin_t, mixed]
    in_specs = [col(qb), col(kb), col(vb), tab, tab, anyspec]
    aliases = {5: 0}
    if kv_all is not None:
        args += list(kv_all)
        in_specs += [anyspec, anyspec]
        aliases.update({6: 1, 7: 2})
    return pl.pallas_call(
        _prompt_attn_kernel,
        grid=(N, HEADS // 2),
        in_specs=in_specs,
        out_specs=[col(M_ATT // 128), kv_spec, kv_spec],
        out_shape=[jax.ShapeDtypeStruct(mixed.shape, mixed.dtype), kv, kv],
        scratch_shapes=[pltpu.VMEM((S, 128), F32)] * 4,
        input_output_aliases=aliases,
        compiler_params=_cparams(("parallel", "parallel")),
        name="prompt_attention",
    )(*args)


def _sample_attn_kernel(*refs, t_valid):
    q_ref, kn_ref, vn_ref, kb_ref, vb_ref = refs[:5]
    o_ref, ks_all_ref, vs_all_ref = refs[-3:]
    ks_ref = ks_all_ref.at[0]
    vs_ref = vs_all_ref.at[0]
    TQ = q_ref.shape[1]
    M = kb_ref.shape[1]
    TN_ = kn_ref.shape[1]
    h0 = _head0_mask((TQ, 2 * HEAD_DIM))
    scale = HEAD_DIM ** -0.5

    def counts(delta, in_range):
        c = jnp.zeros(delta.shape, F32)
        for win, dil in zip(WINDOWS, DILATIONS):
            c = c + jnp.where((delta % dil == 0) & (delta <= win) & in_range, 1.0, 0.0)
        return c

    tq = lax.broadcasted_iota(jnp.int32, (TQ, M), 0)
    kb_i = lax.broadcasted_iota(jnp.int32, (TQ, M), 1)
    d_buf = M + tq - kb_i
    c_buf = counts(d_buf, d_buf >= 0)
    tq2 = lax.broadcasted_iota(jnp.int32, (TQ, TN_), 0)
    tn2 = lax.broadcasted_iota(jnp.int32, (TQ, TN_), 1)
    d_new = tq2 - tn2
    c_new = counts(d_new, (d_new >= 0) & (tn2 < t_valid))

    row8 = lax.broadcasted_iota(jnp.int32, (8, 2 * HEAD_DIM), 0)
    for p in range(4):
        sl = slice(128 * p, 128 * (p + 1))
        qp = q_ref[0, :, sl]
        kb = kb_ref[0, :, sl]
        vb = vb_ref[0, :, sl]
        kn = kn_ref[0, :, sl]
        vn = vn_ref[0, :, sl]
        kb16, vb16, kn16, vn16 = kb.astype(BF16), vb.astype(BF16), kn.astype(BF16), vn.astype(BF16)
        outs = []
        for h in range(2):
            qh = jnp.where(h0 if h == 0 else ~h0, qp, 0.0).astype(BF16)
            s_b = lax.dot_general(qh, kb16, NT, preferred_element_type=F32) * scale
            s_n = lax.dot_general(qh, kn16, NT, preferred_element_type=F32) * scale
            s_b = jnp.where(c_buf > 0, s_b, NEG_INF)
            s_n = jnp.where(c_new > 0, s_n, NEG_INF)
            m = jnp.maximum(jnp.max(s_b, axis=-1, keepdims=True), jnp.max(s_n, axis=-1, keepdims=True))
            w_b = c_buf * jnp.exp(s_b - m)
            w_n = c_new * jnp.exp(s_n - m)
            l = jnp.sum(w_b, axis=-1, keepdims=True) + jnp.sum(w_n, axis=-1, keepdims=True)
            o = (jnp.dot(w_b.astype(BF16), vb16, preferred_element_type=F32)
                 + jnp.dot(w_n.astype(BF16), vn16, preferred_element_type=F32))
            outs.append(o / l)
        o_ref[0, :, sl] = jnp.where(h0, outs[0], outs[1]).astype(BF16)

        for src, new, dst in ((kb, kn, ks_ref), (vb, vn, vs_ref)):
            rolled = pltpu.roll(src, M - t_valid, 0)
            new_r = pltpu.roll(new[0:8], 8 - t_valid, 0)
            dst[0, 0:M - 8, sl] = rolled[0:M - 8]
            dst[0, M - 8:M, sl] = jnp.where(row8 >= 8 - t_valid, new_r, rolled[M - 8:M])


def sample_attention(q_rot, k_new, v_new, k_buf, v_buf, t_valid, layer, depth, kv_all=None):
    N, TQ, _ = q_rot.shape
    M = k_buf.shape[2]
    TN_ = k_new.shape[1]
    assert t_valid <= 8 <= TQ
    blk = lambda r: pl.BlockSpec((1, r, 512), lambda n, j: (n, 0, j))
    buf = pl.BlockSpec((None, 1, M, 512), lambda n, j: (layer, n, 0, j))
    kv = jax.ShapeDtypeStruct((depth, N, M, MIX), F32)
    kv_spec = pl.BlockSpec((1, 1, M, 512), lambda n, j: (layer, n, 0, j))
    args = [q_rot, k_new, v_new, k_buf, v_buf]
    in_specs = [blk(TQ), blk(TN_), blk(TN_), buf, buf]
    aliases = {}
    if kv_all is not None:
        args += list(kv_all)
        in_specs += [pl.BlockSpec(memory_space=pl.ANY)] * 2
        aliases = {5: 1, 6: 2}
    return pl.pallas_call(
        functools.partial(_sample_attn_kernel, t_valid=t_valid),
        grid=(N, MIX // 512),
        in_specs=in_specs,
        out_specs=[blk(TQ), kv_spec, kv_spec],
        out_shape=[jax.ShapeDtypeStruct((N, TQ, MIX), BF16), kv, kv],
        input_output_aliases=aliases,
        compiler_params=_cparams(("parallel", "parallel")),
        name="sample_attention",
    )(*args)


def _glu_kernel(a_ref, g_ref, o_ref):
    o_ref[0] = a_ref[0] * _sigmoid(g_ref[0])


def glu(proj3, tt):
    N, T, _ = proj3.shape
    tt = min(tt, T)
    gw = 256
    assert C_CONV % gw == 0 and CONV_DIM % gw == 0
    ab = C_CONV // gw
    gb = (C_CONV + CONV_DIM) // gw
    return pl.pallas_call(
        _glu_kernel,
        grid=(N, T // tt, CONV_DIM // gw),
        in_specs=[pl.BlockSpec((1, tt, gw), lambda n, i, j: (n, i, ab + j)),
                  pl.BlockSpec((1, tt, gw), lambda n, i, j: (n, i, gb + j))],
        out_specs=pl.BlockSpec((1, tt, gw), lambda n, i, j: (n, i, j)),
        out_shape=jax.ShapeDtypeStruct((N, T, CONV_DIM), F32),
        compiler_params=_cparams(("parallel", "parallel", "parallel")),
        name="glu",
    )(proj3, proj3)


def _dwconv_kernel(u_ref, uh_ref, h0_ref, w_ref, b_ref, lw_ref, lb_ref, *rest, single_tile):
    o_ref, ext_ref, y_ref = rest[-3:]
    tt = u_ref.shape[1]
    i = pl.program_id(1)

    @pl.when(i == 0)
    def _():
        ext_ref[0:CONV_HALO, :] = h0_ref[0]

    if not single_tile:
        @pl.when(i > 0)
        def _():
            ext_ref[0:CONV_HALO, :] = uh_ref[0]

    ext_ref[CONV_HALO:, :] = u_ref[0]
    lead = CONV_HALO - (CONV_WIDTH - 1)
    rows = min(16, tt)
    for r0 in range(0, tt, rows):
        for c0 in range(0, CONV_DIM, 512):
            acc = jnp.broadcast_to(b_ref[:, c0:c0 + 512], (rows, 512))
            for j in range(CONV_WIDTH):
                acc = acc + ext_ref[r0 + lead + j:r0 + lead + j + rows, c0:c0 + 512] * w_ref[j:j + 1, c0:c0 + 512]
            y_ref[r0:r0 + rows, c0:c0 + 512] = acc
    y = y_ref[...]
    mu = jnp.mean(y, axis=-1, keepdims=True)
    var = jnp.mean(jnp.square(y - mu), axis=-1, keepdims=True)
    c = (y - mu) * lax.rsqrt(var + LN_EPS) * lw_ref[...] + lb_ref[...]
    o_ref[0] = (c * _sigmoid(c)).astype(BF16)


def dwconv_ln_silu(u, halo0, w, b, lw, lb, tt, mixed=None):
    N, T, C = u.shape
    tt = min(tt, T)
    hb = tt // CONV_HALO if tt >= CONV_HALO else 1
    in_specs = [pl.BlockSpec((1, tt, C), lambda n, i: (n, i, 0)),
                pl.BlockSpec((1, min(CONV_HALO, T), C), lambda n, i: (n, jnp.maximum(i * hb - 1, 0), 0)),
                pl.BlockSpec((1, CONV_HALO, C), lambda n, i: (n, 0, 0)),
                pl.BlockSpec((32, C), lambda n, i: (0, 0)),
                pl.BlockSpec((1, C), lambda n, i: (0, 0)),
                pl.BlockSpec((1, C), lambda n, i: (0, 0)),
                pl.BlockSpec((1, C), lambda n, i: (0, 0))]
    args = [u, u, halo0, w, b, lw, lb]
    if mixed is None:
        out_shape = jax.ShapeDtypeStruct((N, T, C), BF16)
        out_spec = pl.BlockSpec((1, tt, C), lambda n, i: (n, i, 0))
        aliases = {}
    else:
        in_specs.append(pl.BlockSpec(memory_space=pl.ANY))
        args.append(mixed)
        out_shape = jax.ShapeDtypeStruct(mixed.shape, mixed.dtype)
        out_spec = pl.BlockSpec((1, tt, C), lambda n, i: (n, i, M_CONV // C))
        aliases = {7: 0}
    return pl.pallas_call(
        functools.partial(_dwconv_kernel, single_tile=(T == tt)),
        grid=(N, T // tt),
        in_specs=in_specs,
        out_specs=out_spec,
        out_shape=out_shape,
        scratch_shapes=[pltpu.VMEM((CONV_HALO + tt, C), F32), pltpu.VMEM((tt, C), F32)],
        input_output_aliases=aliases,
        compiler_params=_cparams(("parallel", "arbitrary")),
        name="dwconv_ln_silu",
    )(*args)


def _ffn_conv_act(ext_ref, wg_ref, wv_ref, bg_ref, bv_ref, store, tt):
    lead = FFN_HALO - (FFN_CONV_WIDTH - 1)
    rows = min(32, tt)
    for r0 in range(0, tt, rows):
        ys = []
        for s, w_ref, b_ref in ((0, wg_ref, bg_ref), (1, wv_ref, bv_ref)):
            acc = jnp.broadcast_to(b_ref[...], (rows, b_ref.shape[1]))
            for j in range(FFN_CONV_WIDTH):
                acc = acc + ext_ref[s, r0 + lead + j:r0 + lead + j + rows, :] * w_ref[j:j + 1, :]
            ys.append(acc)
        gate, val = ys
        store(r0, rows, (gate * _sigmoid(gate) * val).astype(BF16))


def _ffn_act_kernel(ug_ref, uv_ref, hg_ref, hv_ref, h0g_ref, h0v_ref, wg_ref, wv_ref, bg_ref, bv_ref, o_ref, ext_ref):
    tt = ug_ref.shape[1]
    i = pl.program_id(1)

    @pl.when(i == 0)
    def _():
        ext_ref[0, 0:FFN_HALO, :] = h0g_ref[0]
        ext_ref[1, 0:FFN_HALO, :] = h0v_ref[0]

    @pl.when(i > 0)
    def _():
        ext_ref[0, 0:FFN_HALO, :] = hg_ref[0]
        ext_ref[1, 0:FFN_HALO, :] = hv_ref[0]

    ext_ref[0, FFN_HALO:, :] = ug_ref[0]
    ext_ref[1, FFN_HALO:, :] = uv_ref[0]

    def store(r0, rows, val):
        o_ref[0, r0:r0 + rows, :] = val

    _ffn_conv_act(ext_ref, wg_ref, wv_ref, bg_ref, bv_ref, store, tt)


def ffn_act(u3, halo0, w, b, tt, tf):
    N, T, F2 = u3.shape
    F = F2 // 2
    tt = min(tt, T)
    nf = F // tf
    hb = tt // FFN_HALO
    g = lambda n, i, j: (n, i, j)
    v = lambda n, i, j: (n, i, j + nf)
    hg = lambda n, i, j: (n, jnp.maximum(i * hb - 1, 0), j)
    hv = lambda n, i, j: (n, jnp.maximum(i * hb - 1, 0), j + nf)
    return pl.pallas_call(
        _ffn_act_kernel,
        grid=(N, T // tt, nf),
        in_specs=[pl.BlockSpec((1, tt, tf), g), pl.BlockSpec((1, tt, tf), v),
                  pl.BlockSpec((1, FFN_HALO, tf), hg), pl.BlockSpec((1, FFN_HALO, tf), hv),
                  pl.BlockSpec((1, FFN_HALO, tf), lambda n, i, j: (n, 0, j)),
                  pl.BlockSpec((1, FFN_HALO, tf), lambda n, i, j: (n, 0, j + nf)),
                  pl.BlockSpec((8, tf), lambda n, i, j: (0, j)), pl.BlockSpec((8, tf), lambda n, i, j: (0, j + nf)),
                  pl.BlockSpec((1, tf), lambda n, i, j: (0, j)), pl.BlockSpec((1, tf), lambda n, i, j: (0, j + nf))],
        out_specs=pl.BlockSpec((1, tt, tf), g),
        out_shape=jax.ShapeDtypeStruct((N, T, F), BF16),
        scratch_shapes=[pltpu.VMEM((2, FFN_HALO + tt, tf), F32)],
        compiler_params=_cparams(("parallel", "parallel", "parallel")),
        name="ffn_act",
    )(u3, u3, u3, u3, halo0, halo0, w, w, b, b)


def _ffn_fused_kernel(x_ref, g_ref, wg_ref, wv_ref, wd_ref, cwg_ref, cwv_ref, cbg_ref, cbv_ref, h0g_ref, h0v_ref,
                      o_ref, st_ref, h_ref, ext_ref, carry_ref, act_ref, *, tiles_per_seq):
    i = pl.program_id(0)
    j = pl.program_id(1)
    tm = x_ref.shape[0]

    @pl.when(j == 0)
    def _():
        x = x_ref[...]
        ms = jnp.mean(x * x, axis=-1, keepdims=True)
        h_ref[...] = (x * lax.rsqrt(ms + RMS_EPS) * g_ref[...]).astype(BF16)
        o_ref[...] = x

    h = h_ref[...]
    ug = jnp.dot(h, wg_ref[...], preferred_element_type=F32)
    uv = jnp.dot(h, wv_ref[...], preferred_element_type=F32)
    seq_start = (i % tiles_per_seq) == 0
    ext_ref[0, 0:FFN_HALO, :] = jnp.where(seq_start, h0g_ref[0], carry_ref[j, 0])
    ext_ref[1, 0:FFN_HALO, :] = jnp.where(seq_start, h0v_ref[0], carry_ref[j, 1])
    ext_ref[0, FFN_HALO:, :] = ug
    ext_ref[1, FFN_HALO:, :] = uv
    carry_ref[j, 0] = ug[tm - FFN_HALO:, :]
    carry_ref[j, 1] = uv[tm - FFN_HALO:, :]

    def store(r0, rows, val):
        act_ref[r0:r0 + rows, :] = val

    _ffn_conv_act(ext_ref, cwg_ref, cwv_ref, cbg_ref, cbv_ref, store, tm)
    o_ref[...] += jnp.dot(act_ref[...], wd_ref[...], preferred_element_type=F32)

    @pl.when((j == pl.num_programs(1) - 1) & (i % tiles_per_seq == tiles_per_seq - 1))
    def _():
        st_ref[0] = carry_ref[...]


def ffn_fused(x, g, w_up, w_down, layer, cw, cb, halo0, seq_len, tm, tf):
    M, D = x.shape
    F = w_down.shape[1]
    assert M % tm == 0 and seq_len % tm == 0 and F % tf == 0
    nf = F // tf
    tps = seq_len // tm
    gate = lambda i, j: (0, j)
    val = lambda i, j: (0, j + nf)
    return pl.pallas_call(
        functools.partial(_ffn_fused_kernel, tiles_per_seq=tps),
        grid=(M // tm, nf),
        in_specs=[pl.BlockSpec((tm, D), lambda i, j: (i, 0), pipeline_mode=pl.Buffered(1)),
                  pl.BlockSpec((1, D), lambda i, j: (0, 0)),
                  pl.BlockSpec((None, D, tf), lambda i, j: (layer, 0, j)),
                  pl.BlockSpec((None, D, tf), lambda i, j: (layer, 0, j + nf)),
                  pl.BlockSpec((None, tf, D), lambda i, j: (layer, j, 0)),
                  pl.BlockSpec((8, tf), gate), pl.BlockSpec((8, tf), val),
                  pl.BlockSpec((1, tf), gate), pl.BlockSpec((1, tf), val),
                  pl.BlockSpec((1, FFN_HALO, tf), lambda i, j: (i // tps, 0, j)),
                  pl.BlockSpec((1, FFN_HALO, tf), lambda i, j: (i // tps, 0, j + nf))],
        out_specs=[pl.BlockSpec((tm, D), lambda i, j: (i, 0)),
                   pl.BlockSpec((1, nf, 2, FFN_HALO, tf), lambda i, j: (i // tps, 0, 0, 0, 0))],
        out_shape=[jax.ShapeDtypeStruct((M, D), F32),
                   jax.ShapeDtypeStruct((M // seq_len, nf, 2, FFN_HALO, tf), F32)],
        scratch_shapes=[pltpu.VMEM((tm, D), BF16), pltpu.VMEM((2, FFN_HALO + tm, tf), F32),
                        pltpu.VMEM((nf, 2, FFN_HALO, tf), F32), pltpu.VMEM((tm, tf), BF16)],
        compiler_params=_cparams(("arbitrary", "arbitrary")),
        name="ffn_fused",
    )(x, g, w_up, w_up, w_down, cw, cw, cb, cb, halo0, halo0)


def _rwkv_kernel(r_ref, k_ref, v_ref, lo_ref,
                 sr_ref, sk_ref, sv_ref, slo_ref,
                 mur_ref, muk_ref, muv_ref, mulo_ref,
                 w0_ref, a0_ref, kk_ref, ka_ref, rk_ref, lnw_ref, lnb_ref,
                 w2_ref, a2_ref, g2_ref, s0_ref,
                 o_ref, so_ref,
                 S_ref, pr_ref, pk_ref, pv_ref, plo_ref, *, t_valid, t_total):
    c = pl.program_id(2)
    Tc = r_ref.shape[1]
    PW = 2 * HEAD_DIM
    n_pairs = r_ref.shape[2] // PW

    @pl.when(c == 0)
    def _():
        S_ref[...] = s0_ref[0]
        pr_ref[...] = sr_ref[0]
        pk_ref[...] = sk_ref[0]
        pv_ref[...] = sv_ref[0]
        plo_ref[...] = slo_ref[0]

    def lerp(p_ref, prev_ref, mu_ref):
        p = p_ref[0]
        row = lax.broadcasted_iota(jnp.int32, p.shape, 0)
        shifted = jnp.where(row == 0, prev_ref[...], pltpu.roll(p, 1, 0))
        prev_ref[...] = p[Tc - 1:Tc, :]
        return p + (shifted - p) * mu_ref[...]

    xr = lerp(r_ref, pr_ref, mur_ref)
    xk = lerp(k_ref, pk_ref, muk_ref)
    xv = lerp(v_ref, pv_ref, muv_ref)
    xlo = lerp(lo_ref, plo_ref, mulo_ref)
    xw = xlo[:, :DECAY_LORA]
    xa = xlo[:, DECAY_LORA:DECAY_LORA + AAA_LORA]
    xg = xlo[:, DECAY_LORA + AAA_LORA:]

    h0p = _head0_mask((Tc, PW))

    def head_sum(x):
        parts = []
        for p in range(n_pairs):
            xp = x[:, PW * p:PW * (p + 1)]
            s_0 = jnp.sum(jnp.where(h0p, xp, 0.0), axis=-1, keepdims=True)
            s_1 = jnp.sum(jnp.where(h0p, 0.0, xp), axis=-1, keepdims=True)
            parts.append(jnp.where(h0p, s_0, s_1))
        return parts[0] if n_pairs == 1 else jnp.concatenate(parts, axis=1)

    def mm(x, y_):
        return jnp.dot(x, y_, preferred_element_type=F32)

    z = w0_ref[...] + mm(jnp.tanh(xw).astype(BF16), w2_ref[...])
    w_log = jnp.minimum(z, 0.0) - jnp.log(1.0 + jnp.exp(-jnp.abs(z))) - 0.5
    ld = -jnp.exp(w_log)
    a = _sigmoid(a0_ref[...] + mm(xa.astype(BF16), a2_ref[...]))
    g = mm(_sigmoid(xg).astype(BF16), g2_ref[...])

    kk = xk * kk_ref[...]
    kk = kk * lax.rsqrt(jnp.maximum(head_sum(kk * kk), 1e-24))
    kmod = xk * (1.0 + (a - 1.0) * ka_ref[...])
    avec = -kk
    bvec = kk * a
    bonus = head_sum(xr * kmod * rk_ref[...]) * xv
    vval = xv

    if t_valid < t_total:
        row = lax.broadcasted_iota(jnp.int32, (Tc, n_pairs * PW), 0) + c * Tc
        ok = row < t_valid
        ld = jnp.where(ok, ld, 0.0)
        avec = jnp.where(ok, avec, 0.0)
        bvec = jnp.where(ok, bvec, 0.0)
        kmod_s = jnp.where(ok, kmod, 0.0)
        vval = jnp.where(ok, vval, 0.0)
    else:
        kmod_s = kmod

    C = CHUNK
    tri_r = lax.broadcasted_iota(jnp.int32, (C, C), 0)
    tri_c = lax.broadcasted_iota(jnp.int32, (C, C), 1)
    tril = jnp.where(tri_r >= tri_c, 1.0, 0.0).astype(F32)
    it = lax.broadcasted_iota(jnp.int32, (2 * C, 2 * C), 0) % C
    js = lax.broadcasted_iota(jnp.int32, (2 * C, 2 * C), 1) % C
    strict = it > js
    incl = it >= js
    h0c = _head0_mask((C, 2 * HEAD_DIM))

    def stack(x):
        return jnp.concatenate([jnp.where(h0c, x, 0.0), jnp.where(h0c, 0.0, x)], axis=0)

    n_chunks = Tc // C
    grp = 2 if n_chunks % 2 == 0 else 1
    gw = 2 * C * grp
    eye_g = jnp.where(lax.broadcasted_iota(jnp.int32, (gw, gw), 0)
                      == lax.broadcasted_iota(jnp.int32, (gw, gw), 1), 1.0, 0.0).astype(F32)
    zero_blk = jnp.zeros((2 * C, 2 * C), F32)

    def bdiag(blocks):
        if len(blocks) == 1:
            return blocks[0]
        return jnp.concatenate([jnp.concatenate([blocks[0], zero_blk], axis=1),
                                jnp.concatenate([zero_blk, blocks[1]], axis=1)], axis=0)

    def cat(blocks):
        return blocks[0] if len(blocks) == 1 else jnp.concatenate(blocks, axis=0)

    a16, r32, bb16, kb16, v16, gts, Ls, AKs, RBs, RKs = ({} for _ in range(10))
    units = [(p, j) for j in range(n_chunks) for p in range(n_pairs)]
    for p, j in units:
        sl = (slice(C * j, C * (j + 1)), slice(PW * p, PW * (p + 1)))
        ldc = ld[sl]
        cs = jnp.dot(tril, ldc, precision=lax.Precision.HIGHEST, preferred_element_type=F32)
        tot = cs[C - 1:C, :]
        eg = jnp.exp(cs)
        egi = jnp.exp(-cs)
        ege = jnp.exp(cs - ldc)
        et = jnp.exp(tot - cs)
        gts[p, j] = jnp.exp(tot)
        a_s = stack(avec[sl] * ege).astype(BF16)
        r_s = stack(xr[sl] * eg)
        b_s = stack(bvec[sl] * egi).astype(BF16)
        k_s = stack(kmod_s[sl] * egi).astype(BF16)
        sc = lax.dot_general(jnp.concatenate([a_s, r_s.astype(BF16)], axis=0), jnp.concatenate([b_s, k_s], axis=0), NT,
                             preferred_element_type=F32)
        a16[p, j] = a_s
        r32[p, j] = r_s
        bb16[p, j] = stack(bvec[sl] * et).astype(BF16)
        kb16[p, j] = stack(kmod_s[sl] * et).astype(BF16)
        v16[p, j] = stack(vval[sl]).astype(BF16)
        Ls[p, j] = jnp.where(strict, sc[:2 * C, :2 * C], 0.0)
        AKs[p, j] = jnp.where(strict, sc[:2 * C, 2 * C:], 0.0)
        RBs[p, j] = jnp.where(incl, sc[2 * C:, :2 * C], 0.0)
        RKs[p, j] = jnp.where(incl, sc[2 * C:, 2 * C:], 0.0)

    groups = [[(p, j) for j in range(g0, g0 + grp)] for g0 in range(0, n_chunks, grp) for p in range(n_pairs)]
    Ps = [bdiag([Ls[c_] for c_ in g_]) for g_ in groups]
    Tms = [eye_g + p_ for p_ in Ps]
    for _ in range(C.bit_length() - 2):
        for gi in range(len(groups)):
            P16 = Ps[gi].astype(BF16)
            Ps[gi] = mm(P16, P16)
            Tms[gi] = Tms[gi] + mm(Tms[gi].astype(BF16), Ps[gi].astype(BF16))

    RAs, YNs, Gs, Ns = {}, {}, {}, {}
    for gi, g_ in enumerate(groups):
        T16 = Tms[gi].astype(BF16)
        A2 = cat([a16[c_] for c_ in g_])
        V2 = cat([v16[c_] for c_ in g_])
        R2 = cat([r32[c_] for c_ in g_])
        RB16 = bdiag([RBs[c_] for c_ in g_]).astype(BF16)
        RK16 = bdiag([RKs[c_] for c_ in g_]).astype(BF16)
        TA16 = mm(T16, A2).astype(BF16)
        TV16 = mm(T16, mm(bdiag([AKs[c_] for c_ in g_]).astype(BF16), V2).astype(BF16)).astype(BF16)
        RA = R2 + mm(RB16, TA16)
        YN = mm(jnp.concatenate([RB16, RK16], axis=1), jnp.concatenate([TV16, V2], axis=0))
        for q, c_ in enumerate(g_):
            rs = slice(2 * C * q, 2 * C * (q + 1))
            RAs[c_] = RA[rs].astype(BF16)
            YNs[c_] = YN[rs]
            Gs[c_] = lax.dot_general(TA16[rs], bb16[c_], TN, preferred_element_type=F32).astype(BF16)
            Ns[c_] = lax.dot_general(jnp.concatenate([TV16[rs], v16[c_]], axis=0),
                                     jnp.concatenate([bb16[c_], kb16[c_]], axis=0), TN, preferred_element_type=F32)

    ys = {}
    Ss = [S_ref[p] for p in range(n_pairs)]
    for p, j in units:
        S16 = Ss[p].astype(BF16)
        Y = lax.dot_general(RAs[p, j], S16, NT, preferred_element_type=F32) + YNs[p, j]
        ys[p, j] = Y[:C] + Y[C:]
        Ss[p] = Ss[p] * gts[p, j] + mm(S16, Gs[p, j]) + Ns[p, j]
    for p in range(n_pairs):
        S_ref[p] = Ss[p]
    y_pairs = [cat([ys[p, j] for j in range(n_chunks)]) for p in range(n_pairs)]
    y = y_pairs[0] if n_pairs == 1 else jnp.concatenate(y_pairs, axis=1)

    mu = head_sum(y) * (1.0 / HEAD_DIM)
    var = head_sum(jnp.square(y - mu)) * (1.0 / HEAD_DIM)
    yn = (y - mu) * lax.rsqrt(var + GN_EPS) * lnw_ref[...] + lnb_ref[...]
    o_ref[0] = ((yn + bonus) * g).astype(BF16)

    @pl.when(c == pl.num_programs(2) - 1)
    def _():
        for p in range(n_pairs):
            so_ref[0, p] = Ss[p]


def rwkv_time_mix(proj3, shift0, s0, prm, t_valid, tc, out_width=MIX, pairs=2):
    N, T, _ = proj3.shape
    tc = min(tc, T)
    W = 2 * HEAD_DIM * pairs
    assert T % tc == 0 and tc % CHUNK == 0 and MIX % W == 0 and C_RWKV % W == 0
    rb = C_RWKV // W
    nb = MIX // W
    pspec = lambda w, off: pl.BlockSpec((1, tc, w), lambda n, h, c: (n, c, off(h)))
    sspec = lambda w, off: pl.BlockSpec((1, 1, w), lambda n, h, c: (n, 0, off(h)))
    mspec = lambda w, off: pl.BlockSpec((1, w), lambda n, h, c: (0, off(h)))
    hspec = pl.BlockSpec((1, W), lambda n, h, c: (0, h))
    assert (C_RWKV + 3 * MIX) % LORA_IN == 0
    lo_p = (C_RWKV + 3 * MIX) // LORA_IN
    lo_s = 3 * MIX // LORA_IN
    in_specs = [
        pspec(W, lambda h: rb + h), pspec(W, lambda h: rb + nb + h), pspec(W, lambda h: rb + 2 * nb + h),
        pspec(LORA_IN, lambda h: lo_p),
        sspec(W, lambda h: h), sspec(W, lambda h: nb + h), sspec(W, lambda h: 2 * nb + h),
        sspec(LORA_IN, lambda h: lo_s),
        mspec(W, lambda h: h), mspec(W, lambda h: nb + h), mspec(W, lambda h: 2 * nb + h),
        mspec(LORA_IN, lambda h: lo_s),
        hspec, hspec, hspec, hspec, hspec, hspec, hspec,
        pl.BlockSpec((DECAY_LORA, W), lambda n, h, c: (0, h)),
        pl.BlockSpec((AAA_LORA, W), lambda n, h, c: (0, h)),
        pl.BlockSpec((GATE_PAD, W), lambda n, h, c: (0, h)),
        pl.BlockSpec((1, pairs, 128, 128), lambda n, h, c: (n, h, 0, 0)),
    ]
    args = ([proj3] * 4 + [shift0] * 4 + [prm['mu']] * 4
            + [prm['w0'], prm['a0'], prm['k_k'], prm['k_a'], prm['r_k'], prm['ln_w'], prm['ln_b'],
               prm['w2'], prm['a2'], prm['g2'], s0])
    return pl.pallas_call(
        functools.partial(_rwkv_kernel, t_valid=t_valid, t_total=T),
        grid=(N, nb, T // tc),
        in_specs=in_specs,
        out_specs=[pl.BlockSpec((1, tc, W), lambda n, h, c: (n, c, M_RWKV // W + h)),
                   pl.BlockSpec((1, pairs, 128, 128), lambda n, h, c: (n, h, 0, 0))],
        out_shape=[jax.ShapeDtypeStruct((N, T, out_width), BF16),
                   jax.ShapeDtypeStruct((N, HEADS // 2, 128, 128), F32)],
        scratch_shapes=[pltpu.VMEM((pairs, 128, 128), F32), pltpu.VMEM((1, W), F32), pltpu.VMEM((1, W), F32),
                        pltpu.VMEM((1, W), F32), pltpu.VMEM((1, LORA_IN), F32)],
        compiler_params=_cparams(("parallel", "parallel", "arbitrary")),
        name="rwkv7_time_mix",
    )(*args)


def _pad_rwkv_cols(t):
    return jnp.pad(t, [(0, 0)] * (t.ndim - 1) + [(0, RWKV_PAD - RWKV_PROJ)])


PREP_BLOCK = 512


def _prep_w_in_kernel(cur_ref, prev_ref, o_ref):
    j = pl.program_id(2)
    cur = cur_ref[...]
    lane = lax.broadcasted_iota(jnp.int32, cur.shape, 1)
    col = lane + j * PREP_BLOCK
    shift = RWKV_PAD - RWKV_PROJ
    shifted = jnp.where(lane < shift, pltpu.roll(prev_ref[...], shift, 1), pltpu.roll(cur, shift, 1))
    in_tail = (col >= RWKV_PAD) & (col < C_CONV + CONV_PROJ)
    o_ref[...] = jnp.where(col < RWKV_PROJ, cur, jnp.where(in_tail, shifted, 0.0)).astype(BF16)


def prep_w_in(w_all):
    depth, K, n_src = w_all.shape
    pb = PREP_BLOCK
    last = pl.cdiv(n_src, pb) - 1
    return pl.pallas_call(
        _prep_w_in_kernel,
        grid=(depth, K // pb, NP // pb),
        in_specs=[pl.BlockSpec((None, pb, pb), lambda l, i, j: (l, i, jnp.minimum(j, last))),
                  pl.BlockSpec((None, pb, pb), lambda l, i, j: (l, i, jnp.clip(j - 1, 0, last)))],
        out_specs=pl.BlockSpec((None, pb, pb), lambda l, i, j: (l, i, j)),
        out_shape=jax.ShapeDtypeStruct((depth, K, NP), BF16),
        compiler_params=_cparams(("parallel", "parallel", "parallel")),
        name="prep_w_in",
    )(w_all, w_all)


def _matmul_weights(w_in, w_out, w_up, w_down):
    return {'w_in': prep_w_in(w_in), 'w_out': w_out.astype(BF16), 'w_up': w_up.astype(BF16), 'w_down': w_down.astype(BF16)}


def _layer_params(l, big, norm_mix, rwkv_mu, rwkv_w0, rwkv_w2, rwkv_a0, rwkv_a2, rwkv_g2, rwkv_k_k, rwkv_k_a,
                  rwkv_r_k, rwkv_ln_w, rwkv_ln_b, conv_w, conv_b, conv_ln_w, conv_ln_b, norm_ffn,
                  ffn_conv_w, ffn_conv_b):
    row = lambda t: t.reshape(1, -1)
    return {
        'w_in': big['w_in'], 'w_out': big['w_out'], 'w_up': big['w_up'], 'w_down': big['w_down'],
        'norm_mix': row(norm_mix[l]),
        'mu': _pad_rwkv_cols(row(rwkv_mu[l])),
        'w0': row(rwkv_w0[l]), 'a0': row(rwkv_a0[l]), 'k_k': row(rwkv_k_k[l]), 'k_a': row(rwkv_k_a[l]),
        'r_k': row(rwkv_r_k[l]), 'ln_w': row(rwkv_ln_w[l]), 'ln_b': row(rwkv_ln_b[l]),
        'w2': rwkv_w2[l].astype(BF16), 'a2': rwkv_a2[l].astype(BF16),
        'g2': jnp.pad(rwkv_g2[l], ((0, GATE_PAD - GATE_LORA), (0, 0))).astype(BF16),
        'conv_w': jnp.pad(conv_w[l], ((0, 32 - CONV_WIDTH), (0, 0))), 'conv_b': row(conv_b[l]),
        'conv_ln_w': row(conv_ln_w[l]), 'conv_ln_b': row(conv_ln_b[l]),
        'norm_ffn': row(norm_ffn[l]),
        'ffn_conv_w': jnp.pad(ffn_conv_w[l], ((0, 8 - FFN_CONV_WIDTH), (0, 0))), 'ffn_conv_b': row(ffn_conv_b[l]),
    }


def _rope_tables(pos):
    half = HEAD_DIM // 2
    inv_freq = ROPE_THETA ** (-jnp.arange(half, dtype=F32) * 2.0 / HEAD_DIM)
    ang = pos.astype(F32)[..., None] * inv_freq
    cos, sin = jnp.cos(ang), jnp.sin(ang)
    return jnp.concatenate([cos, cos, cos, cos], axis=-1), jnp.concatenate([-sin, sin, -sin, sin], axis=-1)


def _state_to_blockdiag(s):
    N = s.shape[0]
    s = s.reshape(N, HEADS // 2, 2, HEAD_DIM, HEAD_DIM)
    z = jnp.zeros_like(s[:, :, 0])
    top = jnp.concatenate([s[:, :, 0], z], axis=-1)
    bot = jnp.concatenate([z, s[:, :, 1]], axis=-1)
    return jnp.concatenate([top, bot], axis=-2)


def _blockdiag_to_state(s):
    N = s.shape[0]
    return jnp.stack([s[:, :, :HEAD_DIM, :HEAD_DIM], s[:, :, HEAD_DIM:, HEAD_DIM:]], axis=2).reshape(
        N, HEADS, HEAD_DIM, HEAD_DIM)


def _last_rows(buf, u, t_valid):
    keep = buf.shape[1]
    if t_valid >= keep:
        return u[:, t_valid - keep:t_valid]
    return jnp.concatenate([buf[:, t_valid:], u[:, :t_valid]], axis=1)


def _front_pad(buf, rows):
    return jnp.pad(buf, ((0, 0), (rows - buf.shape[1], 0), (0, 0)))


def _prompt_layer(x2, N, S, prm, cos_t, sin_t, layer, depth, kv_all):
    f32 = x2.dtype
    proj3 = norm_matmul(x2, prm['norm_mix'], prm['w_in'], layer, 1024, 512, single_buffer_x=True).reshape(N, S, NP)

    mixed, s_new = rwkv_time_mix(proj3, jnp.zeros((N, 1, RWKV_PAD), f32), jnp.zeros((N, HEADS // 2, 128, 128), f32),
                                 prm, S, 256, out_width=D_MODEL, pairs=4)
    shift_new = proj3[:, S - 1:S, C_RWKV:C_RWKV + RWKV_PROJ]

    mixed, k_all, v_all = prompt_attention(proj3, cos_t, sin_t, mixed, layer, depth, kv_all)

    u = glu(proj3, 256)
    mixed = dwconv_ln_silu(u, jnp.zeros((N, CONV_HALO, CONV_DIM), f32), prm['conv_w'], prm['conv_b'],
                           prm['conv_ln_w'], prm['conv_ln_b'], 128, mixed=mixed)
    conv_new = u[:, S - (CONV_WIDTH - 1):]

    x2 = matmul_res(mixed.reshape(N * S, D_MODEL), prm['w_out'], layer, x2, 512, 512, D_MODEL)

    x2, tails = ffn_fused(x2, prm['norm_ffn'], prm['w_up'], prm['w_down'], layer, prm['ffn_conv_w'], prm['ffn_conv_b'],
                          jnp.zeros((N, FFN_HALO, 2 * D_FF), f32), S, 512, 256)
    keep = FFN_CONV_WIDTH - 1
    ffn_new = jnp.transpose(tails[:, :, :, FFN_HALO - keep:, :], (0, 3, 2, 1, 4)).reshape(N, keep, 2 * D_FF)
    return x2, (shift_new, _blockdiag_to_state(s_new), conv_new, ffn_new), (k_all, v_all)


def _sample_layer(x2, N, T, t_valid, prm, cos_t, sin_t, carry, attn_bufs, layer, depth, kv_all):
    shift0, wkv0, conv_buf, ffn_buf = carry
    proj3 = norm_matmul(x2, prm['norm_mix'], prm['w_in'], layer, 512, 512).reshape(N, T, NP)

    proj_r = jnp.pad(proj3, ((0, 0), (0, CHUNK - T), (0, 0)))
    o_rwkv, s_new = rwkv_time_mix(proj_r, _pad_rwkv_cols(shift0), _state_to_blockdiag(wkv0), prm, t_valid, CHUNK)
    o_rwkv = o_rwkv[:, :T]
    shift_new = proj3[:, t_valid - 1:t_valid, C_RWKV:C_RWKV + RWKV_PROJ]

    q_rot, k_rot = rope(proj3, cos_t, sin_t, 256)
    v_new = proj3[:, :, C_ATT + 2 * MIX:C_ATT + 3 * MIX]
    padr = ((0, 0), (0, 128 - T), (0, 0))
    o_att, k_all, v_all = sample_attention(q_rot, jnp.pad(k_rot, padr), jnp.pad(v_new, padr),
                                           attn_bufs[0], attn_bufs[1], t_valid, layer, depth, kv_all)

    u = glu(proj3, 256)
    o_conv = dwconv_ln_silu(u, _front_pad(conv_buf, CONV_HALO), prm['conv_w'], prm['conv_b'],
                            prm['conv_ln_w'], prm['conv_ln_b'], 128)
    conv_new = _last_rows(conv_buf, u, t_valid)

    mixed = jnp.concatenate([o_rwkv, o_att, o_conv], axis=-1).reshape(N * T, D_MODEL)
    x2 = matmul_res(mixed, prm['w_out'], layer, x2, 512, 512, D_MODEL)

    uf3 = norm_matmul(x2, prm['norm_ffn'], prm['w_up'], layer, 512, 512).reshape(N, T, 2 * D_FF)
    act = ffn_act(uf3, _front_pad(ffn_buf, FFN_HALO), prm['ffn_conv_w'], prm['ffn_conv_b'], 512, D_FF // 2)
    ffn_new = _last_rows(ffn_buf, uf3, t_valid)
    x2 = matmul_res(act.reshape(N * T, D_FF), prm['w_down'], layer, x2, 512, 512, D_FF // 2)
    return x2, (shift_new, _blockdiag_to_state(s_new), conv_new, ffn_new), (k_all, v_all)


def kernel(x_prompt, x_sample, state_rwkv_shift, state_rwkv_wkv, state_attn_k, state_attn_v, state_conv, state_ffn_conv, pos_sample, norm_mix, w_in, rwkv_mu, rwkv_w0, rwkv_w2, rwkv_a0, rwkv_a2, rwkv_g2, rwkv_k_k, rwkv_k_a, rwkv_r_k, rwkv_ln_w, rwkv_ln_b, conv_w, conv_b, conv_ln_w, conv_ln_b, w_out, norm_ffn, w_up, ffn_conv_w, ffn_conv_b, w_down, norm_final):
    B, S, _ = x_prompt.shape
    NB, TS, _ = x_sample.shape
    depth = w_in.shape[0]
    win_buf = state_attn_k.shape[2]
    assert win_buf == S, "prompt key/value state is the whole rotated sequence"
    TSP = 8

    xp = x_prompt.reshape(B * S, D_MODEL)
    xs = jnp.pad(x_sample, ((0, 0), (0, TSP - TS), (0, 0))).reshape(NB * TSP, D_MODEL)
    cos_p, sin_p = _rope_tables(jnp.arange(S, dtype=jnp.int32)[None])
    cos_s, sin_s = _rope_tables(jnp.pad(pos_sample, ((0, 0), (0, TSP - TS))))
    new_p, new_s = [], []
    kv_p, kv_s = None, None
    big = _matmul_weights(w_in, w_out, w_up, w_down)
    bufs = (state_attn_k.reshape(depth, NB, win_buf, MIX), state_attn_v.reshape(depth, NB, win_buf, MIX))
    for l in range(depth):
        prm = _layer_params(l, big, norm_mix, rwkv_mu, rwkv_w0, rwkv_w2, rwkv_a0, rwkv_a2, rwkv_g2, rwkv_k_k,
                            rwkv_k_a, rwkv_r_k, rwkv_ln_w, rwkv_ln_b, conv_w, conv_b, conv_ln_w, conv_ln_b,
                            norm_ffn, ffn_conv_w, ffn_conv_b)
        xp, st_p, kv_p = _prompt_layer(xp, B, S, prm, cos_p, sin_p, l, depth, kv_p)
        carry_s = (state_rwkv_shift[l], state_rwkv_wkv[l], state_conv[l], state_ffn_conv[l])
        xs, st_s, kv_s = _sample_layer(xs, NB, TSP, TS, prm, cos_s, sin_s, carry_s, bufs, l, depth, kv_s)
        new_p.append(st_p)
        new_s.append(st_s)

    g = norm_final.reshape(1, D_MODEL)
    y_prompt = rmsnorm(xp, g, 512).reshape(B, S, D_MODEL)
    y_sample = rmsnorm(xs, g, 512).reshape(NB, TSP, D_MODEL)[:, :TS]

    def stack(states, i):
        return jnp.stack([st[i] for st in states], axis=0)

    heads_p = (depth, B, win_buf, HEADS, HEAD_DIM)
    heads_s = (depth, NB, win_buf, HEADS, HEAD_DIM)
    return (y_prompt, y_sample,
            stack(new_p, 0), stack(new_s, 0), stack(new_p, 1), stack(new_s, 1),
            kv_p[0].reshape(heads_p), kv_s[0].reshape(heads_s), kv_p[1].reshape(heads_p), kv_s[1].reshape(heads_s),
            stack(new_p, 2), stack(new_s, 2), stack(new_p, 3), stack(new_s, 3))
```

```python
import functools

import jax
import jax.numpy as jnp
from jax import lax
from jax.experimental import pallas as pl
from jax.experimental.pallas import tpu as pltpu

F32 = jnp.float32
BF16 = jnp.bfloat16

D_MODEL = 4096
HEAD_DIM = 64
HEADS = 24
MIX = HEADS * HEAD_DIM
CONV_DIM = 1024
DECAY_LORA = 128
AAA_LORA = 128
GATE_LORA = 480
GATE_PAD = 512
RWKV_PROJ = 3 * MIX + DECAY_LORA + AAA_LORA + GATE_LORA
RWKV_PAD = 3 * MIX + DECAY_LORA + AAA_LORA + GATE_PAD
ATT_PROJ = 3 * MIX
CONV_PROJ = 2 * CONV_DIM
CONV_WIDTH = 31
CONV_HALO = 32
D_FF = 11008
FFN_CONV_WIDTH = 3
FFN_HALO = 8
BAND = 128
DILATIONS = (1, 4, 16)
WINDOWS = (128, 512, 2048)
ROPE_THETA = 10000.0
RMS_EPS = 1e-6
LN_EPS = 1e-5
GN_EPS = 64e-5
NEG_INF = -1e30
CHUNK = 64

C_RWKV = 0
C_ATT = RWKV_PAD
C_CONV = C_ATT + ATT_PROJ
NP = 12288
LORA_IN = DECAY_LORA + AAA_LORA + GATE_PAD
M_RWKV, M_ATT, M_CONV = 0, MIX, 2 * MIX

VMEM_LIMIT = 56 * 1024 * 1024

NT = (((1,), (1,)), ((), ()))
TN = (((0,), (0,)), ((), ()))


def _cparams(sem):
    return pltpu.CompilerParams(dimension_semantics=sem, vmem_limit_bytes=VMEM_LIMIT)


def _sigmoid(x):
    return 1.0 / (1.0 + jnp.exp(-x))


def _head0_mask(shape):
    return lax.broadcasted_iota(jnp.int32, shape, 1) % (2 * HEAD_DIM) < HEAD_DIM


def _norm_matmul_kernel(x_ref, g_ref, w_ref, o_ref, h_ref):
    @pl.when(pl.program_id(1) == 0)
    def _():
        x = x_ref[...]
        ms = jnp.mean(x * x, axis=-1, keepdims=True)
        h_ref[...] = (x * lax.rsqrt(ms + RMS_EPS) * g_ref[...]).astype(BF16)

    o_ref[...] = jnp.dot(h_ref[...], w_ref[...], preferred_element_type=F32)


def norm_matmul(x, g, w, layer, tm, tn, single_buffer_x=False):
    M, K = x.shape
    N = w.shape[2]
    tm = min(tm, M)
    assert M % tm == 0 and N % tn == 0
    x_mode = dict(pipeline_mode=pl.Buffered(1)) if single_buffer_x else {}
    return pl.pallas_call(
        _norm_matmul_kernel,
        grid=(M // tm, N // tn),
        in_specs=[pl.BlockSpec((tm, K), lambda i, j: (i, 0), **x_mode),
                  pl.BlockSpec((1, K), lambda i, j: (0, 0)),
                  pl.BlockSpec((None, K, tn), lambda i, j: (layer, 0, j))],
        out_specs=pl.BlockSpec((tm, tn), lambda i, j: (i, j)),
        out_shape=jax.ShapeDtypeStruct((M, N), F32),
        scratch_shapes=[pltpu.VMEM((tm, K), BF16)],
        compiler_params=_cparams(("parallel", "arbitrary")),
        name="norm_matmul",
    )(x, g, w)


def _matmul_res_kernel(a_ref, w_ref, r_ref, o_ref, acc_ref, *, nk):
    k = pl.program_id(2)

    @pl.when(k == 0)
    def _():
        acc_ref[...] = r_ref[...]

    acc_ref[...] += jnp.dot(a_ref[...], w_ref[...], preferred_element_type=F32)

    @pl.when(k == nk - 1)
    def _():
        o_ref[...] = acc_ref[...]


def matmul_res(a, w, layer, res, tm, tn, tk):
    M, K = a.shape
    N = w.shape[2]
    tm = min(tm, M)
    assert M % tm == 0 and N % tn == 0 and K % tk == 0
    nk = K // tk
    return pl.pallas_call(
        functools.partial(_matmul_res_kernel, nk=nk),
        grid=(M // tm, N // tn, nk),
        in_specs=[pl.BlockSpec((tm, tk), lambda i, j, k: (i, k)),
                  pl.BlockSpec((None, tk, tn), lambda i, j, k: (layer, k, j)),
                  pl.BlockSpec((tm, tn), lambda i, j, k: (i, j))],
        out_specs=pl.BlockSpec((tm, tn), lambda i, j, k: (i, j)),
        out_shape=jax.ShapeDtypeStruct((M, N), F32),
        scratch_shapes=[pltpu.VMEM((tm, tn), F32)],
        compiler_params=_cparams(("parallel", "parallel", "arbitrary")),
        name="matmul_res",
    )(a, w, res)


def _rmsnorm_kernel(x_ref, g_ref, o_ref):
    x = x_ref[...]
    ms = jnp.mean(x * x, axis=-1, keepdims=True)
    o_ref[...] = x * lax.rsqrt(ms + RMS_EPS) * g_ref[...]


def rmsnorm(x, g, tm):
    M, K = x.shape
    tm = min(tm, M)
    return pl.pallas_call(
        _rmsnorm_kernel,
        grid=(M // tm,),
        in_specs=[pl.BlockSpec((tm, K), lambda i: (i, 0)), pl.BlockSpec((1, K), lambda i: (0, 0))],
        out_specs=pl.BlockSpec((tm, K), lambda i: (i, 0)),
        out_shape=jax.ShapeDtypeStruct((M, K), F32),
        compiler_params=_cparams(("parallel",)),
        name="rmsnorm",
    )(x, g)


def _rot_half(x, first_half):
    w = x.shape[1]
    return jnp.where(first_half, pltpu.roll(x, w - HEAD_DIM // 2, 1), pltpu.roll(x, HEAD_DIM // 2, 1))


ROPE_BLOCK = 256


def _rope_kernel(q_ref, k_ref, cos_ref, sin_ref, qo_ref, ko_ref):
    reps = ROPE_BLOCK // (2 * HEAD_DIM)
    cos = jnp.concatenate([cos_ref[0]] * reps, axis=1)
    sin = jnp.concatenate([sin_ref[0]] * reps, axis=1)
    lane = lax.broadcasted_iota(jnp.int32, cos.shape, 1)
    first_half = (lane % HEAD_DIM) < (HEAD_DIM // 2)
    q = q_ref[0]
    k = k_ref[0]
    qo_ref[0] = q * cos + _rot_half(q, first_half) * sin
    ko_ref[0] = k * cos + _rot_half(k, first_half) * sin


def rope(proj3, cos_t, sin_t, tt):
    N, T, _ = proj3.shape
    tt = min(tt, T)
    tab = lambda n, i, j: (n, i, 0)
    rw = ROPE_BLOCK
    assert C_ATT % rw == 0 and MIX % rw == 0
    qb = C_ATT // rw
    kb = (C_ATT + MIX) // rw
    out = jax.ShapeDtypeStruct((N, T, MIX), F32)
    return pl.pallas_call(
        _rope_kernel,
        grid=(N, T // tt, MIX // rw),
        in_specs=[pl.BlockSpec((1, tt, rw), lambda n, i, j: (n, i, qb + j)),
                  pl.BlockSpec((1, tt, rw), lambda n, i, j: (n, i, kb + j)),
                  pl.BlockSpec((1, tt, 128), tab),
                  pl.BlockSpec((1, tt, 128), tab)],
        out_specs=[pl.BlockSpec((1, tt, rw), lambda n, i, j: (n, i, j)),
                   pl.BlockSpec((1, tt, rw), lambda n, i, j: (n, i, j))],
        out_shape=[out, out],
        compiler_params=_cparams(("parallel", "parallel", "parallel")),
        name="rope",
    )(proj3, proj3, cos_t, sin_t)


def _prompt_attn_kernel(*refs):
    q_ref, k_ref, v_ref, cos_ref, sin_ref = refs[:5]
    o_ref, ko_all_ref, vo_all_ref, qs_ref, acc_ref, m_ref, l_ref = refs[-7:]
    ko_ref = ko_all_ref.at[0]
    vo_ref = vo_all_ref.at[0]
    S = q_ref.shape[1]
    B = BAND
    P = 2 * HEAD_DIM
    lane = lax.broadcasted_iota(jnp.int32, (B, P), 1)
    first_half = (lane % HEAD_DIM) < (HEAD_DIM // 2)
    h0 = _head0_mask((B, P))
    scale = HEAD_DIM ** -0.5

    def rope_rows(i, carry):
        rows = pl.ds(pl.multiple_of(i * B, B), B)
        cos = cos_ref[0, rows, :]
        sin = sin_ref[0, rows, :]
        q = q_ref[0, rows, :]
        k = k_ref[0, rows, :]
        qs_ref[rows, :] = q * cos + _rot_half(q, first_half) * sin
        ko_ref[0, rows, :] = k * cos + _rot_half(k, first_half) * sin
        vo_ref[0, rows, :] = v_ref[0, rows, :]
        return carry

    lax.fori_loop(0, S // B, rope_rows, 0)

    def attend(q, kk, vv, valid):
        ms, ls, os_ = [], [], []
        for h in range(2):
            qh = jnp.where(h0 if h == 0 else ~h0, q, 0.0).astype(BF16)
            s = lax.dot_general(qh, kk, NT, preferred_element_type=F32) * scale
            s = jnp.where(valid, s, NEG_INF)
            m = jnp.max(s, axis=-1, keepdims=True)
            e = jnp.exp(s - m)
            ms.append(m)
            ls.append(jnp.sum(e, axis=-1, keepdims=True))
            os_.append(jnp.dot(e.astype(BF16), vv, preferred_element_type=F32))
        return jnp.where(h0, ms[0], ms[1]), jnp.where(h0, ls[0], ls[1]), jnp.where(h0, os_[0], os_[1])

    def merge(rows, m_c, l_c, o_c):
        m_p = m_ref[rows, :]
        m_n = jnp.maximum(m_p, m_c)
        a_p = jnp.exp(m_p - m_n)
        a_c = jnp.exp(m_c - m_n)
        acc_ref[rows, :] = acc_ref[rows, :] * a_p + o_c * a_c
        l_ref[rows, :] = l_ref[rows, :] * a_p + l_c * a_c
        m_ref[rows, :] = m_n

    qi2 = lax.broadcasted_iota(jnp.int32, (B, 2 * B), 0)
    ki2 = lax.broadcasted_iota(jnp.int32, (B, 2 * B), 1)
    dist2 = qi2 + B - ki2
    band2 = (dist2 >= 0) & (dist2 <= B)
    qi1 = lax.broadcasted_iota(jnp.int32, (B, B), 0)
    ki1 = lax.broadcasted_iota(jnp.int32, (B, B), 1)
    causal1 = qi1 >= ki1

    def two_block_keys(cur, prev):
        kk = jnp.concatenate([ko_ref[0, prev, :], ko_ref[0, cur, :]], axis=0).astype(BF16)
        vv = jnp.concatenate([v_ref[0, prev, :], v_ref[0, cur, :]], axis=0).astype(BF16)
        return kk, vv

    d16 = DILATIONS[2]
    assert S == B * d16
    per_iter = 4

    def dil16(it, carry):
        for u in range(per_iter):
            cur = pl.ds(it * per_iter + u, B, stride=d16)
            m_c, l_c, o_c = attend(qs_ref[cur, :], ko_ref[0, cur, :].astype(BF16), v_ref[0, cur, :].astype(BF16), causal1)
            acc_ref[cur, :] = o_c
            m_ref[cur, :] = m_c
            l_ref[cur, :] = l_c
        return carry

    lax.fori_loop(0, d16 // per_iter, dil16, 0)

    d4 = DILATIONS[1]

    def dil4(ib, carry):
        base = pl.multiple_of(ib * (B * d4), B * d4)
        pbase = pl.multiple_of(jnp.maximum(ib - 1, 0) * (B * d4), B * d4)
        for r in range(d4):
            cur = pl.ds(base + r, B, stride=d4)
            prev = pl.ds(pbase + r, B, stride=d4)
            kk, vv = two_block_keys(cur, prev)
            m_c, l_c, o_c = attend(qs_ref[cur, :], kk, vv, band2 & ((ki2 >= B) | (ib > 0)))
            merge(cur, m_c, l_c, o_c)
        return carry

    lax.fori_loop(0, S // (B * d4), dil4, 0)

    unroll = 4

    def dil1(it, carry):
        for u in range(unroll):
            ib = it * unroll + u
            cur = pl.ds(pl.multiple_of(ib * B, B), B)
            prev = pl.ds(pl.multiple_of(jnp.maximum(ib - 1, 0) * B, B), B)
            kk, vv = two_block_keys(cur, prev)
            m_c, l_c, o_c = attend(qs_ref[cur, :], kk, vv, band2 & ((ki2 >= B) | (ib > 0)))
            m_p = m_ref[cur, :]
            m_n = jnp.maximum(m_p, m_c)
            a_p = jnp.exp(m_p - m_n)
            a_c = jnp.exp(m_c - m_n)
            acc = acc_ref[cur, :] * a_p + o_c * a_c
            l_n = l_ref[cur, :] * a_p + l_c * a_c
            o_ref[0, cur, :] = (acc / l_n).astype(BF16)
        return carry

    lax.fori_loop(0, S // (B * unroll), dil1, 0)


def prompt_attention(proj3, cos_t, sin_t, mixed, layer, depth, kv_all=None):
    N, S, _ = proj3.shape
    qb, kb, vb = C_ATT // 128, (C_ATT + MIX) // 128, (C_ATT + 2 * MIX) // 128
    col = lambda b: pl.BlockSpec((1, S, 128), lambda n, h: (n, 0, b + h))
    tab = pl.BlockSpec((1, S, 128), lambda n, h: (0, 0, 0))
    anyspec = pl.BlockSpec(memory_space=pl.ANY)
    kv = jax.ShapeDtypeStruct((depth, N, S, MIX), F32)
    kv_spec = pl.BlockSpec((1, 1, S, 128), lambda n, h: (layer, n, 0, h))
    args = [proj3, proj3, proj3, cos_t, sin_t, mixed]
    in_specs = [col(qb), col(kb), col(vb), tab, tab, anyspec]
    aliases = {5: 0}
    if kv_all is not None:
        args += list(kv_all)
        in_specs += [anyspec, anyspec]
        aliases.update({6: 1, 7: 2})
    return pl.pallas_call(
        _prompt_attn_kernel,
        grid=(N, HEADS // 2),
        in_specs=in_specs,
        out_specs=[col(M_ATT // 128), kv_spec, kv_spec],
        out_shape=[jax.ShapeDtypeStruct(mixed.shape, mixed.dtype), kv, kv],
        scratch_shapes=[pltpu.VMEM((S, 128), F32)] * 4,
        input_output_aliases=aliases,
        compiler_params=_cparams(("parallel", "parallel")),
        name="prompt_attention",
    )(*args)


def _sample_attn_kernel(*refs, t_valid):
    q_ref, kn_ref, vn_ref, kb_ref, vb_ref = refs[:5]
    o_ref, ks_all_ref, vs_all_ref = refs[-3:]
    ks_ref = ks_all_ref.at[0]
    vs_ref = vs_all_ref.at[0]
    TQ = q_ref.shape[1]
    M = kb_ref.shape[1]
    TN_ = kn_ref.shape[1]
    h0 = _head0_mask((TQ, 2 * HEAD_DIM))
    scale = HEAD_DIM ** -0.5

    def counts(delta, in_range):
        c = jnp.zeros(delta.shape, F32)
        for win, dil in zip(WINDOWS, DILATIONS):
            c = c + jnp.where((delta % dil == 0) & (delta <= win) & in_range, 1.0, 0.0)
        return c

    tq = lax.broadcasted_iota(jnp.int32, (TQ, M), 0)
    kb_i = lax.broadcasted_iota(jnp.int32, (TQ, M), 1)
    d_buf = M + tq - kb_i
    c_buf = counts(d_buf, d_buf >= 0)
    tq2 = lax.broadcasted_iota(jnp.int32, (TQ, TN_), 0)
    tn2 = lax.broadcasted_iota(jnp.int32, (TQ, TN_), 1)
    d_new = tq2 - tn2
    c_new = counts(d_new, (d_new >= 0) & (tn2 < t_valid))

    row8 = lax.broadcasted_iota(jnp.int32, (8, 2 * HEAD_DIM), 0)
    for p in range(4):
        sl = slice(128 * p, 128 * (p + 1))
        qp = q_ref[0, :, sl]
        kb = kb_ref[0, :, sl]
        vb = vb_ref[0, :, sl]
        kn = kn_ref[0, :, sl]
        vn = vn_ref[0, :, sl]
        kb16, vb16, kn16, vn16 = kb.astype(BF16), vb.astype(BF16), kn.astype(BF16), vn.astype(BF16)
        outs = []
        for h in range(2):
            qh = jnp.where(h0 if h == 0 else ~h0, qp, 0.0).astype(BF16)
            s_b = lax.dot_general(qh, kb16, NT, preferred_element_type=F32) * scale
            s_n = lax.dot_general(qh, kn16, NT, preferred_element_type=F32) * scale
            s_b = jnp.where(c_buf > 0, s_b, NEG_INF)
            s_n = jnp.where(c_new > 0, s_n, NEG_INF)
            m = jnp.maximum(jnp.max(s_b, axis=-1, keepdims=True), jnp.max(s_n, axis=-1, keepdims=True))
            w_b = c_buf * jnp.exp(s_b - m)
            w_n = c_new * jnp.exp(s_n - m)
            l = jnp.sum(w_b, axis=-1, keepdims=True) + jnp.sum(w_n, axis=-1, keepdims=True)
            o = (jnp.dot(w_b.astype(BF16), vb16, preferred_element_type=F32)
                 + jnp.dot(w_n.astype(BF16), vn16, preferred_element_type=F32))
            outs.append(o / l)
        o_ref[0, :, sl] = jnp.where(h0, outs[0], outs[1]).astype(BF16)

        for src, new, dst in ((kb, kn, ks_ref), (vb, vn, vs_ref)):
            rolled = pltpu.roll(src, M - t_valid, 0)
            new_r = pltpu.roll(new[0:8], 8 - t_valid, 0)
            dst[0, 0:M - 8, sl] = rolled[0:M - 8]
            dst[0, M - 8:M, sl] = jnp.where(row8 >= 8 - t_valid, new_r, rolled[M - 8:M])


def sample_attention(q_rot, k_new, v_new, k_buf, v_buf, t_valid, layer, depth, kv_all=None):
    N, TQ, _ = q_rot.shape
    M = k_buf.shape[2]
    TN_ = k_new.shape[1]
    assert t_valid <= 8 <= TQ
    blk = lambda r: pl.BlockSpec((1, r, 512), lambda n, j: (n, 0, j))
    buf = pl.BlockSpec((None, 1, M, 512), lambda n, j: (layer, n, 0, j))
    kv = jax.ShapeDtypeStruct((depth, N, M, MIX), F32)
    kv_spec = pl.BlockSpec((1, 1, M, 512), lambda n, j: (layer, n, 0, j))
    args = [q_rot, k_new, v_new, k_buf, v_buf]
    in_specs = [blk(TQ), blk(TN_), blk(TN_), buf, buf]
    aliases = {}
    if kv_all is not None:
        args += list(kv_all)
        in_specs += [pl.BlockSpec(memory_space=pl.ANY)] * 2
        aliases = {5: 1, 6: 2}
    return pl.pallas_call(
        functools.partial(_sample_attn_kernel, t_valid=t_valid),
        grid=(N, MIX // 512),
        in_specs=in_specs,
        out_specs=[blk(TQ), kv_spec, kv_spec],
        out_shape=[jax.ShapeDtypeStruct((N, TQ, MIX), BF16), kv, kv],
        input_output_aliases=aliases,
        compiler_params=_cparams(("parallel", "parallel")),
        name="sample_attention",
    )(*args)


def _glu_kernel(a_ref, g_ref, o_ref):
    o_ref[0] = a_ref[0] * _sigmoid(g_ref[0])


def glu(proj3, tt):
    N, T, _ = proj3.shape
    tt = min(tt, T)
    gw = 256
    assert C_CONV % gw == 0 and CONV_DIM % gw == 0
    ab = C_CONV // gw
    gb = (C_CONV + CONV_DIM) // gw
    return pl.pallas_call(
        _glu_kernel,
        grid=(N, T // tt, CONV_DIM // gw),
        in_specs=[pl.BlockSpec((1, tt, gw), lambda n, i, j: (n, i, ab + j)),
                  pl.BlockSpec((1, tt, gw), lambda n, i, j: (n, i, gb + j))],
        out_specs=pl.BlockSpec((1, tt, gw), lambda n, i, j: (n, i, j)),
        out_shape=jax.ShapeDtypeStruct((N, T, CONV_DIM), F32),
        compiler_params=_cparams(("parallel", "parallel", "parallel")),
        name="glu",
    )(proj3, proj3)


def _dwconv_kernel(u_ref, uh_ref, h0_ref, w_ref, b_ref, lw_ref, lb_ref, *rest, single_tile):
    o_ref, ext_ref, y_ref = rest[-3:]
    tt = u_ref.shape[1]
    i = pl.program_id(1)

    @pl.when(i == 0)
    def _():
        ext_ref[0:CONV_HALO, :] = h0_ref[0]

    if not single_tile:
        @pl.when(i > 0)
        def _():
            ext_ref[0:CONV_HALO, :] = uh_ref[0]

    ext_ref[CONV_HALO:, :] = u_ref[0]
    lead = CONV_HALO - (CONV_WIDTH - 1)
    rows = min(16, tt)
    for r0 in range(0, tt, rows):
        for c0 in range(0, CONV_DIM, 512):
            acc = jnp.broadcast_to(b_ref[:, c0:c0 + 512], (rows, 512))
            for j in range(CONV_WIDTH):
                acc = acc + ext_ref[r0 + lead + j:r0 + lead + j + rows, c0:c0 + 512] * w_ref[j:j + 1, c0:c0 + 512]
            y_ref[r0:r0 + rows, c0:c0 + 512] = acc
    y = y_ref[...]
    mu = jnp.mean(y, axis=-1, keepdims=True)
    var = jnp.mean(jnp.square(y - mu), axis=-1, keepdims=True)
    c = (y - mu) * lax.rsqrt(var + LN_EPS) * lw_ref[...] + lb_ref[...]
    o_ref[0] = (c * _sigmoid(c)).astype(BF16)


def dwconv_ln_silu(u, halo0, w, b, lw, lb, tt, mixed=None):
    N, T, C = u.shape
    tt = min(tt, T)
    hb = tt // CONV_HALO if tt >= CONV_HALO else 1
    in_specs = [pl.BlockSpec((1, tt, C), lambda n, i: (n, i, 0)),
                pl.BlockSpec((1, min(CONV_HALO, T), C), lambda n, i: (n, jnp.maximum(i * hb - 1, 0), 0)),
                pl.BlockSpec((1, CONV_HALO, C), lambda n, i: (n, 0, 0)),
                pl.BlockSpec((32, C), lambda n, i: (0, 0)),
                pl.BlockSpec((1, C), lambda n, i: (0, 0)),
                pl.BlockSpec((1, C), lambda n, i: (0, 0)),
                pl.BlockSpec((1, C), lambda n, i: (0, 0))]
    args = [u, u, halo0, w, b, lw, lb]
    if mixed is None:
        out_shape = jax.ShapeDtypeStruct((N, T, C), BF16)
        out_spec = pl.BlockSpec((1, tt, C), lambda n, i: (n, i, 0))
        aliases = {}
    else:
        in_specs.append(pl.BlockSpec(memory_space=pl.ANY))
        args.append(mixed)
        out_shape = jax.ShapeDtypeStruct(mixed.shape, mixed.dtype)
        out_spec = pl.BlockSpec((1, tt, C), lambda n, i: (n, i, M_CONV // C))
        aliases = {7: 0}
    return pl.pallas_call(
        functools.partial(_dwconv_kernel, single_tile=(T == tt)),
        grid=(N, T // tt),
        in_specs=in_specs,
        out_specs=out_spec,
        out_shape=out_shape,
        scratch_shapes=[pltpu.VMEM((CONV_HALO + tt, C), F32), pltpu.VMEM((tt, C), F32)],
        input_output_aliases=aliases,
        compiler_params=_cparams(("parallel", "arbitrary")),
        name="dwconv_ln_silu",
    )(*args)


def _ffn_conv_act(ext_ref, wg_ref, wv_ref, bg_ref, bv_ref, store, tt):
    lead = FFN_HALO - (FFN_CONV_WIDTH - 1)
    rows = min(32, tt)
    for r0 in range(0, tt, rows):
        ys = []
        for s, w_ref, b_ref in ((0, wg_ref, bg_ref), (1, wv_ref, bv_ref)):
            acc = jnp.broadcast_to(b_ref[...], (rows, b_ref.shape[1]))
            for j in range(FFN_CONV_WIDTH):
                acc = acc + ext_ref[s, r0 + lead + j:r0 + lead + j + rows, :] * w_ref[j:j + 1, :]
            ys.append(acc)
        gate, val = ys
        store(r0, rows, (gate * _sigmoid(gate) * val).astype(BF16))


def _ffn_act_kernel(ug_ref, uv_ref, hg_ref, hv_ref, h0g_ref, h0v_ref, wg_ref, wv_ref, bg_ref, bv_ref, o_ref, ext_ref):
    tt = ug_ref.shape[1]
    i = pl.program_id(1)

    @pl.when(i == 0)
    def _():
        ext_ref[0, 0:FFN_HALO, :] = h0g_ref[0]
        ext_ref[1, 0:FFN_HALO, :] = h0v_ref[0]

    @pl.when(i > 0)
    def _():
        ext_ref[0, 0:FFN_HALO, :] = hg_ref[0]
        ext_ref[1, 0:FFN_HALO, :] = hv_ref[0]

    ext_ref[0, FFN_HALO:, :] = ug_ref[0]
    ext_ref[1, FFN_HALO:, :] = uv_ref[0]

    def store(r0, rows, val):
        o_ref[0, r0:r0 + rows, :] = val

    _ffn_conv_act(ext_ref, wg_ref, wv_ref, bg_ref, bv_ref, store, tt)


def ffn_act(u3, halo0, w, b, tt, tf):
    N, T, F2 = u3.shape
    F = F2 // 2
    tt = min(tt, T)
    nf = F // tf
    hb = tt // FFN_HALO
    g = lambda n, i, j: (n, i, j)
    v = lambda n, i, j: (n, i, j + nf)
    hg = lambda n, i, j: (n, jnp.maximum(i * hb - 1, 0), j)
    hv = lambda n, i, j: (n, jnp.maximum(i * hb - 1, 0), j + nf)
    return pl.pallas_call(
        _ffn_act_kernel,
        grid=(N, T // tt, nf),
        in_specs=[pl.BlockSpec((1, tt, tf), g), pl.BlockSpec((1, tt, tf), v),
                  pl.BlockSpec((1, FFN_HALO, tf), hg), pl.BlockSpec((1, FFN_HALO, tf), hv),
                  pl.BlockSpec((1, FFN_HALO, tf), lambda n, i, j: (n, 0, j)),
                  pl.BlockSpec((1, FFN_HALO, tf), lambda n, i, j: (n, 0, j + nf)),
                  pl.BlockSpec((8, tf), lambda n, i, j: (0, j)), pl.BlockSpec((8, tf), lambda n, i, j: (0, j + nf)),
                  pl.BlockSpec((1, tf), lambda n, i, j: (0, j)), pl.BlockSpec((1, tf), lambda n, i, j: (0, j + nf))],
        out_specs=pl.BlockSpec((1, tt, tf), g),
        out_shape=jax.ShapeDtypeStruct((N, T, F), BF16),
        scratch_shapes=[pltpu.VMEM((2, FFN_HALO + tt, tf), F32)],
        compiler_params=_cparams(("parallel", "parallel", "parallel")),
        name="ffn_act",
    )(u3, u3, u3, u3, halo0, halo0, w, w, b, b)


def _ffn_fused_kernel(x_ref, g_ref, wg_ref, wv_ref, wd_ref, cwg_ref, cwv_ref, cbg_ref, cbv_ref, h0g_ref, h0v_ref,
                      o_ref, st_ref, h_ref, ext_ref, carry_ref, act_ref, *, tiles_per_seq):
    i = pl.program_id(0)
    j = pl.program_id(1)
    tm = x_ref.shape[0]

    @pl.when(j == 0)
    def _():
        x = x_ref[...]
        ms = jnp.mean(x * x, axis=-1, keepdims=True)
        h_ref[...] = (x * lax.rsqrt(ms + RMS_EPS) * g_ref[...]).astype(BF16)
        o_ref[...] = x

    h = h_ref[...]
    ug = jnp.dot(h, wg_ref[...], preferred_element_type=F32)
    uv = jnp.dot(h, wv_ref[...], preferred_element_type=F32)
    seq_start = (i % tiles_per_seq) == 0
    ext_ref[0, 0:FFN_HALO, :] = jnp.where(seq_start, h0g_ref[0], carry_ref[j, 0])
    ext_ref[1, 0:FFN_HALO, :] = jnp.where(seq_start, h0v_ref[0], carry_ref[j, 1])
    ext_ref[0, FFN_HALO:, :] = ug
    ext_ref[1, FFN_HALO:, :] = uv
    carry_ref[j, 0] = ug[tm - FFN_HALO:, :]
    carry_ref[j, 1] = uv[tm - FFN_HALO:, :]

    def store(r0, rows, val):
        act_ref[r0:r0 + rows, :] = val

    _ffn_conv_act(ext_ref, cwg_ref, cwv_ref, cbg_ref, cbv_ref, store, tm)
    o_ref[...] += jnp.dot(act_ref[...], wd_ref[...], preferred_element_type=F32)

    @pl.when((j == pl.num_programs(1) - 1) & (i % tiles_per_seq == tiles_per_seq - 1))
    def _():
        st_ref[0] = carry_ref[...]


def ffn_fused(x, g, w_up, w_down, layer, cw, cb, halo0, seq_len, tm, tf):
    M, D = x.shape
    F = w_down.shape[1]
    assert M % tm == 0 and seq_len % tm == 0 and F % tf == 0
    nf = F // tf
    tps = seq_len // tm
    gate = lambda i, j: (0, j)
    val = lambda i, j: (0, j + nf)
    return pl.pallas_call(
        functools.partial(_ffn_fused_kernel, tiles_per_seq=tps),
        grid=(M // tm, nf),
        in_specs=[pl.BlockSpec((tm, D), lambda i, j: (i, 0), pipeline_mode=pl.Buffered(1)),
                  pl.BlockSpec((1, D), lambda i, j: (0, 0)),
                  pl.BlockSpec((None, D, tf), lambda i, j: (layer, 0, j)),
                  pl.BlockSpec((None, D, tf), lambda i, j: (layer, 0, j + nf)),
                  pl.BlockSpec((None, tf, D), lambda i, j: (layer, j, 0)),
                  pl.BlockSpec((8, tf), gate), pl.BlockSpec((8, tf), val),
                  pl.BlockSpec((1, tf), gate), pl.BlockSpec((1, tf), val),
                  pl.BlockSpec((1, FFN_HALO, tf), lambda i, j: (i // tps, 0, j)),
                  pl.BlockSpec((1, FFN_HALO, tf), lambda i, j: (i // tps, 0, j + nf))],
        out_specs=[pl.BlockSpec((tm, D), lambda i, j: (i, 0)),
                   pl.BlockSpec((1, nf, 2, FFN_HALO, tf), lambda i, j: (i // tps, 0, 0, 0, 0))],
        out_shape=[jax.ShapeDtypeStruct((M, D), F32),
                   jax.ShapeDtypeStruct((M // seq_len, nf, 2, FFN_HALO, tf), F32)],
        scratch_shapes=[pltpu.VMEM((tm, D), BF16), pltpu.VMEM((2, FFN_HALO + tm, tf), F32),
                        pltpu.VMEM((nf, 2, FFN_HALO, tf), F32), pltpu.VMEM((tm, tf), BF16)],
        compiler_params=_cparams(("arbitrary", "arbitrary")),
        name="ffn_fused",
    )(x, g, w_up, w_up, w_down, cw, cw, cb, cb, halo0, halo0)


def _rwkv_kernel(r_ref, k_ref, v_ref, lo_ref,
                 sr_ref, sk_ref, sv_ref, slo_ref,
                 mur_ref, muk_ref, muv_ref, mulo_ref,
                 w0_ref, a0_ref, kk_ref, ka_ref, rk_ref, lnw_ref, lnb_ref,
                 w2_ref, a2_ref, g2_ref, s0_ref,
                 o_ref, so_ref,
                 S_ref, pr_ref, pk_ref, pv_ref, plo_ref, *, t_valid, t_total):
    c = pl.program_id(2)
    Tc = r_ref.shape[1]
    PW = 2 * HEAD_DIM
    n_pairs = r_ref.shape[2] // PW

    @pl.when(c == 0)
    def _():
        S_ref[...] = s0_ref[0]
        pr_ref[...] = sr_ref[0]
        pk_ref[...] = sk_ref[0]
        pv_ref[...] = sv_ref[0]
        plo_ref[...] = slo_ref[0]

    def lerp(p_ref, prev_ref, mu_ref):
        p = p_ref[0]
        row = lax.broadcasted_iota(jnp.int32, p.shape, 0)
        shifted = jnp.where(row == 0, prev_ref[...], pltpu.roll(p, 1, 0))
        prev_ref[...] = p[Tc - 1:Tc, :]
        return p + (shifted - p) * mu_ref[...]

    xr = lerp(r_ref, pr_ref, mur_ref)
    xk = lerp(k_ref, pk_ref, muk_ref)
    xv = lerp(v_ref, pv_ref, muv_ref)
    xlo = lerp(lo_ref, plo_ref, mulo_ref)
    xw = xlo[:, :DECAY_LORA]
    xa = xlo[:, DECAY_LORA:DECAY_LORA + AAA_LORA]
    xg = xlo[:, DECAY_LORA + AAA_LORA:]

    h0p = _head0_mask((Tc, PW))

    def head_sum(x):
        parts = []
        for p in range(n_pairs):
            xp = x[:, PW * p:PW * (p + 1)]
            s_0 = jnp.sum(jnp.where(h0p, xp, 0.0), axis=-1, keepdims=True)
            s_1 = jnp.sum(jnp.where(h0p, 0.0, xp), axis=-1, keepdims=True)
            parts.append(jnp.where(h0p, s_0, s_1))
        return parts[0] if n_pairs == 1 else jnp.concatenate(parts, axis=1)

    def mm(x, y_):
        return jnp.dot(x, y_, preferred_element_type=F32)

    z = w0_ref[...] + mm(jnp.tanh(xw).astype(BF16), w2_ref[...])
    w_log = jnp.minimum(z, 0.0) - jnp.log(1.0 + jnp.exp(-jnp.abs(z))) - 0.5
    ld = -jnp.exp(w_log)
    a = _sigmoid(a0_ref[...] + mm(xa.astype(BF16), a2_ref[...]))
    g = mm(_sigmoid(xg).astype(BF16), g2_ref[...])

    kk = xk * kk_ref[...]
    kk = kk * lax.rsqrt(jnp.maximum(head_sum(kk * kk), 1e-24))
    kmod = xk * (1.0 + (a - 1.0) * ka_ref[...])
    avec = -kk
    bvec = kk * a
    bonus = head_sum(xr * kmod * rk_ref[...]) * xv
    vval = xv

    if t_valid < t_total:
        row = lax.broadcasted_iota(jnp.int32, (Tc, n_pairs * PW), 0) + c * Tc
        ok = row < t_valid
        ld = jnp.where(ok, ld, 0.0)
        avec = jnp.where(ok, avec, 0.0)
        bvec = jnp.where(ok, bvec, 0.0)
        kmod_s = jnp.where(ok, kmod, 0.0)
        vval = jnp.where(ok, vval, 0.0)
    else:
        kmod_s = kmod

    C = CHUNK
    tri_r = lax.broadcasted_iota(jnp.int32, (C, C), 0)
    tri_c = lax.broadcasted_iota(jnp.int32, (C, C), 1)
    tril = jnp.where(tri_r >= tri_c, 1.0, 0.0).astype(F32)
    it = lax.broadcasted_iota(jnp.int32, (2 * C, 2 * C), 0) % C
    js = lax.broadcasted_iota(jnp.int32, (2 * C, 2 * C), 1) % C
    strict = it > js
    incl = it >= js
    h0c = _head0_mask((C, 2 * HEAD_DIM))

    def stack(x):
        return jnp.concatenate([jnp.where(h0c, x, 0.0), jnp.where(h0c, 0.0, x)], axis=0)

    n_chunks = Tc // C
    grp = 2 if n_chunks % 2 == 0 else 1
    gw = 2 * C * grp
    eye_g = jnp.where(lax.broadcasted_iota(jnp.int32, (gw, gw), 0)
                      == lax.broadcasted_iota(jnp.int32, (gw, gw), 1), 1.0, 0.0).astype(F32)
    zero_blk = jnp.zeros((2 * C, 2 * C), F32)

    def bdiag(blocks):
        if len(blocks) == 1:
            return blocks[0]
        return jnp.concatenate([jnp.concatenate([blocks[0], zero_blk], axis=1),
                                jnp.concatenate([zero_blk, blocks[1]], axis=1)], axis=0)

    def cat(blocks):
        return blocks[0] if len(blocks) == 1 else jnp.concatenate(blocks, axis=0)

    a16, r32, bb16, kb16, v16, gts, Ls, AKs, RBs, RKs = ({} for _ in range(10))
    units = [(p, j) for j in range(n_chunks) for p in range(n_pairs)]
    for p, j in units:
        sl = (slice(C * j, C * (j + 1)), slice(PW * p, PW * (p + 1)))
        ldc = ld[sl]
        cs = jnp.dot(tril, ldc, precision=lax.Precision.HIGHEST, preferred_element_type=F32)
        tot = cs[C - 1:C, :]
        eg = jnp.exp(cs)
        egi = jnp.exp(-cs)
        ege = jnp.exp(cs - ldc)
        et = jnp.exp(tot - cs)
        gts[p, j] = jnp.exp(tot)
        a_s = stack(avec[sl] * ege).astype(BF16)
        r_s = stack(xr[sl] * eg)
        b_s = stack(bvec[sl] * egi).astype(BF16)
        k_s = stack(kmod_s[sl] * egi).astype(BF16)
        sc = lax.dot_general(jnp.concatenate([a_s, r_s.astype(BF16)], axis=0), jnp.concatenate([b_s, k_s], axis=0), NT,
                             preferred_element_type=F32)
        a16[p, j] = a_s
        r32[p, j] = r_s
        bb16[p, j] = stack(bvec[sl] * et).astype(BF16)
        kb16[p, j] = stack(kmod_s[sl] * et).astype(BF16)
        v16[p, j] = stack(vval[sl]).astype(BF16)
        Ls[p, j] = jnp.where(strict, sc[:2 * C, :2 * C], 0.0)
        AKs[p, j] = jnp.where(strict, sc[:2 * C, 2 * C:], 0.0)
        RBs[p, j] = jnp.where(incl, sc[2 * C:, :2 * C], 0.0)
        RKs[p, j] = jnp.where(incl, sc[2 * C:, 2 * C:], 0.0)

    groups = [[(p, j) for j in range(g0, g0 + grp)] for g0 in range(0, n_chunks, grp) for p in range(n_pairs)]
    Ps = [bdiag([Ls[c_] for c_ in g_]) for g_ in groups]
    Tms = [eye_g + p_ for p_ in Ps]
    for _ in range(C.bit_length() - 2):
        for gi in range(len(groups)):
            P16 = Ps[gi].astype(BF16)
            Ps[gi] = mm(P16, P16)
            Tms[gi] = Tms[gi] + mm(Tms[gi].astype(BF16), Ps[gi].astype(BF16))

    RAs, YNs, Gs, Ns = {}, {}, {}, {}
    for gi, g_ in enumerate(groups):
        T16 = Tms[gi].astype(BF16)
        A2 = cat([a16[c_] for c_ in g_])
        V2 = cat([v16[c_] for c_ in g_])
        R2 = cat([r32[c_] for c_ in g_])
        RB16 = bdiag([RBs[c_] for c_ in g_]).astype(BF16)
        RK16 = bdiag([RKs[c_] for c_ in g_]).astype(BF16)
        TA16 = mm(T16, A2).astype(BF16)
        TV16 = mm(T16, mm(bdiag([AKs[c_] for c_ in g_]).astype(BF16), V2).astype(BF16)).astype(BF16)
        RA = R2 + mm(RB16, TA16)
        YN = mm(jnp.concatenate([RB16, RK16], axis=1), jnp.concatenate([TV16, V2], axis=0))
        for q, c_ in enumerate(g_):
            rs = slice(2 * C * q, 2 * C * (q + 1))
            RAs[c_] = RA[rs].astype(BF16)
            YNs[c_] = YN[rs]
            Gs[c_] = lax.dot_general(TA16[rs], bb16[c_], TN, preferred_element_type=F32).astype(BF16)
            Ns[c_] = lax.dot_general(jnp.concatenate([TV16[rs], v16[c_]], axis=0),
                                     jnp.concatenate([bb16[c_], kb16[c_]], axis=0), TN, preferred_element_type=F32)

    ys = {}
    Ss = [S_ref[p] for p in range(n_pairs)]
    for p, j in units:
        S16 = Ss[p].astype(BF16)
        Y = lax.dot_general(RAs[p, j], S16, NT, preferred_element_type=F32) + YNs[p, j]
        ys[p, j] = Y[:C] + Y[C:]
        Ss[p] = Ss[p] * gts[p, j] + mm(S16, Gs[p, j]) + Ns[p, j]
    for p in range(n_pairs):
        S_ref[p] = Ss[p]
    y_pairs = [cat([ys[p, j] for j in range(n_chunks)]) for p in range(n_pairs)]
    y = y_pairs[0] if n_pairs == 1 else jnp.concatenate(y_pairs, axis=1)

    mu = head_sum(y) * (1.0 / HEAD_DIM)
    var = head_sum(jnp.square(y - mu)) * (1.0 / HEAD_DIM)
    yn = (y - mu) * lax.rsqrt(var + GN_EPS) * lnw_ref[...] + lnb_ref[...]
    o_ref[0] = ((yn + bonus) * g).astype(BF16)

    @pl.when(c == pl.num_programs(2) - 1)
    def _():
        for p in range(n_pairs):
            so_ref[0, p] = Ss[p]


def rwkv_time_mix(proj3, shift0, s0, prm, t_valid, tc, out_width=MIX, pairs=2):
    N, T, _ = proj3.shape
    tc = min(tc, T)
    W = 2 * HEAD_DIM * pairs
    assert T % tc == 0 and tc % CHUNK == 0 and MIX % W == 0 and C_RWKV % W == 0
    rb = C_RWKV // W
    nb = MIX // W
    pspec = lambda w, off: pl.BlockSpec((1, tc, w), lambda n, h, c: (n, c, off(h)))
    sspec = lambda w, off: pl.BlockSpec((1, 1, w), lambda n, h, c: (n, 0, off(h)))
    mspec = lambda w, off: pl.BlockSpec((1, w), lambda n, h, c: (0, off(h)))
    hspec = pl.BlockSpec((1, W), lambda n, h, c: (0, h))
    assert (C_RWKV + 3 * MIX) % LORA_IN == 0
    lo_p = (C_RWKV + 3 * MIX) // LORA_IN
    lo_s = 3 * MIX // LORA_IN
    in_specs = [
        pspec(W, lambda h: rb + h), pspec(W, lambda h: rb + nb + h), pspec(W, lambda h: rb + 2 * nb + h),
        pspec(LORA_IN, lambda h: lo_p),
        sspec(W, lambda h: h), sspec(W, lambda h: nb + h), sspec(W, lambda h: 2 * nb + h),
        sspec(LORA_IN, lambda h: lo_s),
        mspec(W, lambda h: h), mspec(W, lambda h: nb + h), mspec(W, lambda h: 2 * nb + h),
        mspec(LORA_IN, lambda h: lo_s),
        hspec, hspec, hspec, hspec, hspec, hspec, hspec,
        pl.BlockSpec((DECAY_LORA, W), lambda n, h, c: (0, h)),
        pl.BlockSpec((AAA_LORA, W), lambda n, h, c: (0, h)),
        pl.BlockSpec((GATE_PAD, W), lambda n, h, c: (0, h)),
        pl.BlockSpec((1, pairs, 128, 128), lambda n, h, c: (n, h, 0, 0)),
    ]
    args = ([proj3] * 4 + [shift0] * 4 + [prm['mu']] * 4
            + [prm['w0'], prm['a0'], prm['k_k'], prm['k_a'], prm['r_k'], prm['ln_w'], prm['ln_b'],
               prm['w2'], prm['a2'], prm['g2'], s0])
    return pl.pallas_call(
        functools.partial(_rwkv_kernel, t_valid=t_valid, t_total=T),
        grid=(N, nb, T // tc),
        in_specs=in_specs,
        out_specs=[pl.BlockSpec((1, tc, W), lambda n, h, c: (n, c, M_RWKV // W + h)),
                   pl.BlockSpec((1, pairs, 128, 128), lambda n, h, c: (n, h, 0, 0))],
        out_shape=[jax.ShapeDtypeStruct((N, T, out_width), BF16),
                   jax.ShapeDtypeStruct((N, HEADS // 2, 128, 128), F32)],
        scratch_shapes=[pltpu.VMEM((pairs, 128, 128), F32), pltpu.VMEM((1, W), F32), pltpu.VMEM((1, W), F32),
                        pltpu.VMEM((1, W), F32), pltpu.VMEM((1, LORA_IN), F32)],
        compiler_params=_cparams(("parallel", "parallel", "arbitrary")),
        name="rwkv7_time_mix",
    )(*args)


def _pad_rwkv_cols(t):
    return jnp.pad(t, [(0, 0)] * (t.ndim - 1) + [(0, RWKV_PAD - RWKV_PROJ)])


PREP_ROWS = 512
PREP_COLS = 1024


def _prep_w_in_kernel(cur_ref, prev_ref, o_ref):
    j = pl.program_id(2)
    cur = cur_ref[...].astype(F32)
    prev = prev_ref[...].astype(F32)
    lane = lax.broadcasted_iota(jnp.int32, cur.shape, 1)
    col = lane + j * PREP_COLS
    shift = RWKV_PAD - RWKV_PROJ
    shifted = jnp.where(lane < shift, pltpu.roll(prev, shift, 1), pltpu.roll(cur, shift, 1))
    in_tail = (col >= RWKV_PAD) & (col < C_CONV + CONV_PROJ)
    o_ref[...] = jnp.where(col < RWKV_PROJ, cur, jnp.where(in_tail, shifted, 0.0)).astype(BF16)


def prep_w_in(w16):
    depth, K, n = w16.shape
    assert n == NP and K % PREP_ROWS == 0 and NP % PREP_COLS == 0
    blk = (None, PREP_ROWS, PREP_COLS)
    return pl.pallas_call(
        _prep_w_in_kernel,
        grid=(depth, K // PREP_ROWS, NP // PREP_COLS),
        in_specs=[pl.BlockSpec(blk, lambda l, i, j: (l, i, j)),
                  pl.BlockSpec(blk, lambda l, i, j: (l, i, jnp.maximum(j - 1, 0)))],
        out_specs=pl.BlockSpec(blk, lambda l, i, j: (l, i, j)),
        out_shape=jax.ShapeDtypeStruct((depth, K, NP), BF16),
        compiler_params=_cparams(("parallel", "parallel", "parallel")),
        name="prep_w_in",
    )(w16, w16)


def _matmul_weights(w_in, w_out, w_up, w_down):
    w_in16 = jnp.pad(w_in.astype(BF16), ((0, 0), (0, 0), (0, NP - w_in.shape[2])))
    return {'w_in': prep_w_in(w_in16), 'w_out': w_out.astype(BF16), 'w_up': w_up.astype(BF16), 'w_down': w_down.astype(BF16)}


def _layer_params(l, big, norm_mix, rwkv_mu, rwkv_w0, rwkv_w2, rwkv_a0, rwkv_a2, rwkv_g2, rwkv_k_k, rwkv_k_a,
                  rwkv_r_k, rwkv_ln_w, rwkv_ln_b, conv_w, conv_b, conv_ln_w, conv_ln_b, norm_ffn,
                  ffn_conv_w, ffn_conv_b):
    row = lambda t: t.reshape(1, -1)
    return {
        'w_in': big['w_in'], 'w_out': big['w_out'], 'w_up': big['w_up'], 'w_down': big['w_down'],
        'norm_mix': row(norm_mix[l]),
        'mu': _pad_rwkv_cols(row(rwkv_mu[l])),
        'w0': row(rwkv_w0[l]), 'a0': row(rwkv_a0[l]), 'k_k': row(rwkv_k_k[l]), 'k_a': row(rwkv_k_a[l]),
        'r_k': row(rwkv_r_k[l]), 'ln_w': row(rwkv_ln_w[l]), 'ln_b': row(rwkv_ln_b[l]),
        'w2': rwkv_w2[l].astype(BF16), 'a2': rwkv_a2[l].astype(BF16),
        'g2': jnp.pad(rwkv_g2[l], ((0, GATE_PAD - GATE_LORA), (0, 0))).astype(BF16),
        'conv_w': jnp.pad(conv_w[l], ((0, 32 - CONV_WIDTH), (0, 0))), 'conv_b': row(conv_b[l]),
        'conv_ln_w': row(conv_ln_w[l]), 'conv_ln_b': row(conv_ln_b[l]),
        'norm_ffn': row(norm_ffn[l]),
        'ffn_conv_w': jnp.pad(ffn_conv_w[l], ((0, 8 - FFN_CONV_WIDTH), (0, 0))), 'ffn_conv_b': row(ffn_conv_b[l]),
    }


def _rope_tables(pos):
    half = HEAD_DIM // 2
    inv_freq = ROPE_THETA ** (-jnp.arange(half, dtype=F32) * 2.0 / HEAD_DIM)
    ang = pos.astype(F32)[..., None] * inv_freq
    cos, sin = jnp.cos(ang), jnp.sin(ang)
    return jnp.concatenate([cos, cos, cos, cos], axis=-1), jnp.concatenate([-sin, sin, -sin, sin], axis=-1)


def _state_to_blockdiag(s):
    N = s.shape[0]
    s = s.reshape(N, HEADS // 2, 2, HEAD_DIM, HEAD_DIM)
    z = jnp.zeros_like(s[:, :, 0])
    top = jnp.concatenate([s[:, :, 0], z], axis=-1)
    bot = jnp.concatenate([z, s[:, :, 1]], axis=-1)
    return jnp.concatenate([top, bot], axis=-2)


def _blockdiag_to_state(s):
    N = s.shape[0]
    return jnp.stack([s[:, :, :HEAD_DIM, :HEAD_DIM], s[:, :, HEAD_DIM:, HEAD_DIM:]], axis=2).reshape(
        N, HEADS, HEAD_DIM, HEAD_DIM)


def _last_rows(buf, u, t_valid):
    keep = buf.shape[1]
    if t_valid >= keep:
        return u[:, t_valid - keep:t_valid]
    return jnp.concatenate([buf[:, t_valid:], u[:, :t_valid]], axis=1)


def _front_pad(buf, rows):
    return jnp.pad(buf, ((0, 0), (rows - buf.shape[1], 0), (0, 0)))


def _prompt_layer(x2, N, S, prm, cos_t, sin_t, layer, depth, kv_all):
    f32 = x2.dtype
    proj3 = norm_matmul(x2, prm['norm_mix'], prm['w_in'], layer, 1024, 512, single_buffer_x=True).reshape(N, S, NP)

    mixed, s_new = rwkv_time_mix(proj3, jnp.zeros((N, 1, RWKV_PAD), f32), jnp.zeros((N, HEADS // 2, 128, 128), f32),
                                 prm, S, 256, out_width=D_MODEL, pairs=4)
    shift_new = proj3[:, S - 1:S, C_RWKV:C_RWKV + RWKV_PROJ]

    mixed, k_all, v_all = prompt_attention(proj3, cos_t, sin_t, mixed, layer, depth, kv_all)

    u = glu(proj3, 1024)
    mixed = dwconv_ln_silu(u, jnp.zeros((N, CONV_HALO, CONV_DIM), f32), prm['conv_w'], prm['conv_b'],
                           prm['conv_ln_w'], prm['conv_ln_b'], 128, mixed=mixed)
    conv_new = u[:, S - (CONV_WIDTH - 1):]

    x2 = matmul_res(mixed.reshape(N * S, D_MODEL), prm['w_out'], layer, x2, 512, 512, D_MODEL)

    x2, tails = ffn_fused(x2, prm['norm_ffn'], prm['w_up'], prm['w_down'], layer, prm['ffn_conv_w'], prm['ffn_conv_b'],
                          jnp.zeros((N, FFN_HALO, 2 * D_FF), f32), S, 512, 256)
    keep = FFN_CONV_WIDTH - 1
    ffn_new = jnp.transpose(tails[:, :, :, FFN_HALO - keep:, :], (0, 3, 2, 1, 4)).reshape(N, keep, 2 * D_FF)
    return x2, (shift_new, _blockdiag_to_state(s_new), conv_new, ffn_new), (k_all, v_all)


def _sample_layer(x2, N, T, t_valid, prm, cos_t, sin_t, carry, attn_bufs, layer, depth, kv_all):
    shift0, wkv0, conv_buf, ffn_buf = carry
    proj3 = norm_matmul(x2, prm['norm_mix'], prm['w_in'], layer, 512, 512).reshape(N, T, NP)

    proj_r = jnp.pad(proj3, ((0, 0), (0, CHUNK - T), (0, 0)))
    o_rwkv, s_new = rwkv_time_mix(proj_r, _pad_rwkv_cols(shift0), _state_to_blockdiag(wkv0), prm, t_valid, CHUNK)
    o_rwkv = o_rwkv[:, :T]
    shift_new = proj3[:, t_valid - 1:t_valid, C_RWKV:C_RWKV + RWKV_PROJ]

    q_rot, k_rot = rope(proj3, cos_t, sin_t, 256)
    v_new = proj3[:, :, C_ATT + 2 * MIX:C_ATT + 3 * MIX]
    padr = ((0, 0), (0, 128 - T), (0, 0))
    o_att, k_all, v_all = sample_attention(q_rot, jnp.pad(k_rot, padr), jnp.pad(v_new, padr),
                                           attn_bufs[0], attn_bufs[1], t_valid, layer, depth, kv_all)

    u = glu(proj3, 256)
    o_conv = dwconv_ln_silu(u, _front_pad(conv_buf, CONV_HALO), prm['conv_w'], prm['conv_b'],
                            prm['conv_ln_w'], prm['conv_ln_b'], 128)
    conv_new = _last_rows(conv_buf, u, t_valid)

    mixed = jnp.concatenate([o_rwkv, o_att, o_conv], axis=-1).reshape(N * T, D_MODEL)
    x2 = matmul_res(mixed, prm['w_out'], layer, x2, 512, 512, D_MODEL)

    uf3 = norm_matmul(x2, prm['norm_ffn'], prm['w_up'], layer, 512, 512).reshape(N, T, 2 * D_FF)
    act = ffn_act(uf3, _front_pad(ffn_buf, FFN_HALO), prm['ffn_conv_w'], prm['ffn_conv_b'], 512, D_FF // 2)
    ffn_new = _last_rows(ffn_buf, uf3, t_valid)
    x2 = matmul_res(act.reshape(N * T, D_FF), prm['w_down'], layer, x2, 512, 512, D_FF // 2)
    return x2, (shift_new, _blockdiag_to_state(s_new), conv_new, ffn_new), (k_all, v_all)


def kernel(x_prompt, x_sample, state_rwkv_shift, state_rwkv_wkv, state_attn_k, state_attn_v, state_conv, state_ffn_conv, pos_sample, norm_mix, w_in, rwkv_mu, rwkv_w0, rwkv_w2, rwkv_a0, rwkv_a2, rwkv_g2, rwkv_k_k, rwkv_k_a, rwkv_r_k, rwkv_ln_w, rwkv_ln_b, conv_w, conv_b, conv_ln_w, conv_ln_b, w_out, norm_ffn, w_up, ffn_conv_w, ffn_conv_b, w_down, norm_final):
    B, S, _ = x_prompt.shape
    NB, TS, _ = x_sample.shape
    depth = w_in.shape[0]
    win_buf = state_attn_k.shape[2]
    assert win_buf == S, "prompt key/value state is the whole rotated sequence"
    TSP = 8

    xp = x_prompt.reshape(B * S, D_MODEL)
    xs = jnp.pad(x_sample, ((0, 0), (0, TSP - TS), (0, 0))).reshape(NB * TSP, D_MODEL)
    cos_p, sin_p = _rope_tables(jnp.arange(S, dtype=jnp.int32)[None])
    cos_s, sin_s = _rope_tables(jnp.pad(pos_sample, ((0, 0), (0, TSP - TS))))
    new_p, new_s = [], []
    kv_p, kv_s = None, None
    big = _matmul_weights(w_in, w_out, w_up, w_down)
    bufs = (state_attn_k.reshape(depth, NB, win_buf, MIX), state_attn_v.reshape(depth, NB, win_buf, MIX))
    for l in range(depth):
        prm = _layer_params(l, big, norm_mix, rwkv_mu, rwkv_w0, rwkv_w2, rwkv_a0, rwkv_a2, rwkv_g2, rwkv_k_k,
                            rwkv_k_a, rwkv_r_k, rwkv_ln_w, rwkv_ln_b, conv_w, conv_b, conv_ln_w, conv_ln_b,
                            norm_ffn, ffn_conv_w, ffn_conv_b)
        xp, st_p, kv_p = _prompt_layer(xp, B, S, prm, cos_p, sin_p, l, depth, kv_p)
        carry_s = (state_rwkv_shift[l], state_rwkv_wkv[l], state_conv[l], state_ffn_conv[l])
        xs, st_s, kv_s = _sample_layer(xs, NB, TSP, TS, prm, cos_s, sin_s, carry_s, bufs, l, depth, kv_s)
        new_p.append(st_p)
        new_s.append(st_s)

    g = norm_final.reshape(1, D_MODEL)
    y_prompt = rmsnorm(xp, g, 512).reshape(B, S, D_MODEL)
    y_sample = rmsnorm(xs, g, 512).reshape(NB, TSP, D_MODEL)[:, :TS]

    def stack(states, i):
        return jnp.stack([st[i] for st in states], axis=0)

    heads_p = (depth, B, win_buf, HEADS, HEAD_DIM)
    heads_s = (depth, NB, win_buf, HEADS, HEAD_DIM)
    return (y_prompt, y_sample,
            stack(new_p, 0), stack(new_s, 0), stack(new_p, 1), stack(new_s, 1),
            kv_p[0].reshape(heads_p), kv_s[0].reshape(heads_s), kv_p[1].reshape(heads_p), kv_s[1].reshape(heads_s),
            stack(new_p, 2), stack(new_s, 2), stack(new_p, 3), stack(new_s, 3))
```

```python
import functools

import jax
import jax.numpy as jnp
from jax import lax
from jax.experimental import pallas as pl
from jax.experimental.pallas import tpu as pltpu

F32 = jnp.float32
BF16 = jnp.bfloat16

D_MODEL = 4096
HEAD_DIM = 64
HEADS = 24
MIX = HEADS * HEAD_DIM
CONV_DIM = 1024
DECAY_LORA = 128
AAA_LORA = 128
GATE_LORA = 480
GATE_PAD = 512
RWKV_PROJ = 3 * MIX + DECAY_LORA + AAA_LORA + GATE_LORA
RWKV_PAD = 3 * MIX + DECAY_LORA + AAA_LORA + GATE_PAD
ATT_PROJ = 3 * MIX
CONV_PROJ = 2 * CONV_DIM
CONV_WIDTH = 31
CONV_HALO = 32
D_FF = 11008
FFN_CONV_WIDTH = 3
FFN_HALO = 8
BAND = 128
DILATIONS = (1, 4, 16)
WINDOWS = (128, 512, 2048)
ROPE_THETA = 10000.0
RMS_EPS = 1e-6
LN_EPS = 1e-5
GN_EPS = 64e-5
NEG_INF = -1e30
CHUNK = 64

C_RWKV = 0
C_ATT = RWKV_PAD
C_CONV = C_ATT + ATT_PROJ
NP = 12288
LORA_IN = DECAY_LORA + AAA_LORA + GATE_PAD
M_RWKV, M_ATT, M_CONV = 0, MIX, 2 * MIX

VMEM_LIMIT = 56 * 1024 * 1024

NT = (((1,), (1,)), ((), ()))
TN = (((0,), (0,)), ((), ()))


def _cparams(sem):
    return pltpu.CompilerParams(dimension_semantics=sem, vmem_limit_bytes=VMEM_LIMIT)


def _sigmoid(x):
    return 1.0 / (1.0 + jnp.exp(-x))


def _head0_mask(shape):
    return lax.broadcasted_iota(jnp.int32, shape, 1) % (2 * HEAD_DIM) < HEAD_DIM


def _norm_matmul_kernel(x_ref, g_ref, w_ref, o_ref, h_ref):
    @pl.when(pl.program_id(1) == 0)
    def _():
        x = x_ref[...]
        ms = jnp.mean(x * x, axis=-1, keepdims=True)
        h_ref[...] = (x * lax.rsqrt(ms + RMS_EPS) * g_ref[...]).astype(BF16)

    o_ref[...] = jnp.dot(h_ref[...], w_ref[...], preferred_element_type=F32)


def norm_matmul(x, g, w, layer, tm, tn, single_buffer_x=False):
    M, K = x.shape
    N = w.shape[2]
    tm = min(tm, M)
    assert M % tm == 0 and N % tn == 0
    x_mode = dict(pipeline_mode=pl.Buffered(1)) if single_buffer_x else {}
    return pl.pallas_call(
        _norm_matmul_kernel,
        grid=(M // tm, N // tn),
        in_specs=[pl.BlockSpec((tm, K), lambda i, j: (i, 0), **x_mode),
                  pl.BlockSpec((1, K), lambda i, j: (0, 0)),
                  pl.BlockSpec((None, K, tn), lambda i, j: (layer, 0, j))],
        out_specs=pl.BlockSpec((tm, tn), lambda i, j: (i, j)),
        out_shape=jax.ShapeDtypeStruct((M, N), F32),
        scratch_shapes=[pltpu.VMEM((tm, K), BF16)],
        compiler_params=_cparams(("parallel", "arbitrary")),
        name="norm_matmul",
    )(x, g, w)


def _matmul_res_kernel(a_ref, w_ref, r_ref, o_ref, acc_ref, *, nk):
    k = pl.program_id(2)

    @pl.when(k == 0)
    def _():
        acc_ref[...] = r_ref[...]

    acc_ref[...] += jnp.dot(a_ref[...], w_ref[...], preferred_element_type=F32)

    @pl.when(k == nk - 1)
    def _():
        o_ref[...] = acc_ref[...]


def matmul_res(a, w, layer, res, tm, tn, tk):
    M, K = a.shape
    N = w.shape[2]
    tm = min(tm, M)
    assert M % tm == 0 and N % tn == 0 and K % tk == 0
    nk = K // tk
    return pl.pallas_call(
        functools.partial(_matmul_res_kernel, nk=nk),
        grid=(M // tm, N // tn, nk),
        in_specs=[pl.BlockSpec((tm, tk), lambda i, j, k: (i, k)),
                  pl.BlockSpec((None, tk, tn), lambda i, j, k: (layer, k, j)),
                  pl.BlockSpec((tm, tn), lambda i, j, k: (i, j))],
        out_specs=pl.BlockSpec((tm, tn), lambda i, j, k: (i, j)),
        out_shape=jax.ShapeDtypeStruct((M, N), F32),
        scratch_shapes=[pltpu.VMEM((tm, tn), F32)],
        compiler_params=_cparams(("parallel", "parallel", "arbitrary")),
        name="matmul_res",
    )(a, w, res)


def _rmsnorm_kernel(x_ref, g_ref, o_ref):
    x = x_ref[...]
    ms = jnp.mean(x * x, axis=-1, keepdims=True)
    o_ref[...] = x * lax.rsqrt(ms + RMS_EPS) * g_ref[...]


def rmsnorm(x, g, tm):
    M, K = x.shape
    tm = min(tm, M)
    return pl.pallas_call(
        _rmsnorm_kernel,
        grid=(M // tm,),
        in_specs=[pl.BlockSpec((tm, K), lambda i: (i, 0)), pl.BlockSpec((1, K), lambda i: (0, 0))],
        out_specs=pl.BlockSpec((tm, K), lambda i: (i, 0)),
        out_shape=jax.ShapeDtypeStruct((M, K), F32),
        compiler_params=_cparams(("parallel",)),
        name="rmsnorm",
    )(x, g)


def _rot_half(x, first_half):
    w = x.shape[1]
    return jnp.where(first_half, pltpu.roll(x, w - HEAD_DIM // 2, 1), pltpu.roll(x, HEAD_DIM // 2, 1))


ROPE_BLOCK = 256


def _rope_kernel(q_ref, k_ref, cos_ref, sin_ref, qo_ref, ko_ref):
    reps = ROPE_BLOCK // (2 * HEAD_DIM)
    cos = jnp.concatenate([cos_ref[0]] * reps, axis=1)
    sin = jnp.concatenate([sin_ref[0]] * reps, axis=1)
    lane = lax.broadcasted_iota(jnp.int32, cos.shape, 1)
    first_half = (lane % HEAD_DIM) < (HEAD_DIM // 2)
    q = q_ref[0]
    k = k_ref[0]
    qo_ref[0] = q * cos + _rot_half(q, first_half) * sin
    ko_ref[0] = k * cos + _rot_half(k, first_half) * sin


def rope(proj3, cos_t, sin_t, tt):
    N, T, _ = proj3.shape
    tt = min(tt, T)
    tab = lambda n, i, j: (n, i, 0)
    rw = ROPE_BLOCK
    assert C_ATT % rw == 0 and MIX % rw == 0
    qb = C_ATT // rw
    kb = (C_ATT + MIX) // rw
    out = jax.ShapeDtypeStruct((N, T, MIX), F32)
    return pl.pallas_call(
        _rope_kernel,
        grid=(N, T // tt, MIX // rw),
        in_specs=[pl.BlockSpec((1, tt, rw), lambda n, i, j: (n, i, qb + j)),
                  pl.BlockSpec((1, tt, rw), lambda n, i, j: (n, i, kb + j)),
                  pl.BlockSpec((1, tt, 128), tab),
                  pl.BlockSpec((1, tt, 128), tab)],
        out_specs=[pl.BlockSpec((1, tt, rw), lambda n, i, j: (n, i, j)),
                   pl.BlockSpec((1, tt, rw), lambda n, i, j: (n, i, j))],
        out_shape=[out, out],
        compiler_params=_cparams(("parallel", "parallel", "parallel")),
        name="rope",
    )(proj3, proj3, cos_t, sin_t)


def _prompt_attn_kernel(*refs):
    q_ref, k_ref, v_ref, cos_ref, sin_ref = refs[:5]
    o_ref, ko_all_ref, vo_all_ref, qs_ref, acc_ref, m_ref, l_ref = refs[-7:]
    ko_ref = ko_all_ref.at[0]
    vo_ref = vo_all_ref.at[0]
    S = q_ref.shape[1]
    B = BAND
    P = 2 * HEAD_DIM
    lane = lax.broadcasted_iota(jnp.int32, (B, P), 1)
    first_half = (lane % HEAD_DIM) < (HEAD_DIM // 2)
    h0 = _head0_mask((B, P))
    scale = HEAD_DIM ** -0.5

    def rope_rows(i, carry):
        rows = pl.ds(pl.multiple_of(i * B, B), B)
        cos = cos_ref[0, rows, :]
        sin = sin_ref[0, rows, :]
        q = q_ref[0, rows, :]
        k = k_ref[0, rows, :]
        qs_ref[rows, :] = q * cos + _rot_half(q, first_half) * sin
        ko_ref[0, rows, :] = k * cos + _rot_half(k, first_half) * sin
        vo_ref[0, rows, :] = v_ref[0, rows, :]
        return carry

    lax.fori_loop(0, S // B, rope_rows, 0)

    def attend(q, kk, vv, valid):
        ms, ls, os_ = [], [], []
        for h in range(2):
            qh = jnp.where(h0 if h == 0 else ~h0, q, 0.0).astype(BF16)
            s = lax.dot_general(qh, kk, NT, preferred_element_type=F32) * scale
            s = jnp.where(valid, s, NEG_INF)
            m = jnp.max(s, axis=-1, keepdims=True)
            e = jnp.exp(s - m)
            ms.append(m)
            ls.append(jnp.sum(e, axis=-1, keepdims=True))
            os_.append(jnp.dot(e.astype(BF16), vv, preferred_element_type=F32))
        return jnp.where(h0, ms[0], ms[1]), jnp.where(h0, ls[0], ls[1]), jnp.where(h0, os_[0], os_[1])

    def merge(rows, m_c, l_c, o_c):
        m_p = m_ref[rows, :]
        m_n = jnp.maximum(m_p, m_c)
        a_p = jnp.exp(m_p - m_n)
        a_c = jnp.exp(m_c - m_n)
        acc_ref[rows, :] = acc_ref[rows, :] * a_p + o_c * a_c
        l_ref[rows, :] = l_ref[rows, :] * a_p + l_c * a_c
        m_ref[rows, :] = m_n

    qi2 = lax.broadcasted_iota(jnp.int32, (B, 2 * B), 0)
    ki2 = lax.broadcasted_iota(jnp.int32, (B, 2 * B), 1)
    dist2 = qi2 + B - ki2
    band2 = (dist2 >= 0) & (dist2 <= B)
    qi1 = lax.broadcasted_iota(jnp.int32, (B, B), 0)
    ki1 = lax.broadcasted_iota(jnp.int32, (B, B), 1)
    causal1 = qi1 >= ki1

    def two_block_keys(cur, prev):
        kk = jnp.concatenate([ko_ref[0, prev, :], ko_ref[0, cur, :]], axis=0).astype(BF16)
        vv = jnp.concatenate([v_ref[0, prev, :], v_ref[0, cur, :]], axis=0).astype(BF16)
        return kk, vv

    d16 = DILATIONS[2]
    assert S == B * d16
    per_iter = 4

    def dil16(it, carry):
        for u in range(per_iter):
            cur = pl.ds(it * per_iter + u, B, stride=d16)
            kk, vv = two_block_keys(cur, cur)
            m_c, l_c, o_c = attend(qs_ref[cur, :], kk, vv, band2 & (ki2 >= B))
            acc_ref[cur, :] = o_c
            m_ref[cur, :] = m_c
            l_ref[cur, :] = l_c
        return carry

    lax.fori_loop(0, d16 // per_iter, dil16, 0)

    d4 = DILATIONS[1]

    def dil4(ib, carry):
        base = pl.multiple_of(ib * (B * d4), B * d4)
        pbase = pl.multiple_of(jnp.maximum(ib - 1, 0) * (B * d4), B * d4)
        for r in range(d4):
            cur = pl.ds(base + r, B, stride=d4)
            prev = pl.ds(pbase + r, B, stride=d4)
            kk, vv = two_block_keys(cur, prev)
            m_c, l_c, o_c = attend(qs_ref[cur, :], kk, vv, band2 & ((ki2 >= B) | (ib > 0)))
            merge(cur, m_c, l_c, o_c)
        return carry

    lax.fori_loop(0, S // (B * d4), dil4, 0)

    unroll = 4

    def dil1(it, carry):
        for u in range(unroll):
            ib = it * unroll + u
            cur = pl.ds(pl.multiple_of(ib * B, B), B)
            prev = pl.ds(pl.multiple_of(jnp.maximum(ib - 1, 0) * B, B), B)
            kk, vv = two_block_keys(cur, prev)
            m_c, l_c, o_c = attend(qs_ref[cur, :], kk, vv, band2 & ((ki2 >= B) | (ib > 0)))
            m_p = m_ref[cur, :]
            m_n = jnp.maximum(m_p, m_c)
            a_p = jnp.exp(m_p - m_n)
            a_c = jnp.exp(m_c - m_n)
            acc = acc_ref[cur, :] * a_p + o_c * a_c
            l_n = l_ref[cur, :] * a_p + l_c * a_c
            o_ref[0, cur, :] = (acc / l_n).astype(BF16)
        return carry

    lax.fori_loop(0, S // (B * unroll), dil1, 0)


def prompt_attention(proj3, cos_t, sin_t, mixed, layer, depth, kv_all=None):
    N, S, _ = proj3.shape
    qb, kb, vb = C_ATT // 128, (C_ATT + MIX) // 128, (C_ATT + 2 * MIX) // 128
    col = lambda b: pl.BlockSpec((1, S, 128), lambda n, h: (n, 0, b + h))
    tab = pl.BlockSpec((1, S, 128), lambda n, h: (0, 0, 0))
    anyspec = pl.BlockSpec(memory_space=pl.ANY)
    kv = jax.ShapeDtypeStruct((depth, N, S, MIX), F32)
    kv_spec = pl.BlockSpec((1, 1, S, 128), lambda n, h: (layer, n, 0, h))
    args = [proj3, proj3, proj3, cos_t, sin_t, mixed]
    in_specs = [col(qb), col(kb), col(vb), tab, tab, anyspec]
    aliases = {5: 0}
    if kv_all is not None:
        args += list(kv_all)
        in_specs += [anyspec, anyspec]
        aliases.update({6: 1, 7: 2})
    return pl.pallas_call(
        _prompt_attn_kernel,
        grid=(N, HEADS // 2),
        in_specs=in_specs,
        out_specs=[col(M_ATT // 128), kv_spec, kv_spec],
        out_shape=[jax.ShapeDtypeStruct(mixed.shape, mixed.dtype), kv, kv],
        scratch_shapes=[pltpu.VMEM((S, 128), F32)] * 4,
        input_output_aliases=aliases,
        compiler_params=_cparams(("parallel", "parallel")),
        name="prompt_attention",
    )(*args)


def _sample_attn_kernel(*refs, t_valid):
    q_ref, kn_ref, vn_ref, kb_ref, vb_ref = refs[:5]
    o_ref, ks_all_ref, vs_all_ref = refs[-3:]
    ks_ref = ks_all_ref.at[0]
    vs_ref = vs_all_ref.at[0]
    TQ = q_ref.shape[1]
    M = kb_ref.shape[1]
    TN_ = kn_ref.shape[1]
    h0 = _head0_mask((TQ, 2 * HEAD_DIM))
    scale = HEAD_DIM ** -0.5

    def counts(delta, in_range):
        c = jnp.zeros(delta.shape, F32)
        for win, dil in zip(WINDOWS, DILATIONS):
            c = c + jnp.where((delta % dil == 0) & (delta <= win) & in_range, 1.0, 0.0)
        return c

    tq = lax.broadcasted_iota(jnp.int32, (TQ, M), 0)
    kb_i = lax.broadcasted_iota(jnp.int32, (TQ, M), 1)
    d_buf = M + tq - kb_i
    c_buf = counts(d_buf, d_buf >= 0)
    tq2 = lax.broadcasted_iota(jnp.int32, (TQ, TN_), 0)
    tn2 = lax.broadcasted_iota(jnp.int32, (TQ, TN_), 1)
    d_new = tq2 - tn2
    c_new = counts(d_new, (d_new >= 0) & (tn2 < t_valid))

    row8 = lax.broadcasted_iota(jnp.int32, (8, 2 * HEAD_DIM), 0)
    for p in range(4):
        sl = slice(128 * p, 128 * (p + 1))
        qp = q_ref[0, :, sl]
        kb = kb_ref[0, :, sl]
        vb = vb_ref[0, :, sl]
        kn = kn_ref[0, :, sl]
        vn = vn_ref[0, :, sl]
        kb16, vb16, kn16, vn16 = kb.astype(BF16), vb.astype(BF16), kn.astype(BF16), vn.astype(BF16)
        outs = []
        for h in range(2):
            qh = jnp.where(h0 if h == 0 else ~h0, qp, 0.0).astype(BF16)
            s_b = lax.dot_general(qh, kb16, NT, preferred_element_type=F32) * scale
            s_n = lax.dot_general(qh, kn16, NT, preferred_element_type=F32) * scale
            s_b = jnp.where(c_buf > 0, s_b, NEG_INF)
            s_n = jnp.where(c_new > 0, s_n, NEG_INF)
            m = jnp.maximum(jnp.max(s_b, axis=-1, keepdims=True), jnp.max(s_n, axis=-1, keepdims=True))
            w_b = c_buf * jnp.exp(s_b - m)
            w_n = c_new * jnp.exp(s_n - m)
            l = jnp.sum(w_b, axis=-1, keepdims=True) + jnp.sum(w_n, axis=-1, keepdims=True)
            o = (jnp.dot(w_b.astype(BF16), vb16, preferred_element_type=F32)
                 + jnp.dot(w_n.astype(BF16), vn16, preferred_element_type=F32))
            outs.append(o / l)
        o_ref[0, :, sl] = jnp.where(h0, outs[0], outs[1]).astype(BF16)

        for src, new, dst in ((kb, kn, ks_ref), (vb, vn, vs_ref)):
            rolled = pltpu.roll(src, M - t_valid, 0)
            new_r = pltpu.roll(new[0:8], 8 - t_valid, 0)
            dst[0, 0:M - 8, sl] = rolled[0:M - 8]
            dst[0, M - 8:M, sl] = jnp.where(row8 >= 8 - t_valid, new_r, rolled[M - 8:M])


def sample_attention(q_rot, k_new, v_new, k_buf, v_buf, t_valid, layer, depth, kv_all=None):
    N, TQ, _ = q_rot.shape
    M = k_buf.shape[2]
    TN_ = k_new.shape[1]
    assert t_valid <= 8 <= TQ
    blk = lambda r: pl.BlockSpec((1, r, 512), lambda n, j: (n, 0, j))
    buf = pl.BlockSpec((None, 1, M, 512), lambda n, j: (layer, n, 0, j))
    kv = jax.ShapeDtypeStruct((depth, N, M, MIX), F32)
    kv_spec = pl.BlockSpec((1, 1, M, 512), lambda n, j: (layer, n, 0, j))
    args = [q_rot, k_new, v_new, k_buf, v_buf]
    in_specs = [blk(TQ), blk(TN_), blk(TN_), buf, buf]
    aliases = {}
    if kv_all is not None:
        args += list(kv_all)
        in_specs += [pl.BlockSpec(memory_space=pl.ANY)] * 2
        aliases = {5: 1, 6: 2}
    return pl.pallas_call(
        functools.partial(_sample_attn_kernel, t_valid=t_valid),
        grid=(N, MIX // 512),
        in_specs=in_specs,
        out_specs=[blk(TQ), kv_spec, kv_spec],
        out_shape=[jax.ShapeDtypeStruct((N, TQ, MIX), BF16), kv, kv],
        input_output_aliases=aliases,
        compiler_params=_cparams(("parallel", "parallel")),
        name="sample_attention",
    )(*args)


def _glu_kernel(a_ref, g_ref, o_ref):
    o_ref[0] = a_ref[0] * _sigmoid(g_ref[0])


def glu(proj3, tt):
    N, T, _ = proj3.shape
    tt = min(tt, T)
    gw = 256
    assert C_CONV % gw == 0 and CONV_DIM % gw == 0
    ab = C_CONV // gw
    gb = (C_CONV + CONV_DIM) // gw
    return pl.pallas_call(
        _glu_kernel,
        grid=(N, T // tt, CONV_DIM // gw),
        in_specs=[pl.BlockSpec((1, tt, gw), lambda n, i, j: (n, i, ab + j)),
                  pl.BlockSpec((1, tt, gw), lambda n, i, j: (n, i, gb + j))],
        out_specs=pl.BlockSpec((1, tt, gw), lambda n, i, j: (n, i, j)),
        out_shape=jax.ShapeDtypeStruct((N, T, CONV_DIM), F32),
        compiler_params=_cparams(("parallel", "parallel", "parallel")),
        name="glu",
    )(proj3, proj3)


def _dwconv_kernel(u_ref, uh_ref, h0_ref, w_ref, b_ref, lw_ref, lb_ref, *rest, single_tile):
    o_ref, ext_ref, y_ref = rest[-3:]
    tt = u_ref.shape[1]
    i = pl.program_id(1)

    @pl.when(i == 0)
    def _():
        ext_ref[0:CONV_HALO, :] = h0_ref[0]

    if not single_tile:
        @pl.when(i > 0)
        def _():
            ext_ref[0:CONV_HALO, :] = uh_ref[0]

    ext_ref[CONV_HALO:, :] = u_ref[0]
    lead = CONV_HALO - (CONV_WIDTH - 1)
    rows = min(16, tt)
    for r0 in range(0, tt, rows):
        for c0 in range(0, CONV_DIM, 512):
            acc = jnp.broadcast_to(b_ref[:, c0:c0 + 512], (rows, 512))
            for j in range(CONV_WIDTH):
                acc = acc + ext_ref[r0 + lead + j:r0 + lead + j + rows, c0:c0 + 512] * w_ref[j:j + 1, c0:c0 + 512]
            y_ref[r0:r0 + rows, c0:c0 + 512] = acc
    y = y_ref[...]
    mu = jnp.mean(y, axis=-1, keepdims=True)
    var = jnp.mean(jnp.square(y - mu), axis=-1, keepdims=True)
    c = (y - mu) * lax.rsqrt(var + LN_EPS) * lw_ref[...] + lb_ref[...]
    o_ref[0] = (c * _sigmoid(c)).astype(BF16)


def dwconv_ln_silu(u, halo0, w, b, lw, lb, tt, mixed=None):
    N, T, C = u.shape
    tt = min(tt, T)
    hb = tt // CONV_HALO if tt >= CONV_HALO else 1
    in_specs = [pl.BlockSpec((1, tt, C), lambda n, i: (n, i, 0)),
                pl.BlockSpec((1, min(CONV_HALO, T), C), lambda n, i: (n, jnp.maximum(i * hb - 1, 0), 0)),
                pl.BlockSpec((1, CONV_HALO, C), lambda n, i: (n, 0, 0)),
                pl.BlockSpec((32, C), lambda n, i: (0, 0)),
                pl.BlockSpec((1, C), lambda n, i: (0, 0)),
                pl.BlockSpec((1, C), lambda n, i: (0, 0)),
                pl.BlockSpec((1, C), lambda n, i: (0, 0))]
    args = [u, u, halo0, w, b, lw, lb]
    if mixed is None:
        out_shape = jax.ShapeDtypeStruct((N, T, C), BF16)
        out_spec = pl.BlockSpec((1, tt, C), lambda n, i: (n, i, 0))
        aliases = {}
    else:
        in_specs.append(pl.BlockSpec(memory_space=pl.ANY))
        args.append(mixed)
        out_shape = jax.ShapeDtypeStruct(mixed.shape, mixed.dtype)
        out_spec = pl.BlockSpec((1, tt, C), lambda n, i: (n, i, M_CONV // C))
        aliases = {7: 0}
    return pl.pallas_call(
        functools.partial(_dwconv_kernel, single_tile=(T == tt)),
        grid=(N, T // tt),
        in_specs=in_specs,
        out_specs=out_spec,
        out_shape=out_shape,
        scratch_shapes=[pltpu.VMEM((CONV_HALO + tt, C), F32), pltpu.VMEM((tt, C), F32)],
        input_output_aliases=aliases,
        compiler_params=_cparams(("parallel", "arbitrary")),
        name="dwconv_ln_silu",
    )(*args)


def _ffn_conv_act(ext_ref, wg_ref, wv_ref, bg_ref, bv_ref, store, tt):
    lead = FFN_HALO - (FFN_CONV_WIDTH - 1)
    rows = min(32, tt)
    for r0 in range(0, tt, rows):
        ys = []
        for s, w_ref, b_ref in ((0, wg_ref, bg_ref), (1, wv_ref, bv_ref)):
            acc = jnp.broadcast_to(b_ref[...], (rows, b_ref.shape[1]))
            for j in range(FFN_CONV_WIDTH):
                acc = acc + ext_ref[s, r0 + lead + j:r0 + lead + j + rows, :] * w_ref[j:j + 1, :]
            ys.append(acc)
        gate, val = ys
        store(r0, rows, (gate * _sigmoid(gate) * val).astype(BF16))


def _ffn_act_kernel(ug_ref, uv_ref, hg_ref, hv_ref, h0g_ref, h0v_ref, wg_ref, wv_ref, bg_ref, bv_ref, o_ref, ext_ref):
    tt = ug_ref.shape[1]
    i = pl.program_id(1)

    @pl.when(i == 0)
    def _():
        ext_ref[0, 0:FFN_HALO, :] = h0g_ref[0]
        ext_ref[1, 0:FFN_HALO, :] = h0v_ref[0]

    @pl.when(i > 0)
    def _():
        ext_ref[0, 0:FFN_HALO, :] = hg_ref[0]
        ext_ref[1, 0:FFN_HALO, :] = hv_ref[0]

    ext_ref[0, FFN_HALO:, :] = ug_ref[0]
    ext_ref[1, FFN_HALO:, :] = uv_ref[0]

    def store(r0, rows, val):
        o_ref[0, r0:r0 + rows, :] = val

    _ffn_conv_act(ext_ref, wg_ref, wv_ref, bg_ref, bv_ref, store, tt)


def ffn_act(u3, halo0, w, b, tt, tf):
    N, T, F2 = u3.shape
    F = F2 // 2
    tt = min(tt, T)
    nf = F // tf
    hb = tt // FFN_HALO
    g = lambda n, i, j: (n, i, j)
    v = lambda n, i, j: (n, i, j + nf)
    hg = lambda n, i, j: (n, jnp.maximum(i * hb - 1, 0), j)
    hv = lambda n, i, j: (n, jnp.maximum(i * hb - 1, 0), j + nf)
    return pl.pallas_call(
        _ffn_act_kernel,
        grid=(N, T // tt, nf),
        in_specs=[pl.BlockSpec((1, tt, tf), g), pl.BlockSpec((1, tt, tf), v),
                  pl.BlockSpec((1, FFN_HALO, tf), hg), pl.BlockSpec((1, FFN_HALO, tf), hv),
                  pl.BlockSpec((1, FFN_HALO, tf), lambda n, i, j: (n, 0, j)),
                  pl.BlockSpec((1, FFN_HALO, tf), lambda n, i, j: (n, 0, j + nf)),
                  pl.BlockSpec((8, tf), lambda n, i, j: (0, j)), pl.BlockSpec((8, tf), lambda n, i, j: (0, j + nf)),
                  pl.BlockSpec((1, tf), lambda n, i, j: (0, j)), pl.BlockSpec((1, tf), lambda n, i, j: (0, j + nf))],
        out_specs=pl.BlockSpec((1, tt, tf), g),
        out_shape=jax.ShapeDtypeStruct((N, T, F), BF16),
        scratch_shapes=[pltpu.VMEM((2, FFN_HALO + tt, tf), F32)],
        compiler_params=_cparams(("parallel", "parallel", "parallel")),
        name="ffn_act",
    )(u3, u3, u3, u3, halo0, halo0, w, w, b, b)


def _ffn_fused_kernel(x_ref, g_ref, wg_ref, wv_ref, wd_ref, cwg_ref, cwv_ref, cbg_ref, cbv_ref, h0g_ref, h0v_ref,
                      o_ref, st_ref, h_ref, ext_ref, carry_ref, act_ref, *, tiles_per_seq):
    i = pl.program_id(0)
    j = pl.program_id(1)
    tm = x_ref.shape[0]

    @pl.when(j == 0)
    def _():
        x = x_ref[...]
        ms = jnp.mean(x * x, axis=-1, keepdims=True)
        h_ref[...] = (x * lax.rsqrt(ms + RMS_EPS) * g_ref[...]).astype(BF16)
        o_ref[...] = x

    h = h_ref[...]
    ug = jnp.dot(h, wg_ref[...], preferred_element_type=F32)
    uv = jnp.dot(h, wv_ref[...], preferred_element_type=F32)
    seq_start = (i % tiles_per_seq) == 0
    ext_ref[0, 0:FFN_HALO, :] = jnp.where(seq_start, h0g_ref[0], carry_ref[j, 0])
    ext_ref[1, 0:FFN_HALO, :] = jnp.where(seq_start, h0v_ref[0], carry_ref[j, 1])
    ext_ref[0, FFN_HALO:, :] = ug
    ext_ref[1, FFN_HALO:, :] = uv
    carry_ref[j, 0] = ug[tm - FFN_HALO:, :]
    carry_ref[j, 1] = uv[tm - FFN_HALO:, :]

    def store(r0, rows, val):
        act_ref[r0:r0 + rows, :] = val

    _ffn_conv_act(ext_ref, cwg_ref, cwv_ref, cbg_ref, cbv_ref, store, tm)
    o_ref[...] += jnp.dot(act_ref[...], wd_ref[...], preferred_element_type=F32)

    @pl.when((j == pl.num_programs(1) - 1) & (i % tiles_per_seq == tiles_per_seq - 1))
    def _():
        st_ref[0] = carry_ref[...]


def ffn_fused(x, g, w_up, w_down, layer, cw, cb, halo0, seq_len, tm, tf):
    M, D = x.shape
    F = w_down.shape[1]
    assert M % tm == 0 and seq_len % tm == 0 and F % tf == 0
    nf = F // tf
    tps = seq_len // tm
    gate = lambda i, j: (0, j)
    val = lambda i, j: (0, j + nf)
    return pl.pallas_call(
        functools.partial(_ffn_fused_kernel, tiles_per_seq=tps),
        grid=(M // tm, nf),
        in_specs=[pl.BlockSpec((tm, D), lambda i, j: (i, 0), pipeline_mode=pl.Buffered(1)),
                  pl.BlockSpec((1, D), lambda i, j: (0, 0)),
                  pl.BlockSpec((None, D, tf), lambda i, j: (layer, 0, j)),
                  pl.BlockSpec((None, D, tf), lambda i, j: (layer, 0, j + nf)),
                  pl.BlockSpec((None, tf, D), lambda i, j: (layer, j, 0)),
                  pl.BlockSpec((8, tf), gate), pl.BlockSpec((8, tf), val),
                  pl.BlockSpec((1, tf), gate), pl.BlockSpec((1, tf), val),
                  pl.BlockSpec((1, FFN_HALO, tf), lambda i, j: (i // tps, 0, j)),
                  pl.BlockSpec((1, FFN_HALO, tf), lambda i, j: (i // tps, 0, j + nf))],
        out_specs=[pl.BlockSpec((tm, D), lambda i, j: (i, 0)),
                   pl.BlockSpec((1, nf, 2, FFN_HALO, tf), lambda i, j: (i // tps, 0, 0, 0, 0))],
        out_shape=[jax.ShapeDtypeStruct((M, D), F32),
                   jax.ShapeDtypeStruct((M // seq_len, nf, 2, FFN_HALO, tf), F32)],
        scratch_shapes=[pltpu.VMEM((tm, D), BF16), pltpu.VMEM((2, FFN_HALO + tm, tf), F32),
                        pltpu.VMEM((nf, 2, FFN_HALO, tf), F32), pltpu.VMEM((tm, tf), BF16)],
        compiler_params=_cparams(("arbitrary", "arbitrary")),
        name="ffn_fused",
    )(x, g, w_up, w_up, w_down, cw, cw, cb, cb, halo0, halo0)


def _rwkv_kernel(r_ref, k_ref, v_ref, lo_ref,
                 sr_ref, sk_ref, sv_ref, slo_ref,
                 mur_ref, muk_ref, muv_ref, mulo_ref,
                 w0_ref, a0_ref, kk_ref, ka_ref, rk_ref, lnw_ref, lnb_ref,
                 w2_ref, a2_ref, g2_ref, s0_ref,
                 o_ref, so_ref,
                 S_ref, pr_ref, pk_ref, pv_ref, plo_ref, *, t_valid, t_total):
    c = pl.program_id(2)
    Tc = r_ref.shape[1]
    PW = 2 * HEAD_DIM
    n_pairs = r_ref.shape[2] // PW

    @pl.when(c == 0)
    def _():
        S_ref[...] = s0_ref[0]
        pr_ref[...] = sr_ref[0]
        pk_ref[...] = sk_ref[0]
        pv_ref[...] = sv_ref[0]
        plo_ref[...] = slo_ref[0]

    def lerp(p_ref, prev_ref, mu_ref):
        p = p_ref[0]
        row = lax.broadcasted_iota(jnp.int32, p.shape, 0)
        shifted = jnp.where(row == 0, prev_ref[...], pltpu.roll(p, 1, 0))
        prev_ref[...] = p[Tc - 1:Tc, :]
        return p + (shifted - p) * mu_ref[...]

    xr = lerp(r_ref, pr_ref, mur_ref)
    xk = lerp(k_ref, pk_ref, muk_ref)
    xv = lerp(v_ref, pv_ref, muv_ref)
    xlo = lerp(lo_ref, plo_ref, mulo_ref)
    xw = xlo[:, :DECAY_LORA]
    xa = xlo[:, DECAY_LORA:DECAY_LORA + AAA_LORA]
    xg = xlo[:, DECAY_LORA + AAA_LORA:]

    h0p = _head0_mask((Tc, PW))

    def head_sum(x):
        parts = []
        for p in range(n_pairs):
            xp = x[:, PW * p:PW * (p + 1)]
            s_0 = jnp.sum(jnp.where(h0p, xp, 0.0), axis=-1, keepdims=True)
            s_1 = jnp.sum(jnp.where(h0p, 0.0, xp), axis=-1, keepdims=True)
            parts.append(jnp.where(h0p, s_0, s_1))
        return parts[0] if n_pairs == 1 else jnp.concatenate(parts, axis=1)

    def mm(x, y_):
        return jnp.dot(x, y_, preferred_element_type=F32)

    z = w0_ref[...] + mm(jnp.tanh(xw).astype(BF16), w2_ref[...])
    w_log = jnp.minimum(z, 0.0) - jnp.log(1.0 + jnp.exp(-jnp.abs(z))) - 0.5
    ld = -jnp.exp(w_log)
    a = _sigmoid(a0_ref[...] + mm(xa.astype(BF16), a2_ref[...]))
    g = mm(_sigmoid(xg).astype(BF16), g2_ref[...])

    kk = xk * kk_ref[...]
    kk = kk * lax.rsqrt(jnp.maximum(head_sum(kk * kk), 1e-24))
    kmod = xk * (1.0 + (a - 1.0) * ka_ref[...])
    avec = -kk
    bvec = kk * a
    bonus = head_sum(xr * kmod * rk_ref[...]) * xv
    vval = xv

    if t_valid < t_total:
        row = lax.broadcasted_iota(jnp.int32, (Tc, n_pairs * PW), 0) + c * Tc
        ok = row < t_valid
        ld = jnp.where(ok, ld, 0.0)
        avec = jnp.where(ok, avec, 0.0)
        bvec = jnp.where(ok, bvec, 0.0)
        kmod_s = jnp.where(ok, kmod, 0.0)
        vval = jnp.where(ok, vval, 0.0)
    else:
        kmod_s = kmod

    C = CHUNK
    tri_r = lax.broadcasted_iota(jnp.int32, (C, C), 0)
    tri_c = lax.broadcasted_iota(jnp.int32, (C, C), 1)
    tril = jnp.where(tri_r >= tri_c, 1.0, 0.0).astype(F32)
    it = lax.broadcasted_iota(jnp.int32, (2 * C, 2 * C), 0) % C
    js = lax.broadcasted_iota(jnp.int32, (2 * C, 2 * C), 1) % C
    strict = it > js
    incl = it >= js
    h0c = _head0_mask((C, 2 * HEAD_DIM))

    def stack(x):
        return jnp.concatenate([jnp.where(h0c, x, 0.0), jnp.where(h0c, 0.0, x)], axis=0)

    n_chunks = Tc // C
    grp = 2 if n_chunks % 2 == 0 else 1
    gw = 2 * C * grp
    eye_g = jnp.where(lax.broadcasted_iota(jnp.int32, (gw, gw), 0)
                      == lax.broadcasted_iota(jnp.int32, (gw, gw), 1), 1.0, 0.0).astype(F32)
    zero_blk = jnp.zeros((2 * C, 2 * C), F32)

    def bdiag(blocks):
        if len(blocks) == 1:
            return blocks[0]
        return jnp.concatenate([jnp.concatenate([blocks[0], zero_blk], axis=1),
                                jnp.concatenate([zero_blk, blocks[1]], axis=1)], axis=0)

    def cat(blocks):
        return blocks[0] if len(blocks) == 1 else jnp.concatenate(blocks, axis=0)

    a16, r32, bb16, kb16, v16, gts, Ls, AKs, RBs, RKs = ({} for _ in range(10))
    units = [(p, j) for j in range(n_chunks) for p in range(n_pairs)]
    for p, j in units:
        sl = (slice(C * j, C * (j + 1)), slice(PW * p, PW * (p + 1)))
        ldc = ld[sl]
        cs = jnp.dot(tril, ldc, precision=lax.Precision.HIGHEST, preferred_element_type=F32)
        tot = cs[C - 1:C, :]
        eg = jnp.exp(cs)
        egi = jnp.exp(-cs)
        ege = jnp.exp(cs - ldc)
        et = jnp.exp(tot - cs)
        gts[p, j] = jnp.exp(tot)
        a_s = stack(avec[sl] * ege).astype(BF16)
        r_s = stack(xr[sl] * eg)
        b_s = stack(bvec[sl] * egi).astype(BF16)
        k_s = stack(kmod_s[sl] * egi).astype(BF16)
        sc = lax.dot_general(jnp.concatenate([a_s, r_s.astype(BF16)], axis=0), jnp.concatenate([b_s, k_s], axis=0), NT,
                             preferred_element_type=F32)
        a16[p, j] = a_s
        r32[p, j] = r_s
        bb16[p, j] = stack(bvec[sl] * et).astype(BF16)
        kb16[p, j] = stack(kmod_s[sl] * et).astype(BF16)
        v16[p, j] = stack(vval[sl]).astype(BF16)
        Ls[p, j] = jnp.where(strict, sc[:2 * C, :2 * C], 0.0)
        AKs[p, j] = jnp.where(strict, sc[:2 * C, 2 * C:], 0.0)
        RBs[p, j] = jnp.where(incl, sc[2 * C:, :2 * C], 0.0)
        RKs[p, j] = jnp.where(incl, sc[2 * C:, 2 * C:], 0.0)

    groups = [[(p, j) for j in range(g0, g0 + grp)] for g0 in range(0, n_chunks, grp) for p in range(n_pairs)]
    Ps = [bdiag([Ls[c_] for c_ in g_]) for g_ in groups]
    Tms = [eye_g + p_ for p_ in Ps]
    for _ in range(C.bit_length() - 2):
        for gi in range(len(groups)):
            P16 = Ps[gi].astype(BF16)
            Ps[gi] = mm(P16, P16)
            Tms[gi] = Tms[gi] + mm(Tms[gi].astype(BF16), Ps[gi].astype(BF16))

    RAs, YNs, Gs, Ns = {}, {}, {}, {}
    for gi, g_ in enumerate(groups):
        T16 = Tms[gi].astype(BF16)
        A2 = cat([a16[c_] for c_ in g_])
        V2 = cat([v16[c_] for c_ in g_])
        R2 = cat([r32[c_] for c_ in g_])
        RB16 = bdiag([RBs[c_] for c_ in g_]).astype(BF16)
        RK16 = bdiag([RKs[c_] for c_ in g_]).astype(BF16)
        TA16 = mm(T16, A2).astype(BF16)
        TV16 = mm(T16, mm(bdiag([AKs[c_] for c_ in g_]).astype(BF16), V2).astype(BF16)).astype(BF16)
        RA = R2 + mm(RB16, TA16)
        YN = mm(jnp.concatenate([RB16, RK16], axis=1), jnp.concatenate([TV16, V2], axis=0))
        for q, c_ in enumerate(g_):
            rs = slice(2 * C * q, 2 * C * (q + 1))
            RAs[c_] = RA[rs].astype(BF16)
            YNs[c_] = YN[rs]
            Gs[c_] = lax.dot_general(TA16[rs], bb16[c_], TN, preferred_element_type=F32).astype(BF16)
            Ns[c_] = lax.dot_general(jnp.concatenate([TV16[rs], v16[c_]], axis=0),
                                     jnp.concatenate([bb16[c_], kb16[c_]], axis=0), TN, preferred_element_type=F32)

    ys = {}
    Ss = [S_ref[p] for p in range(n_pairs)]
    for p, j in units:
        S16 = Ss[p].astype(BF16)
        Y = lax.dot_general(RAs[p, j], S16, NT, preferred_element_type=F32) + YNs[p, j]
        ys[p, j] = Y[:C] + Y[C:]
        Ss[p] = Ss[p] * gts[p, j] + mm(S16, Gs[p, j]) + Ns[p, j]
    for p in range(n_pairs):
        S_ref[p] = Ss[p]
    y_pairs = [cat([ys[p, j] for j in range(n_chunks)]) for p in range(n_pairs)]
    y = y_pairs[0] if n_pairs == 1 else jnp.concatenate(y_pairs, axis=1)

    mu = head_sum(y) * (1.0 / HEAD_DIM)
    var = head_sum(jnp.square(y - mu)) * (1.0 / HEAD_DIM)
    yn = (y - mu) * lax.rsqrt(var + GN_EPS) * lnw_ref[...] + lnb_ref[...]
    o_ref[0] = ((yn + bonus) * g).astype(BF16)

    @pl.when(c == pl.num_programs(2) - 1)
    def _():
        for p in range(n_pairs):
            so_ref[0, p] = Ss[p]


def rwkv_time_mix(proj3, shift0, s0, prm, t_valid, tc, out_width=MIX, pairs=2):
    N, T, _ = proj3.shape
    tc = min(tc, T)
    W = 2 * HEAD_DIM * pairs
    assert T % tc == 0 and tc % CHUNK == 0 and MIX % W == 0 and C_RWKV % W == 0
    rb = C_RWKV // W
    nb = MIX // W
    pspec = lambda w, off: pl.BlockSpec((1, tc, w), lambda n, h, c: (n, c, off(h)))
    sspec = lambda w, off: pl.BlockSpec((1, 1, w), lambda n, h, c: (n, 0, off(h)))
    mspec = lambda w, off: pl.BlockSpec((1, w), lambda n, h, c: (0, off(h)))
    hspec = pl.BlockSpec((1, W), lambda n, h, c: (0, h))
    assert (C_RWKV + 3 * MIX) % LORA_IN == 0
    lo_p = (C_RWKV + 3 * MIX) // LORA_IN
    lo_s = 3 * MIX // LORA_IN
    in_specs = [
        pspec(W, lambda h: rb + h), pspec(W, lambda h: rb + nb + h), pspec(W, lambda h: rb + 2 * nb + h),
        pspec(LORA_IN, lambda h: lo_p),
        sspec(W, lambda h: h), sspec(W, lambda h: nb + h), sspec(W, lambda h: 2 * nb + h),
        sspec(LORA_IN, lambda h: lo_s),
        mspec(W, lambda h: h), mspec(W, lambda h: nb + h), mspec(W, lambda h: 2 * nb + h),
        mspec(LORA_IN, lambda h: lo_s),
        hspec, hspec, hspec, hspec, hspec, hspec, hspec,
        pl.BlockSpec((DECAY_LORA, W), lambda n, h, c: (0, h)),
        pl.BlockSpec((AAA_LORA, W), lambda n, h, c: (0, h)),
        pl.BlockSpec((GATE_PAD, W), lambda n, h, c: (0, h)),
        pl.BlockSpec((1, pairs, 128, 128), lambda n, h, c: (n, h, 0, 0)),
    ]
    args = ([proj3] * 4 + [shift0] * 4 + [prm['mu']] * 4
            + [prm['w0'], prm['a0'], prm['k_k'], prm['k_a'], prm['r_k'], prm['ln_w'], prm['ln_b'],
               prm['w2'], prm['a2'], prm['g2'], s0])
    return pl.pallas_call(
        functools.partial(_rwkv_kernel, t_valid=t_valid, t_total=T),
        grid=(N, nb, T // tc),
        in_specs=in_specs,
        out_specs=[pl.BlockSpec((1, tc, W), lambda n, h, c: (n, c, M_RWKV // W + h)),
                   pl.BlockSpec((1, pairs, 128, 128), lambda n, h, c: (n, h, 0, 0))],
        out_shape=[jax.ShapeDtypeStruct((N, T, out_width), BF16),
                   jax.ShapeDtypeStruct((N, HEADS // 2, 128, 128), F32)],
        scratch_shapes=[pltpu.VMEM((pairs, 128, 128), F32), pltpu.VMEM((1, W), F32), pltpu.VMEM((1, W), F32),
                        pltpu.VMEM((1, W), F32), pltpu.VMEM((1, LORA_IN), F32)],
        compiler_params=_cparams(("parallel", "parallel", "arbitrary")),
        name="rwkv7_time_mix",
    )(*args)


def _pad_rwkv_cols(t):
    return jnp.pad(t, [(0, 0)] * (t.ndim - 1) + [(0, RWKV_PAD - RWKV_PROJ)])


PREP_ROWS = 512
PREP_COLS = 1024


def _prep_w_in_kernel(cur_ref, prev_ref, o_ref):
    j = pl.program_id(2)
    cur = cur_ref[...].astype(F32)
    prev = prev_ref[...].astype(F32)
    lane = lax.broadcasted_iota(jnp.int32, cur.shape, 1)
    col = lane + j * PREP_COLS
    shift = RWKV_PAD - RWKV_PROJ
    shifted = jnp.where(lane < shift, pltpu.roll(prev, shift, 1), pltpu.roll(cur, shift, 1))
    in_tail = (col >= RWKV_PAD) & (col < C_CONV + CONV_PROJ)
    o_ref[...] = jnp.where(col < RWKV_PROJ, cur, jnp.where(in_tail, shifted, 0.0)).astype(BF16)


def prep_w_in(w16):
    depth, K, n = w16.shape
    assert K % PREP_ROWS == 0 and NP % PREP_COLS == 0 and pl.cdiv(n, PREP_COLS) == NP // PREP_COLS
    assert n + RWKV_PAD - RWKV_PROJ == C_CONV + CONV_PROJ
    blk = (None, PREP_ROWS, PREP_COLS)
    return pl.pallas_call(
        _prep_w_in_kernel,
        grid=(depth, K // PREP_ROWS, NP // PREP_COLS),
        in_specs=[pl.BlockSpec(blk, lambda l, i, j: (l, i, j)),
                  pl.BlockSpec(blk, lambda l, i, j: (l, i, jnp.maximum(j - 1, 0)))],
        out_specs=pl.BlockSpec(blk, lambda l, i, j: (l, i, j)),
        out_shape=jax.ShapeDtypeStruct((depth, K, NP), BF16),
        compiler_params=_cparams(("parallel", "parallel", "parallel")),
        name="prep_w_in",
    )(w16, w16)


def _matmul_weights(w_in, w_out, w_up, w_down):
    return {'w_in': prep_w_in(w_in.astype(BF16)), 'w_out': w_out.astype(BF16), 'w_up': w_up.astype(BF16), 'w_down': w_down.astype(BF16)}


def _layer_params(l, big, norm_mix, rwkv_mu, rwkv_w0, rwkv_w2, rwkv_a0, rwkv_a2, rwkv_g2, rwkv_k_k, rwkv_k_a,
                  rwkv_r_k, rwkv_ln_w, rwkv_ln_b, conv_w, conv_b, conv_ln_w, conv_ln_b, norm_ffn,
                  ffn_conv_w, ffn_conv_b):
    row = lambda t: t.reshape(1, -1)
    return {
        'w_in': big['w_in'], 'w_out': big['w_out'], 'w_up': big['w_up'], 'w_down': big['w_down'],
        'norm_mix': row(norm_mix[l]),
        'mu': _pad_rwkv_cols(row(rwkv_mu[l])),
        'w0': row(rwkv_w0[l]), 'a0': row(rwkv_a0[l]), 'k_k': row(rwkv_k_k[l]), 'k_a': row(rwkv_k_a[l]),
        'r_k': row(rwkv_r_k[l]), 'ln_w': row(rwkv_ln_w[l]), 'ln_b': row(rwkv_ln_b[l]),
        'w2': rwkv_w2[l].astype(BF16), 'a2': rwkv_a2[l].astype(BF16),
        'g2': jnp.pad(rwkv_g2[l], ((0, GATE_PAD - GATE_LORA), (0, 0))).astype(BF16),
        'conv_w': jnp.pad(conv_w[l], ((0, 32 - CONV_WIDTH), (0, 0))), 'conv_b': row(conv_b[l]),
        'conv_ln_w': row(conv_ln_w[l]), 'conv_ln_b': row(conv_ln_b[l]),
        'norm_ffn': row(norm_ffn[l]),
        'ffn_conv_w': jnp.pad(ffn_conv_w[l], ((0, 8 - FFN_CONV_WIDTH), (0, 0))), 'ffn_conv_b': row(ffn_conv_b[l]),
    }


def _rope_tables(pos):
    half = HEAD_DIM // 2
    inv_freq = ROPE_THETA ** (-jnp.arange(half, dtype=F32) * 2.0 / HEAD_DIM)
    ang = pos.astype(F32)[..., None] * inv_freq
    cos, sin = jnp.cos(ang), jnp.sin(ang)
    return jnp.concatenate([cos, cos, cos, cos], axis=-1), jnp.concatenate([-sin, sin, -sin, sin], axis=-1)


def _state_to_blockdiag(s):
    N = s.shape[0]
    s = s.reshape(N, HEADS // 2, 2, HEAD_DIM, HEAD_DIM)
    z = jnp.zeros_like(s[:, :, 0])
    top = jnp.concatenate([s[:, :, 0], z], axis=-1)
    bot = jnp.concatenate([z, s[:, :, 1]], axis=-1)
    return jnp.concatenate([top, bot], axis=-2)


def _blockdiag_to_state(s):
    N = s.shape[0]
    return jnp.stack([s[:, :, :HEAD_DIM, :HEAD_DIM], s[:, :, HEAD_DIM:, HEAD_DIM:]], axis=2).reshape(
        N, HEADS, HEAD_DIM, HEAD_DIM)


def _last_rows(buf, u, t_valid):
    keep = buf.shape[1]
    if t_valid >= keep:
        return u[:, t_valid - keep:t_valid]
    return jnp.concatenate([buf[:, t_valid:], u[:, :t_valid]], axis=1)


def _front_pad(buf, rows):
    return jnp.pad(buf, ((0, 0), (rows - buf.shape[1], 0), (0, 0)))


def _prompt_layer(x2, N, S, prm, cos_t, sin_t, layer, depth, kv_all):
    f32 = x2.dtype
    proj3 = norm_matmul(x2, prm['norm_mix'], prm['w_in'], layer, 1024, 512, single_buffer_x=True).reshape(N, S, NP)

    mixed, s_new = rwkv_time_mix(proj3, jnp.zeros((N, 1, RWKV_PAD), f32), jnp.zeros((N, HEADS // 2, 128, 128), f32),
                                 prm, S, 256, out_width=D_MODEL, pairs=4)
    shift_new = proj3[:, S - 1:S, C_RWKV:C_RWKV + RWKV_PROJ]

    mixed, k_all, v_all = prompt_attention(proj3, cos_t, sin_t, mixed, layer, depth, kv_all)

    u = glu(proj3, 1024)
    mixed = dwconv_ln_silu(u, jnp.zeros((N, CONV_HALO, CONV_DIM), f32), prm['conv_w'], prm['conv_b'],
                           prm['conv_ln_w'], prm['conv_ln_b'], 128, mixed=mixed)
    conv_new = u[:, S - (CONV_WIDTH - 1):]

    x2 = matmul_res(mixed.reshape(N * S, D_MODEL), prm['w_out'], layer, x2, 512, 512, D_MODEL)

    x2, tails = ffn_fused(x2, prm['norm_ffn'], prm['w_up'], prm['w_down'], layer, prm['ffn_conv_w'], prm['ffn_conv_b'],
                          jnp.zeros((N, FFN_HALO, 2 * D_FF), f32), S, 512, 256)
    keep = FFN_CONV_WIDTH - 1
    ffn_new = jnp.transpose(tails[:, :, :, FFN_HALO - keep:, :], (0, 3, 2, 1, 4)).reshape(N, keep, 2 * D_FF)
    return x2, (shift_new, _blockdiag_to_state(s_new), conv_new, ffn_new), (k_all, v_all)


def _sample_layer(x2, N, T, t_valid, prm, cos_t, sin_t, carry, attn_bufs, layer, depth, kv_all):
    shift0, wkv0, conv_buf, ffn_buf = carry
    proj3 = norm_matmul(x2, prm['norm_mix'], prm['w_in'], layer, 512, 512).reshape(N, T, NP)

    proj_r = jnp.pad(proj3, ((0, 0), (0, CHUNK - T), (0, 0)))
    o_rwkv, s_new = rwkv_time_mix(proj_r, _pad_rwkv_cols(shift0), _state_to_blockdiag(wkv0), prm, t_valid, CHUNK)
    o_rwkv = o_rwkv[:, :T]
    shift_new = proj3[:, t_valid - 1:t_valid, C_RWKV:C_RWKV + RWKV_PROJ]

    q_rot, k_rot = rope(proj3, cos_t, sin_t, 256)
    v_new = proj3[:, :, C_ATT + 2 * MIX:C_ATT + 3 * MIX]
    padr = ((0, 0), (0, 128 - T), (0, 0))
    o_att, k_all, v_all = sample_attention(q_rot, jnp.pad(k_rot, padr), jnp.pad(v_new, padr),
                                           attn_bufs[0], attn_bufs[1], t_valid, layer, depth, kv_all)

    u = glu(proj3, 256)
    o_conv = dwconv_ln_silu(u, _front_pad(conv_buf, CONV_HALO), prm['conv_w'], prm['conv_b'],
                            prm['conv_ln_w'], prm['conv_ln_b'], 128)
    conv_new = _last_rows(conv_buf, u, t_valid)

    mixed = jnp.concatenate([o_rwkv, o_att, o_conv], axis=-1).reshape(N * T, D_MODEL)
    x2 = matmul_res(mixed, prm['w_out'], layer, x2, 512, 512, D_MODEL)

    uf3 = norm_matmul(x2, prm['norm_ffn'], prm['w_up'], layer, 512, 512).reshape(N, T, 2 * D_FF)
    act = ffn_act(uf3, _front_pad(ffn_buf, FFN_HALO), prm['ffn_conv_w'], prm['ffn_conv_b'], 512, D_FF // 2)
    ffn_new = _last_rows(ffn_buf, uf3, t_valid)
    x2 = matmul_res(act.reshape(N * T, D_FF), prm['w_down'], layer, x2, 512, 512, D_FF // 2)
    return x2, (shift_new, _blockdiag_to_state(s_new), conv_new, ffn_new), (k_all, v_all)


def kernel(x_prompt, x_sample, state_rwkv_shift, state_rwkv_wkv, state_attn_k, state_attn_v, state_conv, state_ffn_conv, pos_sample, norm_mix, w_in, rwkv_mu, rwkv_w0, rwkv_w2, rwkv_a0, rwkv_a2, rwkv_g2, rwkv_k_k, rwkv_k_a, rwkv_r_k, rwkv_ln_w, rwkv_ln_b, conv_w, conv_b, conv_ln_w, conv_ln_b, w_out, norm_ffn, w_up, ffn_conv_w, ffn_conv_b, w_down, norm_final):
    B, S, _ = x_prompt.shape
    NB, TS, _ = x_sample.shape
    depth = w_in.shape[0]
    win_buf = state_attn_k.shape[2]
    assert win_buf == S, "prompt key/value state is the whole rotated sequence"
    TSP = 8

    xp = x_prompt.reshape(B * S, D_MODEL)
    xs = jnp.pad(x_sample, ((0, 0), (0, TSP - TS), (0, 0))).reshape(NB * TSP, D_MODEL)
    cos_p, sin_p = _rope_tables(jnp.arange(S, dtype=jnp.int32)[None])
    cos_s, sin_s = _rope_tables(jnp.pad(pos_sample, ((0, 0), (0, TSP - TS))))
    new_p, new_s = [], []
    kv_p, kv_s = None, None
    big = _matmul_weights(w_in, w_out, w_up, w_down)
    bufs = (state_attn_k.reshape(depth, NB, win_buf, MIX), state_attn_v.reshape(depth, NB, win_buf, MIX))
    for l in range(depth):
        prm = _layer_params(l, big, norm_mix, rwkv_mu, rwkv_w0, rwkv_w2, rwkv_a0, rwkv_a2, rwkv_g2, rwkv_k_k,
                            rwkv_k_a, rwkv_r_k, rwkv_ln_w, rwkv_ln_b, conv_w, conv_b, conv_ln_w, conv_ln_b,
                            norm_ffn, ffn_conv_w, ffn_conv_b)
        xp, st_p, kv_p = _prompt_layer(xp, B, S, prm, cos_p, sin_p, l, depth, kv_p)
        carry_s = (state_rwkv_shift[l], state_rwkv_wkv[l], state_conv[l], state_ffn_conv[l])
        xs, st_s, kv_s = _sample_layer(xs, NB, TSP, TS, prm, cos_s, sin_s, carry_s, bufs, l, depth, kv_s)
        new_p.append(st_p)
        new_s.append(st_s)

    g = norm_final.reshape(1, D_MODEL)
    y_prompt = rmsnorm(xp, g, 512).reshape(B, S, D_MODEL)
    y_sample = rmsnorm(xs, g, 512).reshape(NB, TSP, D_MODEL)[:, :TS]

    def stack(states, i):
        return jnp.stack([st[i] for st in states], axis=0)

    heads_p = (depth, B, win_buf, HEADS, HEAD_DIM)
    heads_s = (depth, NB, win_buf, HEADS, HEAD_DIM)
    return (y_prompt, y_sample,
            stack(new_p, 0), stack(new_s, 0), stack(new_p, 1), stack(new_s, 1),
            kv_p[0].reshape(heads_p), kv_s[0].reshape(heads_s), kv_p[1].reshape(heads_p), kv_s[1].reshape(heads_s),
            stack(new_p, 2), stack(new_s, 2), stack(new_p, 3), stack(new_s, 3))
```

```python
import functools

import jax
import jax.numpy as jnp
from jax import lax
from jax.experimental import pallas as pl
from jax.experimental.pallas import tpu as pltpu

F32 = jnp.float32
BF16 = jnp.bfloat16

D_MODEL = 4096
HEAD_DIM = 64
HEADS = 24
MIX = HEADS * HEAD_DIM
CONV_DIM = 1024
DECAY_LORA = 128
AAA_LORA = 128
GATE_LORA = 480
GATE_PAD = 512
RWKV_PROJ = 3 * MIX + DECAY_LORA + AAA_LORA + GATE_LORA
RWKV_PAD = 3 * MIX + DECAY_LORA + AAA_LORA + GATE_PAD
ATT_PROJ = 3 * MIX
CONV_PROJ = 2 * CONV_DIM
CONV_WIDTH = 31
CONV_HALO = 32
D_FF = 11008
FFN_CONV_WIDTH = 3
FFN_HALO = 8
BAND = 128
DILATIONS = (1, 4, 16)
WINDOWS = (128, 512, 2048)
ROPE_THETA = 10000.0
RMS_EPS = 1e-6
LN_EPS = 1e-5
GN_EPS = 64e-5
NEG_INF = -1e30
CHUNK = 64

C_RWKV = 0
C_ATT = RWKV_PAD
C_CONV = C_ATT + ATT_PROJ
NP = 12288
LORA_IN = DECAY_LORA + AAA_LORA + GATE_PAD
M_RWKV, M_ATT, M_CONV = 0, MIX, 2 * MIX

VMEM_LIMIT = 56 * 1024 * 1024

NT = (((1,), (1,)), ((), ()))
TN = (((0,), (0,)), ((), ()))


def _cparams(sem):
    return pltpu.CompilerParams(dimension_semantics=sem, vmem_limit_bytes=VMEM_LIMIT)


def _sigmoid(x):
    return 1.0 / (1.0 + jnp.exp(-x))


def _head0_mask(shape):
    return lax.broadcasted_iota(jnp.int32, shape, 1) % (2 * HEAD_DIM) < HEAD_DIM


def _norm_matmul_kernel(x_ref, g_ref, w_ref, o_ref, h_ref):
    @pl.when(pl.program_id(1) == 0)
    def _():
        x = x_ref[...]
        ms = jnp.mean(x * x, axis=-1, keepdims=True)
        h_ref[...] = (x * lax.rsqrt(ms + RMS_EPS) * g_ref[...]).astype(BF16)

    o_ref[...] = jnp.dot(h_ref[...], w_ref[...], preferred_element_type=F32)


def norm_matmul(x, g, w, layer, tm, tn, single_buffer_x=False):
    M, K = x.shape
    N = w.shape[2]
    tm = min(tm, M)
    assert M % tm == 0 and N % tn == 0
    x_mode = dict(pipeline_mode=pl.Buffered(1)) if single_buffer_x else {}
    return pl.pallas_call(
        _norm_matmul_kernel,
        grid=(M // tm, N // tn),
        in_specs=[pl.BlockSpec((tm, K), lambda i, j: (i, 0), **x_mode),
                  pl.BlockSpec((1, K), lambda i, j: (0, 0)),
                  pl.BlockSpec((None, K, tn), lambda i, j: (layer, 0, j))],
        out_specs=pl.BlockSpec((tm, tn), lambda i, j: (i, j)),
        out_shape=jax.ShapeDtypeStruct((M, N), F32),
        scratch_shapes=[pltpu.VMEM((tm, K), BF16)],
        compiler_params=_cparams(("parallel", "arbitrary")),
        name="norm_matmul",
    )(x, g, w)


def _matmul_res_kernel(a_ref, w_ref, r_ref, o_ref, acc_ref, *, nk):
    k = pl.program_id(2)

    @pl.when(k == 0)
    def _():
        acc_ref[...] = r_ref[...]

    acc_ref[...] += jnp.dot(a_ref[...], w_ref[...], preferred_element_type=F32)

    @pl.when(k == nk - 1)
    def _():
        o_ref[...] = acc_ref[...]


def matmul_res(a, w, layer, res, tm, tn, tk):
    M, K = a.shape
    N = w.shape[2]
    tm = min(tm, M)
    assert M % tm == 0 and N % tn == 0 and K % tk == 0
    nk = K // tk
    return pl.pallas_call(
        functools.partial(_matmul_res_kernel, nk=nk),
        grid=(M // tm, N // tn, nk),
        in_specs=[pl.BlockSpec((tm, tk), lambda i, j, k: (i, k)),
                  pl.BlockSpec((None, tk, tn), lambda i, j, k: (layer, k, j)),
                  pl.BlockSpec((tm, tn), lambda i, j, k: (i, j))],
        out_specs=pl.BlockSpec((tm, tn), lambda i, j, k: (i, j)),
        out_shape=jax.ShapeDtypeStruct((M, N), F32),
        scratch_shapes=[pltpu.VMEM((tm, tn), F32)],
        compiler_params=_cparams(("parallel", "parallel", "arbitrary")),
        name="matmul_res",
    )(a, w, res)


def _rmsnorm_kernel(x_ref, g_ref, o_ref):
    x = x_ref[...]
    ms = jnp.mean(x * x, axis=-1, keepdims=True)
    o_ref[...] = x * lax.rsqrt(ms + RMS_EPS) * g_ref[...]


def rmsnorm(x, g, tm):
    M, K = x.shape
    tm = min(tm, M)
    return pl.pallas_call(
        _rmsnorm_kernel,
        grid=(M // tm,),
        in_specs=[pl.BlockSpec((tm, K), lambda i: (i, 0)), pl.BlockSpec((1, K), lambda i: (0, 0))],
        out_specs=pl.BlockSpec((tm, K), lambda i: (i, 0)),
        out_shape=jax.ShapeDtypeStruct((M, K), F32),
        compiler_params=_cparams(("parallel",)),
        name="rmsnorm",
    )(x, g)


def _rot_half(x, first_half):
    w = x.shape[1]
    return jnp.where(first_half, pltpu.roll(x, w - HEAD_DIM // 2, 1), pltpu.roll(x, HEAD_DIM // 2, 1))


ROPE_BLOCK = 256


def _rope_kernel(q_ref, k_ref, cos_ref, sin_ref, qo_ref, ko_ref):
    reps = ROPE_BLOCK // (2 * HEAD_DIM)
    cos = jnp.concatenate([cos_ref[0]] * reps, axis=1)
    sin = jnp.concatenate([sin_ref[0]] * reps, axis=1)
    lane = lax.broadcasted_iota(jnp.int32, cos.shape, 1)
    first_half = (lane % HEAD_DIM) < (HEAD_DIM // 2)
    q = q_ref[0]
    k = k_ref[0]
    qo_ref[0] = q * cos + _rot_half(q, first_half) * sin
    ko_ref[0] = k * cos + _rot_half(k, first_half) * sin


def rope(proj3, cos_t, sin_t, tt):
    N, T, _ = proj3.shape
    tt = min(tt, T)
    tab = lambda n, i, j: (n, i, 0)
    rw = ROPE_BLOCK
    assert C_ATT % rw == 0 and MIX % rw == 0
    qb = C_ATT // rw
    kb = (C_ATT + MIX) // rw
    out = jax.ShapeDtypeStruct((N, T, MIX), F32)
    return pl.pallas_call(
        _rope_kernel,
        grid=(N, T // tt, MIX // rw),
        in_specs=[pl.BlockSpec((1, tt, rw), lambda n, i, j: (n, i, qb + j)),
                  pl.BlockSpec((1, tt, rw), lambda n, i, j: (n, i, kb + j)),
                  pl.BlockSpec((1, tt, 128), tab),
                  pl.BlockSpec((1, tt, 128), tab)],
        out_specs=[pl.BlockSpec((1, tt, rw), lambda n, i, j: (n, i, j)),
                   pl.BlockSpec((1, tt, rw), lambda n, i, j: (n, i, j))],
        out_shape=[out, out],
        compiler_params=_cparams(("parallel", "parallel", "parallel")),
        name="rope",
    )(proj3, proj3, cos_t, sin_t)


def _prompt_attn_kernel(*refs):
    q_ref, k_ref, v_ref, cos_ref, sin_ref = refs[:5]
    o_ref, ko_all_ref, vo_all_ref, qs_ref, acc_ref, m_ref, l_ref = refs[-7:]
    ko_ref = ko_all_ref.at[0]
    vo_ref = vo_all_ref.at[0]
    S = q_ref.shape[1]
    B = BAND
    P = 2 * HEAD_DIM
    lane = lax.broadcasted_iota(jnp.int32, (B, P), 1)
    first_half = (lane % HEAD_DIM) < (HEAD_DIM // 2)
    h0 = _head0_mask((B, P))
    scale = HEAD_DIM ** -0.5

    def rope_rows(i, carry):
        for u in range(2):
            rows = pl.ds(pl.multiple_of((2 * i + u) * B, B), B)
            cos = cos_ref[0, rows, :]
            sin = sin_ref[0, rows, :]
            q = q_ref[0, rows, :]
            k = k_ref[0, rows, :]
            qs_ref[rows, :] = q * cos + _rot_half(q, first_half) * sin
            ko_ref[0, rows, :] = k * cos + _rot_half(k, first_half) * sin
            vo_ref[0, rows, :] = v_ref[0, rows, :]
        return carry

    lax.fori_loop(0, S // (2 * B), rope_rows, 0)

    def attend(q, kk, vv, valid):
        ms, ls, os_ = [], [], []
        for h in range(2):
            qh = jnp.where(h0 if h == 0 else ~h0, q, 0.0).astype(BF16)
            s = lax.dot_general(qh, kk, NT, preferred_element_type=F32) * scale
            s = jnp.where(valid, s, NEG_INF)
            m = jnp.max(s, axis=-1, keepdims=True)
            e = jnp.exp(s - m)
            ms.append(m)
            ls.append(jnp.sum(e, axis=-1, keepdims=True))
            os_.append(jnp.dot(e.astype(BF16), vv, preferred_element_type=F32))
        return jnp.where(h0, ms[0], ms[1]), jnp.where(h0, ls[0], ls[1]), jnp.where(h0, os_[0], os_[1])

    def merge(rows, m_c, l_c, o_c):
        m_p = m_ref[rows, :]
        m_n = jnp.maximum(m_p, m_c)
        a_p = jnp.exp(m_p - m_n)
        a_c = jnp.exp(m_c - m_n)
        acc_ref[rows, :] = acc_ref[rows, :] * a_p + o_c * a_c
        l_ref[rows, :] = l_ref[rows, :] * a_p + l_c * a_c
        m_ref[rows, :] = m_n

    qi2 = lax.broadcasted_iota(jnp.int32, (B, 2 * B), 0)
    ki2 = lax.broadcasted_iota(jnp.int32, (B, 2 * B), 1)
    dist2 = qi2 + B - ki2
    band2 = (dist2 >= 0) & (dist2 <= B)
    qi1 = lax.broadcasted_iota(jnp.int32, (B, B), 0)
    ki1 = lax.broadcasted_iota(jnp.int32, (B, B), 1)
    causal1 = qi1 >= ki1

    def two_block_keys(cur, prev):
        kk = jnp.concatenate([ko_ref[0, prev, :], ko_ref[0, cur, :]], axis=0).astype(BF16)
        vv = jnp.concatenate([v_ref[0, prev, :], v_ref[0, cur, :]], axis=0).astype(BF16)
        return kk, vv

    d16 = DILATIONS[2]
    assert S == B * d16
    per_iter = 8

    def dil16(it, carry):
        for u in range(per_iter):
            cur = pl.ds(it * per_iter + u, B, stride=d16)
            kk, vv = two_block_keys(cur, cur)
            m_c, l_c, o_c = attend(qs_ref[cur, :], kk, vv, band2 & (ki2 >= B))
            acc_ref[cur, :] = o_c
            m_ref[cur, :] = m_c
            l_ref[cur, :] = l_c
        return carry

    lax.fori_loop(0, d16 // per_iter, dil16, 0)

    d4 = DILATIONS[1]

    def dil4(ib, carry):
        base = pl.multiple_of(ib * (B * d4), B * d4)
        pbase = pl.multiple_of(jnp.maximum(ib - 1, 0) * (B * d4), B * d4)
        for r in range(d4):
            cur = pl.ds(base + r, B, stride=d4)
            prev = pl.ds(pbase + r, B, stride=d4)
            kk, vv = two_block_keys(cur, prev)
            m_c, l_c, o_c = attend(qs_ref[cur, :], kk, vv, band2 & ((ki2 >= B) | (ib > 0)))
            merge(cur, m_c, l_c, o_c)
        return carry

    lax.fori_loop(0, S // (B * d4), dil4, 0)

    unroll = 4

    def dil1(it, carry):
        for u in range(unroll):
            ib = it * unroll + u
            cur = pl.ds(pl.multiple_of(ib * B, B), B)
            prev = pl.ds(pl.multiple_of(jnp.maximum(ib - 1, 0) * B, B), B)
            kk, vv = two_block_keys(cur, prev)
            m_c, l_c, o_c = attend(qs_ref[cur, :], kk, vv, band2 & ((ki2 >= B) | (ib > 0)))
            m_p = m_ref[cur, :]
            m_n = jnp.maximum(m_p, m_c)
            a_p = jnp.exp(m_p - m_n)
            a_c = jnp.exp(m_c - m_n)
            acc = acc_ref[cur, :] * a_p + o_c * a_c
            l_n = l_ref[cur, :] * a_p + l_c * a_c
            o_ref[0, cur, :] = (acc / l_n).astype(BF16)
        return carry

    lax.fori_loop(0, S // (B * unroll), dil1, 0)


def prompt_attention(proj3, cos_t, sin_t, mixed, layer, depth, kv_all=None):
    N, S, _ = proj3.shape
    qb, kb, vb = C_ATT // 128, (C_ATT + MIX) // 128, (C_ATT + 2 * MIX) // 128
    col = lambda b: pl.BlockSpec((1, S, 128), lambda n, h: (n, 0, b + h))
    tab = pl.BlockSpec((1, S, 128), lambda n, h: (0, 0, 0))
    anyspec = pl.BlockSpec(memory_space=pl.ANY)
    kv = jax.ShapeDtypeStruct((depth, N, S, MIX), F32)
    kv_spec = pl.BlockSpec((1, 1, S, 128), lambda n, h: (layer, n, 0, h))
    args = [proj3, proj3, proj3, cos_t, sin_t, mixed]
    in_specs = [col(qb), col(kb), col(vb), tab, tab, anyspec]
    aliases = {5: 0}
    if kv_all is not None:
        args += list(kv_all)
        in_specs += [anyspec, anyspec]
        aliases.update({6: 1, 7: 2})
    return pl.pallas_call(
        _prompt_attn_kernel,
        grid=(N, HEADS // 2),
        in_specs=in_specs,
        out_specs=[col(M_ATT // 128), kv_spec, kv_spec],
        out_shape=[jax.ShapeDtypeStruct(mixed.shape, mixed.dtype), kv, kv],
        scratch_shapes=[pltpu.VMEM((S, 128), F32)] * 4,
        input_output_aliases=aliases,
        compiler_params=_cparams(("parallel", "parallel")),
        name="prompt_attention",
    )(*args)


def _sample_attn_kernel(*refs, t_valid):
    q_ref, kn_ref, vn_ref, kb_ref, vb_ref = refs[:5]
    o_ref, ks_all_ref, vs_all_ref = refs[-3:]
    ks_ref = ks_all_ref.at[0]
    vs_ref = vs_all_ref.at[0]
    TQ = q_ref.shape[1]
    M = kb_ref.shape[1]
    TN_ = kn_ref.shape[1]
    h0 = _head0_mask((TQ, 2 * HEAD_DIM))
    scale = HEAD_DIM ** -0.5

    def counts(delta, in_range):
        c = jnp.zeros(delta.shape, F32)
        for win, dil in zip(WINDOWS, DILATIONS):
            c = c + jnp.where((delta % dil == 0) & (delta <= win) & in_range, 1.0, 0.0)
        return c

    tq = lax.broadcasted_iota(jnp.int32, (TQ, M), 0)
    kb_i = lax.broadcasted_iota(jnp.int32, (TQ, M), 1)
    d_buf = M + tq - kb_i
    c_buf = counts(d_buf, d_buf >= 0)
    tq2 = lax.broadcasted_iota(jnp.int32, (TQ, TN_), 0)
    tn2 = lax.broadcasted_iota(jnp.int32, (TQ, TN_), 1)
    d_new = tq2 - tn2
    c_new = counts(d_new, (d_new >= 0) & (tn2 < t_valid))

    row8 = lax.broadcasted_iota(jnp.int32, (8, 2 * HEAD_DIM), 0)
    for p in range(4):
        sl = slice(128 * p, 128 * (p + 1))
        qp = q_ref[0, :, sl]
        kb = kb_ref[0, :, sl]
        vb = vb_ref[0, :, sl]
        kn = kn_ref[0, :, sl]
        vn = vn_ref[0, :, sl]
        kb16, vb16, kn16, vn16 = kb.astype(BF16), vb.astype(BF16), kn.astype(BF16), vn.astype(BF16)
        outs = []
        for h in range(2):
            qh = jnp.where(h0 if h == 0 else ~h0, qp, 0.0).astype(BF16)
            s_b = lax.dot_general(qh, kb16, NT, preferred_element_type=F32) * scale
            s_n = lax.dot_general(qh, kn16, NT, preferred_element_type=F32) * scale
            s_b = jnp.where(c_buf > 0, s_b, NEG_INF)
            s_n = jnp.where(c_new > 0, s_n, NEG_INF)
            m = jnp.maximum(jnp.max(s_b, axis=-1, keepdims=True), jnp.max(s_n, axis=-1, keepdims=True))
            w_b = c_buf * jnp.exp(s_b - m)
            w_n = c_new * jnp.exp(s_n - m)
            l = jnp.sum(w_b, axis=-1, keepdims=True) + jnp.sum(w_n, axis=-1, keepdims=True)
            o = (jnp.dot(w_b.astype(BF16), vb16, preferred_element_type=F32)
                 + jnp.dot(w_n.astype(BF16), vn16, preferred_element_type=F32))
            outs.append(o / l)
        o_ref[0, :, sl] = jnp.where(h0, outs[0], outs[1]).astype(BF16)

        for src, new, dst in ((kb, kn, ks_ref), (vb, vn, vs_ref)):
            rolled = pltpu.roll(src, M - t_valid, 0)
            new_r = pltpu.roll(new[0:8], 8 - t_valid, 0)
            dst[0, 0:M - 8, sl] = rolled[0:M - 8]
            dst[0, M - 8:M, sl] = jnp.where(row8 >= 8 - t_valid, new_r, rolled[M - 8:M])


def sample_attention(q_rot, k_new, v_new, k_buf, v_buf, t_valid, layer, depth, kv_all=None):
    N, TQ, _ = q_rot.shape
    M = k_buf.shape[2]
    TN_ = k_new.shape[1]
    assert t_valid <= 8 <= TQ
    blk = lambda r: pl.BlockSpec((1, r, 512), lambda n, j: (n, 0, j))
    buf = pl.BlockSpec((None, 1, M, 512), lambda n, j: (layer, n, 0, j))
    kv = jax.ShapeDtypeStruct((depth, N, M, MIX), F32)
    kv_spec = pl.BlockSpec((1, 1, M, 512), lambda n, j: (layer, n, 0, j))
    args = [q_rot, k_new, v_new, k_buf, v_buf]
    in_specs = [blk(TQ), blk(TN_), blk(TN_), buf, buf]
    aliases = {}
    if kv_all is not None:
        args += list(kv_all)
        in_specs += [pl.BlockSpec(memory_space=pl.ANY)] * 2
        aliases = {5: 1, 6: 2}
    return pl.pallas_call(
        functools.partial(_sample_attn_kernel, t_valid=t_valid),
        grid=(N, MIX // 512),
        in_specs=in_specs,
        out_specs=[blk(TQ), kv_spec, kv_spec],
        out_shape=[jax.ShapeDtypeStruct((N, TQ, MIX), BF16), kv, kv],
        input_output_aliases=aliases,
        compiler_params=_cparams(("parallel", "parallel")),
        name="sample_attention",
    )(*args)


def _glu_kernel(a_ref, g_ref, o_ref):
    o_ref[0] = a_ref[0] * _sigmoid(g_ref[0])


def glu(proj3, tt):
    N, T, _ = proj3.shape
    tt = min(tt, T)
    gw = 256
    assert C_CONV % gw == 0 and CONV_DIM % gw == 0
    ab = C_CONV // gw
    gb = (C_CONV + CONV_DIM) // gw
    return pl.pallas_call(
        _glu_kernel,
        grid=(N, T // tt, CONV_DIM // gw),
        in_specs=[pl.BlockSpec((1, tt, gw), lambda n, i, j: (n, i, ab + j)),
                  pl.BlockSpec((1, tt, gw), lambda n, i, j: (n, i, gb + j))],
        out_specs=pl.BlockSpec((1, tt, gw), lambda n, i, j: (n, i, j)),
        out_shape=jax.ShapeDtypeStruct((N, T, CONV_DIM), F32),
        compiler_params=_cparams(("parallel", "parallel", "parallel")),
        name="glu",
    )(proj3, proj3)


def _dwconv_kernel(u_ref, uh_ref, h0_ref, w_ref, b_ref, lw_ref, lb_ref, *rest, single_tile):
    o_ref, ext_ref, y_ref = rest[-3:]
    tt = u_ref.shape[1]
    i = pl.program_id(1)

    @pl.when(i == 0)
    def _():
        ext_ref[0:CONV_HALO, :] = h0_ref[0]

    if not single_tile:
        @pl.when(i > 0)
        def _():
            ext_ref[0:CONV_HALO, :] = uh_ref[0]

    ext_ref[CONV_HALO:, :] = u_ref[0]
    lead = CONV_HALO - (CONV_WIDTH - 1)
    rows = min(16, tt)
    for r0 in range(0, tt, rows):
        for c0 in range(0, CONV_DIM, 512):
            acc = jnp.broadcast_to(b_ref[:, c0:c0 + 512], (rows, 512))
            for j in range(CONV_WIDTH):
                acc = acc + ext_ref[r0 + lead + j:r0 + lead + j + rows, c0:c0 + 512] * w_ref[j:j + 1, c0:c0 + 512]
            y_ref[r0:r0 + rows, c0:c0 + 512] = acc
    y = y_ref[...]
    mu = jnp.mean(y, axis=-1, keepdims=True)
    var = jnp.mean(jnp.square(y - mu), axis=-1, keepdims=True)
    c = (y - mu) * lax.rsqrt(var + LN_EPS) * lw_ref[...] + lb_ref[...]
    o_ref[0] = (c * _sigmoid(c)).astype(BF16)


def dwconv_ln_silu(u, halo0, w, b, lw, lb, tt, mixed=None):
    N, T, C = u.shape
    tt = min(tt, T)
    hb = tt // CONV_HALO if tt >= CONV_HALO else 1
    in_specs = [pl.BlockSpec((1, tt, C), lambda n, i: (n, i, 0)),
                pl.BlockSpec((1, min(CONV_HALO, T), C), lambda n, i: (n, jnp.maximum(i * hb - 1, 0), 0)),
                pl.BlockSpec((1, CONV_HALO, C), lambda n, i: (n, 0, 0)),
                pl.BlockSpec((32, C), lambda n, i: (0, 0)),
                pl.BlockSpec((1, C), lambda n, i: (0, 0)),
                pl.BlockSpec((1, C), lambda n, i: (0, 0)),
                pl.BlockSpec((1, C), lambda n, i: (0, 0))]
    args = [u, u, halo0, w, b, lw, lb]
    if mixed is None:
        out_shape = jax.ShapeDtypeStruct((N, T, C), BF16)
        out_spec = pl.BlockSpec((1, tt, C), lambda n, i: (n, i, 0))
        aliases = {}
    else:
        in_specs.append(pl.BlockSpec(memory_space=pl.ANY))
        args.append(mixed)
        out_shape = jax.ShapeDtypeStruct(mixed.shape, mixed.dtype)
        out_spec = pl.BlockSpec((1, tt, C), lambda n, i: (n, i, M_CONV // C))
        aliases = {7: 0}
    return pl.pallas_call(
        functools.partial(_dwconv_kernel, single_tile=(T == tt)),
        grid=(N, T // tt),
        in_specs=in_specs,
        out_specs=out_spec,
        out_shape=out_shape,
        scratch_shapes=[pltpu.VMEM((CONV_HALO + tt, C), F32), pltpu.VMEM((tt, C), F32)],
        input_output_aliases=aliases,
        compiler_params=_cparams(("parallel", "arbitrary")),
        name="dwconv_ln_silu",
    )(*args)


def _ffn_conv_act(ext_ref, wg_ref, wv_ref, bg_ref, bv_ref, store, tt):
    lead = FFN_HALO - (FFN_CONV_WIDTH - 1)
    rows = min(32, tt)
    for r0 in range(0, tt, rows):
        ys = []
        for s, w_ref, b_ref in ((0, wg_ref, bg_ref), (1, wv_ref, bv_ref)):
            acc = jnp.broadcast_to(b_ref[...], (rows, b_ref.shape[1]))
            for j in range(FFN_CONV_WIDTH):
                acc = acc + ext_ref[s, r0 + lead + j:r0 + lead + j + rows, :] * w_ref[j:j + 1, :]
            ys.append(acc)
        gate, val = ys
        store(r0, rows, (gate * _sigmoid(gate) * val).astype(BF16))


def _ffn_act_kernel(ug_ref, uv_ref, hg_ref, hv_ref, h0g_ref, h0v_ref, wg_ref, wv_ref, bg_ref, bv_ref, o_ref, ext_ref):
    tt = ug_ref.shape[1]
    i = pl.program_id(1)

    @pl.when(i == 0)
    def _():
        ext_ref[0, 0:FFN_HALO, :] = h0g_ref[0]
        ext_ref[1, 0:FFN_HALO, :] = h0v_ref[0]

    @pl.when(i > 0)
    def _():
        ext_ref[0, 0:FFN_HALO, :] = hg_ref[0]
        ext_ref[1, 0:FFN_HALO, :] = hv_ref[0]

    ext_ref[0, FFN_HALO:, :] = ug_ref[0]
    ext_ref[1, FFN_HALO:, :] = uv_ref[0]

    def store(r0, rows, val):
        o_ref[0, r0:r0 + rows, :] = val

    _ffn_conv_act(ext_ref, wg_ref, wv_ref, bg_ref, bv_ref, store, tt)


def ffn_act(u3, halo0, w, b, tt, tf):
    N, T, F2 = u3.shape
    F = F2 // 2
    tt = min(tt, T)
    nf = F // tf
    hb = tt // FFN_HALO
    g = lambda n, i, j: (n, i, j)
    v = lambda n, i, j: (n, i, j + nf)
    hg = lambda n, i, j: (n, jnp.maximum(i * hb - 1, 0), j)
    hv = lambda n, i, j: (n, jnp.maximum(i * hb - 1, 0), j + nf)
    return pl.pallas_call(
        _ffn_act_kernel,
        grid=(N, T // tt, nf),
        in_specs=[pl.BlockSpec((1, tt, tf), g), pl.BlockSpec((1, tt, tf), v),
                  pl.BlockSpec((1, FFN_HALO, tf), hg), pl.BlockSpec((1, FFN_HALO, tf), hv),
                  pl.BlockSpec((1, FFN_HALO, tf), lambda n, i, j: (n, 0, j)),
                  pl.BlockSpec((1, FFN_HALO, tf), lambda n, i, j: (n, 0, j + nf)),
                  pl.BlockSpec((8, tf), lambda n, i, j: (0, j)), pl.BlockSpec((8, tf), lambda n, i, j: (0, j + nf)),
                  pl.BlockSpec((1, tf), lambda n, i, j: (0, j)), pl.BlockSpec((1, tf), lambda n, i, j: (0, j + nf))],
        out_specs=pl.BlockSpec((1, tt, tf), g),
        out_shape=jax.ShapeDtypeStruct((N, T, F), BF16),
        scratch_shapes=[pltpu.VMEM((2, FFN_HALO + tt, tf), F32)],
        compiler_params=_cparams(("parallel", "parallel", "parallel")),
        name="ffn_act",
    )(u3, u3, u3, u3, halo0, halo0, w, w, b, b)


def _ffn_fused_kernel(x_ref, g_ref, wg_ref, wv_ref, wd_ref, cwg_ref, cwv_ref, cbg_ref, cbv_ref, h0g_ref, h0v_ref,
                      o_ref, st_ref, h_ref, ext_ref, carry_ref, act_ref, *, tiles_per_seq):
    i = pl.program_id(0)
    j = pl.program_id(1)
    tm = x_ref.shape[0]

    @pl.when(j == 0)
    def _():
        x = x_ref[...]
        ms = jnp.mean(x * x, axis=-1, keepdims=True)
        h_ref[...] = (x * lax.rsqrt(ms + RMS_EPS) * g_ref[...]).astype(BF16)
        o_ref[...] = x

    h = h_ref[...]
    ug = jnp.dot(h, wg_ref[...], preferred_element_type=F32)
    uv = jnp.dot(h, wv_ref[...], preferred_element_type=F32)
    seq_start = (i % tiles_per_seq) == 0
    ext_ref[0, 0:FFN_HALO, :] = jnp.where(seq_start, h0g_ref[0], carry_ref[j, 0])
    ext_ref[1, 0:FFN_HALO, :] = jnp.where(seq_start, h0v_ref[0], carry_ref[j, 1])
    ext_ref[0, FFN_HALO:, :] = ug
    ext_ref[1, FFN_HALO:, :] = uv
    carry_ref[j, 0] = ug[tm - FFN_HALO:, :]
    carry_ref[j, 1] = uv[tm - FFN_HALO:, :]

    def store(r0, rows, val):
        act_ref[r0:r0 + rows, :] = val

    _ffn_conv_act(ext_ref, cwg_ref, cwv_ref, cbg_ref, cbv_ref, store, tm)
    o_ref[...] += jnp.dot(act_ref[...], wd_ref[...], preferred_element_type=F32)

    @pl.when((j == pl.num_programs(1) - 1) & (i % tiles_per_seq == tiles_per_seq - 1))
    def _():
        st_ref[0] = carry_ref[...]


def ffn_fused(x, g, w_up, w_down, layer, cw, cb, halo0, seq_len, tm, tf):
    M, D = x.shape
    F = w_down.shape[1]
    assert M % tm == 0 and seq_len % tm == 0 and F % tf == 0
    nf = F // tf
    tps = seq_len // tm
    gate = lambda i, j: (0, j)
    val = lambda i, j: (0, j + nf)
    return pl.pallas_call(
        functools.partial(_ffn_fused_kernel, tiles_per_seq=tps),
        grid=(M // tm, nf),
        in_specs=[pl.BlockSpec((tm, D), lambda i, j: (i, 0), pipeline_mode=pl.Buffered(1)),
                  pl.BlockSpec((1, D), lambda i, j: (0, 0)),
                  pl.BlockSpec((None, D, tf), lambda i, j: (layer, 0, j)),
                  pl.BlockSpec((None, D, tf), lambda i, j: (layer, 0, j + nf)),
                  pl.BlockSpec((None, tf, D), lambda i, j: (layer, j, 0)),
                  pl.BlockSpec((8, tf), gate), pl.BlockSpec((8, tf), val),
                  pl.BlockSpec((1, tf), gate), pl.BlockSpec((1, tf), val),
                  pl.BlockSpec((1, FFN_HALO, tf), lambda i, j: (i // tps, 0, j)),
                  pl.BlockSpec((1, FFN_HALO, tf), lambda i, j: (i // tps, 0, j + nf))],
        out_specs=[pl.BlockSpec((tm, D), lambda i, j: (i, 0)),
                   pl.BlockSpec((1, nf, 2, FFN_HALO, tf), lambda i, j: (i // tps, 0, 0, 0, 0))],
        out_shape=[jax.ShapeDtypeStruct((M, D), F32),
                   jax.ShapeDtypeStruct((M // seq_len, nf, 2, FFN_HALO, tf), F32)],
        scratch_shapes=[pltpu.VMEM((tm, D), BF16), pltpu.VMEM((2, FFN_HALO + tm, tf), F32),
                        pltpu.VMEM((nf, 2, FFN_HALO, tf), F32), pltpu.VMEM((tm, tf), BF16)],
        compiler_params=_cparams(("arbitrary", "arbitrary")),
        name="ffn_fused",
    )(x, g, w_up, w_up, w_down, cw, cw, cb, cb, halo0, halo0)


def _rwkv_kernel(r_ref, k_ref, v_ref, lo_ref,
                 sr_ref, sk_ref, sv_ref, slo_ref,
                 mur_ref, muk_ref, muv_ref, mulo_ref,
                 w0_ref, a0_ref, kk_ref, ka_ref, rk_ref, lnw_ref, lnb_ref,
                 w2_ref, a2_ref, g2_ref, s0_ref,
                 o_ref, so_ref,
                 S_ref, pr_ref, pk_ref, pv_ref, plo_ref, *, t_valid, t_total):
    c = pl.program_id(2)
    Tc = r_ref.shape[1]
    PW = 2 * HEAD_DIM
    n_pairs = r_ref.shape[2] // PW

    @pl.when(c == 0)
    def _():
        S_ref[...] = s0_ref[0]
        pr_ref[...] = sr_ref[0]
        pk_ref[...] = sk_ref[0]
        pv_ref[...] = sv_ref[0]
        plo_ref[...] = slo_ref[0]

    def lerp(p_ref, prev_ref, mu_ref):
        p = p_ref[0]
        row = lax.broadcasted_iota(jnp.int32, p.shape, 0)
        shifted = jnp.where(row == 0, prev_ref[...], pltpu.roll(p, 1, 0))
        prev_ref[...] = p[Tc - 1:Tc, :]
        return p + (shifted - p) * mu_ref[...]

    xr = lerp(r_ref, pr_ref, mur_ref)
    xk = lerp(k_ref, pk_ref, muk_ref)
    xv = lerp(v_ref, pv_ref, muv_ref)
    xlo = lerp(lo_ref, plo_ref, mulo_ref)
    xw = xlo[:, :DECAY_LORA]
    xa = xlo[:, DECAY_LORA:DECAY_LORA + AAA_LORA]
    xg = xlo[:, DECAY_LORA + AAA_LORA:]

    h0p = _head0_mask((Tc, PW))

    def head_sum(x):
        parts = []
        for p in range(n_pairs):
            xp = x[:, PW * p:PW * (p + 1)]
            s_0 = jnp.sum(jnp.where(h0p, xp, 0.0), axis=-1, keepdims=True)
            s_1 = jnp.sum(jnp.where(h0p, 0.0, xp), axis=-1, keepdims=True)
            parts.append(jnp.where(h0p, s_0, s_1))
        return parts[0] if n_pairs == 1 else jnp.concatenate(parts, axis=1)

    def mm(x, y_):
        return jnp.dot(x, y_, preferred_element_type=F32)

    z = w0_ref[...] + mm(jnp.tanh(xw).astype(BF16), w2_ref[...])
    w_log = jnp.minimum(z, 0.0) - jnp.log(1.0 + jnp.exp(-jnp.abs(z))) - 0.5
    ld = -jnp.exp(w_log)
    a = _sigmoid(a0_ref[...] + mm(xa.astype(BF16), a2_ref[...]))
    g = mm(_sigmoid(xg).astype(BF16), g2_ref[...])

    kk = xk * kk_ref[...]
    kk = kk * lax.rsqrt(jnp.maximum(head_sum(kk * kk), 1e-24))
    kmod = xk * (1.0 + (a - 1.0) * ka_ref[...])
    avec = -kk
    bvec = kk * a
    bonus = head_sum(xr * kmod * rk_ref[...]) * xv
    vval = xv

    if t_valid < t_total:
        row = lax.broadcasted_iota(jnp.int32, (Tc, n_pairs * PW), 0) + c * Tc
        ok = row < t_valid
        ld = jnp.where(ok, ld, 0.0)
        avec = jnp.where(ok, avec, 0.0)
        bvec = jnp.where(ok, bvec, 0.0)
        kmod_s = jnp.where(ok, kmod, 0.0)
        vval = jnp.where(ok, vval, 0.0)
    else:
        kmod_s = kmod

    C = CHUNK
    tri_r = lax.broadcasted_iota(jnp.int32, (C, C), 0)
    tri_c = lax.broadcasted_iota(jnp.int32, (C, C), 1)
    tril = jnp.where(tri_r >= tri_c, 1.0, 0.0).astype(F32)
    it = lax.broadcasted_iota(jnp.int32, (2 * C, 2 * C), 0) % C
    js = lax.broadcasted_iota(jnp.int32, (2 * C, 2 * C), 1) % C
    strict = it > js
    incl = it >= js
    h0c = _head0_mask((C, 2 * HEAD_DIM))

    def stack(x):
        return jnp.concatenate([jnp.where(h0c, x, 0.0), jnp.where(h0c, 0.0, x)], axis=0)

    n_chunks = Tc // C
    grp = 2 if n_chunks % 2 == 0 else 1
    gw = 2 * C * grp
    eye_g = jnp.where(lax.broadcasted_iota(jnp.int32, (gw, gw), 0)
                      == lax.broadcasted_iota(jnp.int32, (gw, gw), 1), 1.0, 0.0).astype(F32)
    zero_blk = jnp.zeros((2 * C, 2 * C), F32)

    def bdiag(blocks):
        if len(blocks) == 1:
            return blocks[0]
        return jnp.concatenate([jnp.concatenate([blocks[0], zero_blk], axis=1),
                                jnp.concatenate([zero_blk, blocks[1]], axis=1)], axis=0)

    def cat(blocks):
        return blocks[0] if len(blocks) == 1 else jnp.concatenate(blocks, axis=0)

    a16, r32, bb16, kb16, v16, gts, Ls, AKs, RBs, RKs = ({} for _ in range(10))
    units = [(p, j) for j in range(n_chunks) for p in range(n_pairs)]
    for p, j in units:
        sl = (slice(C * j, C * (j + 1)), slice(PW * p, PW * (p + 1)))
        ldc = ld[sl]
        cs = jnp.dot(tril, ldc, precision=lax.Precision.HIGHEST, preferred_element_type=F32)
        tot = cs[C - 1:C, :]
        eg = jnp.exp(cs)
        egi = jnp.exp(-cs)
        ege = jnp.exp(cs - ldc)
        et = jnp.exp(tot - cs)
        gts[p, j] = jnp.exp(tot)
        a_s = stack(avec[sl] * ege).astype(BF16)
        r_s = stack(xr[sl] * eg)
        b_s = stack(bvec[sl] * egi).astype(BF16)
        k_s = stack(kmod_s[sl] * egi).astype(BF16)
        sc = lax.dot_general(jnp.concatenate([a_s, r_s.astype(BF16)], axis=0), jnp.concatenate([b_s, k_s], axis=0), NT,
                             preferred_element_type=F32)
        a16[p, j] = a_s
        r32[p, j] = r_s
        bb16[p, j] = stack(bvec[sl] * et).astype(BF16)
        kb16[p, j] = stack(kmod_s[sl] * et).astype(BF16)
        v16[p, j] = stack(vval[sl]).astype(BF16)
        Ls[p, j] = jnp.where(strict, sc[:2 * C, :2 * C], 0.0)
        AKs[p, j] = jnp.where(strict, sc[:2 * C, 2 * C:], 0.0)
        RBs[p, j] = jnp.where(incl, sc[2 * C:, :2 * C], 0.0)
        RKs[p, j] = jnp.where(incl, sc[2 * C:, 2 * C:], 0.0)

    groups = [[(p, j) for j in range(g0, g0 + grp)] for g0 in range(0, n_chunks, grp) for p in range(n_pairs)]
    Ps = [bdiag([Ls[c_] for c_ in g_]) for g_ in groups]
    Tms = [eye_g + p_ for p_ in Ps]
    for _ in range(C.bit_length() - 2):
        for gi in range(len(groups)):
            P16 = Ps[gi].astype(BF16)
            Ps[gi] = mm(P16, P16)
            Tms[gi] = Tms[gi] + mm(Tms[gi].astype(BF16), Ps[gi].astype(BF16))

    RAs, YNs, Gs, Ns = {}, {}, {}, {}
    for gi, g_ in enumerate(groups):
        T16 = Tms[gi].astype(BF16)
        A2 = cat([a16[c_] for c_ in g_])
        V2 = cat([v16[c_] for c_ in g_])
        R2 = cat([r32[c_] for c_ in g_])
        RB16 = bdiag([RBs[c_] for c_ in g_]).astype(BF16)
        RK16 = bdiag([RKs[c_] for c_ in g_]).astype(BF16)
        TA16 = mm(T16, A2).astype(BF16)
        TV16 = mm(T16, mm(bdiag([AKs[c_] for c_ in g_]).astype(BF16), V2).astype(BF16)).astype(BF16)
        RA = R2 + mm(RB16, TA16)
        YN = mm(jnp.concatenate([RB16, RK16], axis=1), jnp.concatenate([TV16, V2], axis=0))
        for q, c_ in enumerate(g_):
            rs = slice(2 * C * q, 2 * C * (q + 1))
            RAs[c_] = RA[rs].astype(BF16)
            YNs[c_] = YN[rs]
            Gs[c_] = lax.dot_general(TA16[rs], bb16[c_], TN, preferred_element_type=F32).astype(BF16)
            Ns[c_] = lax.dot_general(jnp.concatenate([TV16[rs], v16[c_]], axis=0),
                                     jnp.concatenate([bb16[c_], kb16[c_]], axis=0), TN, preferred_element_type=F32)

    ys = {}
    Ss = [S_ref[p] for p in range(n_pairs)]
    for p, j in units:
        S16 = Ss[p].astype(BF16)
        Y = lax.dot_general(RAs[p, j], S16, NT, preferred_element_type=F32) + YNs[p, j]
        ys[p, j] = Y[:C] + Y[C:]
        Ss[p] = Ss[p] * gts[p, j] + mm(S16, Gs[p, j]) + Ns[p, j]
    for p in range(n_pairs):
        S_ref[p] = Ss[p]
    y_pairs = [cat([ys[p, j] for j in range(n_chunks)]) for p in range(n_pairs)]
    y = y_pairs[0] if n_pairs == 1 else jnp.concatenate(y_pairs, axis=1)

    mu = head_sum(y) * (1.0 / HEAD_DIM)
    var = head_sum(jnp.square(y - mu)) * (1.0 / HEAD_DIM)
    yn = (y - mu) * lax.rsqrt(var + GN_EPS) * lnw_ref[...] + lnb_ref[...]
    o_ref[0] = ((yn + bonus) * g).astype(BF16)

    @pl.when(c == pl.num_programs(2) - 1)
    def _():
        for p in range(n_pairs):
            so_ref[0, p] = Ss[p]


def rwkv_time_mix(proj3, shift0, s0, prm, t_valid, tc, out_width=MIX, pairs=2):
    N, T, _ = proj3.shape
    tc = min(tc, T)
    W = 2 * HEAD_DIM * pairs
    assert T % tc == 0 and tc % CHUNK == 0 and MIX % W == 0 and C_RWKV % W == 0
    rb = C_RWKV // W
    nb = MIX // W
    pspec = lambda w, off: pl.BlockSpec((1, tc, w), lambda n, h, c: (n, c, off(h)))
    sspec = lambda w, off: pl.BlockSpec((1, 1, w), lambda n, h, c: (n, 0, off(h)))
    mspec = lambda w, off: pl.BlockSpec((1, w), lambda n, h, c: (0, off(h)))
    hspec = pl.BlockSpec((1, W), lambda n, h, c: (0, h))
    assert (C_RWKV + 3 * MIX) % LORA_IN == 0
    lo_p = (C_RWKV + 3 * MIX) // LORA_IN
    lo_s = 3 * MIX // LORA_IN
    in_specs = [
        pspec(W, lambda h: rb + h), pspec(W, lambda h: rb + nb + h), pspec(W, lambda h: rb + 2 * nb + h),
        pspec(LORA_IN, lambda h: lo_p),
        sspec(W, lambda h: h), sspec(W, lambda h: nb + h), sspec(W, lambda h: 2 * nb + h),
        sspec(LORA_IN, lambda h: lo_s),
        mspec(W, lambda h: h), mspec(W, lambda h: nb + h), mspec(W, lambda h: 2 * nb + h),
        mspec(LORA_IN, lambda h: lo_s),
        hspec, hspec, hspec, hspec, hspec, hspec, hspec,
        pl.BlockSpec((DECAY_LORA, W), lambda n, h, c: (0, h)),
        pl.BlockSpec((AAA_LORA, W), lambda n, h, c: (0, h)),
        pl.BlockSpec((GATE_PAD, W), lambda n, h, c: (0, h)),
        pl.BlockSpec((1, pairs, 128, 128), lambda n, h, c: (n, h, 0, 0)),
    ]
    args = ([proj3] * 4 + [shift0] * 4 + [prm['mu']] * 4
            + [prm['w0'], prm['a0'], prm['k_k'], prm['k_a'], prm['r_k'], prm['ln_w'], prm['ln_b'],
               prm['w2'], prm['a2'], prm['g2'], s0])
    return pl.pallas_call(
        functools.partial(_rwkv_kernel, t_valid=t_valid, t_total=T),
        grid=(N, nb, T // tc),
        in_specs=in_specs,
        out_specs=[pl.BlockSpec((1, tc, W), lambda n, h, c: (n, c, M_RWKV // W + h)),
                   pl.BlockSpec((1, pairs, 128, 128), lambda n, h, c: (n, h, 0, 0))],
        out_shape=[jax.ShapeDtypeStruct((N, T, out_width), BF16),
                   jax.ShapeDtypeStruct((N, HEADS // 2, 128, 128), F32)],
        scratch_shapes=[pltpu.VMEM((pairs, 128, 128), F32), pltpu.VMEM((1, W), F32), pltpu.VMEM((1, W), F32),
                        pltpu.VMEM((1, W), F32), pltpu.VMEM((1, LORA_IN), F32)],
        compiler_params=_cparams(("parallel", "parallel", "arbitrary")),
        name="rwkv7_time_mix",
    )(*args)


def _pad_rwkv_cols(t):
    return jnp.pad(t, [(0, 0)] * (t.ndim - 1) + [(0, RWKV_PAD - RWKV_PROJ)])


PREP_ROWS = 512
PREP_COLS = 1024


def _prep_w_in_kernel(cur_ref, prev_ref, o_ref):
    j = pl.program_id(2)
    cur = cur_ref[...].astype(F32)
    prev = prev_ref[...].astype(F32)
    lane = lax.broadcasted_iota(jnp.int32, cur.shape, 1)
    col = lane + j * PREP_COLS
    shift = RWKV_PAD - RWKV_PROJ
    shifted = jnp.where(lane < shift, pltpu.roll(prev, shift, 1), pltpu.roll(cur, shift, 1))
    in_tail = (col >= RWKV_PAD) & (col < C_CONV + CONV_PROJ)
    o_ref[...] = jnp.where(col < RWKV_PROJ, cur, jnp.where(in_tail, shifted, 0.0)).astype(BF16)


def prep_w_in(w16):
    depth, K, n = w16.shape
    assert K % PREP_ROWS == 0 and NP % PREP_COLS == 0 and pl.cdiv(n, PREP_COLS) == NP // PREP_COLS
    assert n + RWKV_PAD - RWKV_PROJ == C_CONV + CONV_PROJ
    blk = (None, PREP_ROWS, PREP_COLS)
    return pl.pallas_call(
        _prep_w_in_kernel,
        grid=(depth, K // PREP_ROWS, NP // PREP_COLS),
        in_specs=[pl.BlockSpec(blk, lambda l, i, j: (l, i, j)),
                  pl.BlockSpec(blk, lambda l, i, j: (l, i, jnp.maximum(j - 1, 0)))],
        out_specs=pl.BlockSpec(blk, lambda l, i, j: (l, i, j)),
        out_shape=jax.ShapeDtypeStruct((depth, K, NP), BF16),
        compiler_params=_cparams(("parallel", "parallel", "parallel")),
        name="prep_w_in",
    )(w16, w16)


def _matmul_weights(w_in, w_out, w_up, w_down):
    return {'w_in': prep_w_in(w_in.astype(BF16)), 'w_out': w_out.astype(BF16), 'w_up': w_up.astype(BF16), 'w_down': w_down.astype(BF16)}


def _layer_params(l, big, norm_mix, rwkv_mu, rwkv_w0, rwkv_w2, rwkv_a0, rwkv_a2, rwkv_g2, rwkv_k_k, rwkv_k_a,
                  rwkv_r_k, rwkv_ln_w, rwkv_ln_b, conv_w, conv_b, conv_ln_w, conv_ln_b, norm_ffn,
                  ffn_conv_w, ffn_conv_b):
    row = lambda t: t.reshape(1, -1)
    return {
        'w_in': big['w_in'], 'w_out': big['w_out'], 'w_up': big['w_up'], 'w_down': big['w_down'],
        'norm_mix': row(norm_mix[l]),
        'mu': _pad_rwkv_cols(row(rwkv_mu[l])),
        'w0': row(rwkv_w0[l]), 'a0': row(rwkv_a0[l]), 'k_k': row(rwkv_k_k[l]), 'k_a': row(rwkv_k_a[l]),
        'r_k': row(rwkv_r_k[l]), 'ln_w': row(rwkv_ln_w[l]), 'ln_b': row(rwkv_ln_b[l]),
        'w2': rwkv_w2[l].astype(BF16), 'a2': rwkv_a2[l].astype(BF16),
        'g2': jnp.pad(rwkv_g2[l], ((0, GATE_PAD - GATE_LORA), (0, 0))).astype(BF16),
        'conv_w': jnp.pad(conv_w[l], ((0, 32 - CONV_WIDTH), (0, 0))), 'conv_b': row(conv_b[l]),
        'conv_ln_w': row(conv_ln_w[l]), 'conv_ln_b': row(conv_ln_b[l]),
        'norm_ffn': row(norm_ffn[l]),
        'ffn_conv_w': jnp.pad(ffn_conv_w[l], ((0, 8 - FFN_CONV_WIDTH), (0, 0))), 'ffn_conv_b': row(ffn_conv_b[l]),
    }


def _rope_tables(pos):
    half = HEAD_DIM // 2
    inv_freq = ROPE_THETA ** (-jnp.arange(half, dtype=F32) * 2.0 / HEAD_DIM)
    ang = pos.astype(F32)[..., None] * inv_freq
    cos, sin = jnp.cos(ang), jnp.sin(ang)
    return jnp.concatenate([cos, cos, cos, cos], axis=-1), jnp.concatenate([-sin, sin, -sin, sin], axis=-1)


def _state_to_blockdiag(s):
    N = s.shape[0]
    s = s.reshape(N, HEADS // 2, 2, HEAD_DIM, HEAD_DIM)
    z = jnp.zeros_like(s[:, :, 0])
    top = jnp.concatenate([s[:, :, 0], z], axis=-1)
    bot = jnp.concatenate([z, s[:, :, 1]], axis=-1)
    return jnp.concatenate([top, bot], axis=-2)


def _blockdiag_to_state(s):
    N = s.shape[0]
    return jnp.stack([s[:, :, :HEAD_DIM, :HEAD_DIM], s[:, :, HEAD_DIM:, HEAD_DIM:]], axis=2).reshape(
        N, HEADS, HEAD_DIM, HEAD_DIM)


def _last_rows(buf, u, t_valid):
    keep = buf.shape[1]
    if t_valid >= keep:
        return u[:, t_valid - keep:t_valid]
    return jnp.concatenate([buf[:, t_valid:], u[:, :t_valid]], axis=1)


def _front_pad(buf, rows):
    return jnp.pad(buf, ((0, 0), (rows - buf.shape[1], 0), (0, 0)))


def _prompt_layer(x2, N, S, prm, cos_t, sin_t, layer, depth, kv_all):
    f32 = x2.dtype
    proj3 = norm_matmul(x2, prm['norm_mix'], prm['w_in'], layer, 1024, 512, single_buffer_x=True).reshape(N, S, NP)

    mixed, s_new = rwkv_time_mix(proj3, jnp.zeros((N, 1, RWKV_PAD), f32), jnp.zeros((N, HEADS // 2, 128, 128), f32),
                                 prm, S, 256, out_width=D_MODEL, pairs=4)
    shift_new = proj3[:, S - 1:S, C_RWKV:C_RWKV + RWKV_PROJ]

    mixed, k_all, v_all = prompt_attention(proj3, cos_t, sin_t, mixed, layer, depth, kv_all)

    u = glu(proj3, 1024)
    mixed = dwconv_ln_silu(u, jnp.zeros((N, CONV_HALO, CONV_DIM), f32), prm['conv_w'], prm['conv_b'],
                           prm['conv_ln_w'], prm['conv_ln_b'], 128, mixed=mixed)
    conv_new = u[:, S - (CONV_WIDTH - 1):]

    x2 = matmul_res(mixed.reshape(N * S, D_MODEL), prm['w_out'], layer, x2, 512, 512, D_MODEL)

    x2, tails = ffn_fused(x2, prm['norm_ffn'], prm['w_up'], prm['w_down'], layer, prm['ffn_conv_w'], prm['ffn_conv_b'],
                          jnp.zeros((N, FFN_HALO, 2 * D_FF), f32), S, 512, 256)
    keep = FFN_CONV_WIDTH - 1
    ffn_new = jnp.transpose(tails[:, :, :, FFN_HALO - keep:, :], (0, 3, 2, 1, 4)).reshape(N, keep, 2 * D_FF)
    return x2, (shift_new, _blockdiag_to_state(s_new), conv_new, ffn_new), (k_all, v_all)


def _sample_layer(x2, N, T, t_valid, prm, cos_t, sin_t, carry, attn_bufs, layer, depth, kv_all):
    shift0, wkv0, conv_buf, ffn_buf = carry
    proj3 = norm_matmul(x2, prm['norm_mix'], prm['w_in'], layer, 512, 512).reshape(N, T, NP)

    proj_r = jnp.pad(proj3, ((0, 0), (0, CHUNK - T), (0, 0)))
    o_rwkv, s_new = rwkv_time_mix(proj_r, _pad_rwkv_cols(shift0), _state_to_blockdiag(wkv0), prm, t_valid, CHUNK, pairs=4)
    o_rwkv = o_rwkv[:, :T]
    shift_new = proj3[:, t_valid - 1:t_valid, C_RWKV:C_RWKV + RWKV_PROJ]

    q_rot, k_rot = rope(proj3, cos_t, sin_t, 256)
    v_new = proj3[:, :, C_ATT + 2 * MIX:C_ATT + 3 * MIX]
    padr = ((0, 0), (0, 128 - T), (0, 0))
    o_att, k_all, v_all = sample_attention(q_rot, jnp.pad(k_rot, padr), jnp.pad(v_new, padr),
                                           attn_bufs[0], attn_bufs[1], t_valid, layer, depth, kv_all)

    u = glu(proj3, 256)
    o_conv = dwconv_ln_silu(u, _front_pad(conv_buf, CONV_HALO), prm['conv_w'], prm['conv_b'],
                            prm['conv_ln_w'], prm['conv_ln_b'], 128)
    conv_new = _last_rows(conv_buf, u, t_valid)

    mixed = jnp.concatenate([o_rwkv, o_att, o_conv], axis=-1).reshape(N * T, D_MODEL)
    x2 = matmul_res(mixed, prm['w_out'], layer, x2, 512, 512, D_MODEL)

    uf3 = norm_matmul(x2, prm['norm_ffn'], prm['w_up'], layer, 512, 512).reshape(N, T, 2 * D_FF)
    act = ffn_act(uf3, _front_pad(ffn_buf, FFN_HALO), prm['ffn_conv_w'], prm['ffn_conv_b'], 512, D_FF // 2)
    ffn_new = _last_rows(ffn_buf, uf3, t_valid)
    x2 = matmul_res(act.reshape(N * T, D_FF), prm['w_down'], layer, x2, 512, 512, D_FF // 2)
    return x2, (shift_new, _blockdiag_to_state(s_new), conv_new, ffn_new), (k_all, v_all)


def kernel(x_prompt, x_sample, state_rwkv_shift, state_rwkv_wkv, state_attn_k, state_attn_v, state_conv, state_ffn_conv, pos_sample, norm_mix, w_in, rwkv_mu, rwkv_w0, rwkv_w2, rwkv_a0, rwkv_a2, rwkv_g2, rwkv_k_k, rwkv_k_a, rwkv_r_k, rwkv_ln_w, rwkv_ln_b, conv_w, conv_b, conv_ln_w, conv_ln_b, w_out, norm_ffn, w_up, ffn_conv_w, ffn_conv_b, w_down, norm_final):
    B, S, _ = x_prompt.shape
    NB, TS, _ = x_sample.shape
    depth = w_in.shape[0]
    win_buf = state_attn_k.shape[2]
    assert win_buf == S, "prompt key/value state is the whole rotated sequence"
    TSP = 8

    xp = x_prompt.reshape(B * S, D_MODEL)
    xs = jnp.pad(x_sample, ((0, 0), (0, TSP - TS), (0, 0))).reshape(NB * TSP, D_MODEL)
    cos_p, sin_p = _rope_tables(jnp.arange(S, dtype=jnp.int32)[None])
    cos_s, sin_s = _rope_tables(jnp.pad(pos_sample, ((0, 0), (0, TSP - TS))))
    new_p, new_s = [], []
    kv_p, kv_s = None, None
    big = _matmul_weights(w_in, w_out, w_up, w_down)
    bufs = (state_attn_k.reshape(depth, NB, win_buf, MIX), state_attn_v.reshape(depth, NB, win_buf, MIX))
    for l in range(depth):
        prm = _layer_params(l, big, norm_mix, rwkv_mu, rwkv_w0, rwkv_w2, rwkv_a0, rwkv_a2, rwkv_g2, rwkv_k_k,
                            rwkv_k_a, rwkv_r_k, rwkv_ln_w, rwkv_ln_b, conv_w, conv_b, conv_ln_w, conv_ln_b,
                            norm_ffn, ffn_conv_w, ffn_conv_b)
        xp, st_p, kv_p = _prompt_layer(xp, B, S, prm, cos_p, sin_p, l, depth, kv_p)
        carry_s = (state_rwkv_shift[l], state_rwkv_wkv[l], state_conv[l], state_ffn_conv[l])
        xs, st_s, kv_s = _sample_layer(xs, NB, TSP, TS, prm, cos_s, sin_s, carry_s, bufs, l, depth, kv_s)
        new_p.append(st_p)
        new_s.append(st_s)

    g = norm_final.reshape(1, D_MODEL)
    y_prompt = rmsnorm(xp, g, 512).reshape(B, S, D_MODEL)
    y_sample = rmsnorm(xs, g, 512).reshape(NB, TSP, D_MODEL)[:, :TS]

    def stack(states, i):
        return jnp.stack([st[i] for st in states], axis=0)

    heads_p = (depth, B, win_buf, HEADS, HEAD_DIM)
    heads_s = (depth, NB, win_buf, HEADS, HEAD_DIM)
    return (y_prompt, y_sample,
            stack(new_p, 0), stack(new_s, 0), stack(new_p, 1), stack(new_s, 1),
            kv_p[0].reshape(heads_p), kv_s[0].reshape(heads_s), kv_p[1].reshape(heads_p), kv_s[1].reshape(heads_s),
            stack(new_p, 2), stack(new_s, 2), stack(new_p, 3), stack(new_s, 3))
```

```python
import functools

import jax
import jax.numpy as jnp
from jax import lax
from jax.experimental import pallas as pl
from jax.experimental.pallas import tpu as pltpu

F32 = jnp.float32
BF16 = jnp.bfloat16

D_MODEL = 4096
HEAD_DIM = 64
HEADS = 24
MIX = HEADS * HEAD_DIM
CONV_DIM = 1024
DECAY_LORA = 128
AAA_LORA = 128
GATE_LORA = 480
GATE_PAD = 512
RWKV_PROJ = 3 * MIX + DECAY_LORA + AAA_LORA + GATE_LORA
RWKV_PAD = 3 * MIX + DECAY_LORA + AAA_LORA + GATE_PAD
ATT_PROJ = 3 * MIX
CONV_PROJ = 2 * CONV_DIM
CONV_WIDTH = 31
CONV_HALO = 32
D_FF = 11008
FFN_CONV_WIDTH = 3
FFN_HALO = 8
BAND = 128
DILATIONS = (1, 4, 16)
WINDOWS = (128, 512, 2048)
ROPE_THETA = 10000.0
RMS_EPS = 1e-6
LN_EPS = 1e-5
GN_EPS = 64e-5
NEG_INF = -1e30
CHUNK = 64

C_RWKV = 0
C_ATT = RWKV_PAD
C_CONV = C_ATT + ATT_PROJ
NP = 12288
LORA_IN = DECAY_LORA + AAA_LORA + GATE_PAD
M_RWKV, M_ATT, M_CONV = 0, MIX, 2 * MIX

VMEM_LIMIT = 56 * 1024 * 1024

NT = (((1,), (1,)), ((), ()))
TN = (((0,), (0,)), ((), ()))


def _cparams(sem):
    return pltpu.CompilerParams(dimension_semantics=sem, vmem_limit_bytes=VMEM_LIMIT)


def _sigmoid(x):
    return 1.0 / (1.0 + jnp.exp(-x))


def _head0_mask(shape):
    return lax.broadcasted_iota(jnp.int32, shape, 1) % (2 * HEAD_DIM) < HEAD_DIM


def _norm_matmul_kernel(x_ref, g_ref, w_ref, o_ref, h_ref):
    @pl.when(pl.program_id(1) == 0)
    def _():
        x = x_ref[...]
        ms = jnp.mean(x * x, axis=-1, keepdims=True)
        h_ref[...] = (x * lax.rsqrt(ms + RMS_EPS) * g_ref[...]).astype(BF16)

    o_ref[...] = jnp.dot(h_ref[...], w_ref[...], preferred_element_type=F32)


def norm_matmul(x, g, w, layer, tm, tn, single_buffer_x=False):
    M, K = x.shape
    N = w.shape[2]
    tm = min(tm, M)
    assert M % tm == 0 and N % tn == 0
    x_mode = dict(pipeline_mode=pl.Buffered(1)) if single_buffer_x else {}
    return pl.pallas_call(
        _norm_matmul_kernel,
        grid=(M // tm, N // tn),
        in_specs=[pl.BlockSpec((tm, K), lambda i, j: (i, 0), **x_mode),
                  pl.BlockSpec((1, K), lambda i, j: (0, 0)),
                  pl.BlockSpec((None, K, tn), lambda i, j: (layer, 0, j))],
        out_specs=pl.BlockSpec((tm, tn), lambda i, j: (i, j)),
        out_shape=jax.ShapeDtypeStruct((M, N), F32),
        scratch_shapes=[pltpu.VMEM((tm, K), BF16)],
        compiler_params=_cparams(("parallel", "arbitrary")),
        name="norm_matmul",
    )(x, g, w)


def _matmul_res_kernel(a_ref, w_ref, r_ref, o_ref, acc_ref, *, nk):
    k = pl.program_id(2)

    @pl.when(k == 0)
    def _():
        acc_ref[...] = r_ref[...]

    acc_ref[...] += jnp.dot(a_ref[...], w_ref[...], preferred_element_type=F32)

    @pl.when(k == nk - 1)
    def _():
        o_ref[...] = acc_ref[...]


def matmul_res(a, w, layer, res, tm, tn, tk):
    M, K = a.shape
    N = w.shape[2]
    tm = min(tm, M)
    assert M % tm == 0 and N % tn == 0 and K % tk == 0
    nk = K // tk
    return pl.pallas_call(
        functools.partial(_matmul_res_kernel, nk=nk),
        grid=(M // tm, N // tn, nk),
        in_specs=[pl.BlockSpec((tm, tk), lambda i, j, k: (i, k)),
                  pl.BlockSpec((None, tk, tn), lambda i, j, k: (layer, k, j)),
                  pl.BlockSpec((tm, tn), lambda i, j, k: (i, j))],
        out_specs=pl.BlockSpec((tm, tn), lambda i, j, k: (i, j)),
        out_shape=jax.ShapeDtypeStruct((M, N), F32),
        scratch_shapes=[pltpu.VMEM((tm, tn), F32)],
        compiler_params=_cparams(("parallel", "parallel", "arbitrary")),
        name="matmul_res",
    )(a, w, res)


def _rmsnorm_kernel(x_ref, g_ref, o_ref):
    x = x_ref[...]
    ms = jnp.mean(x * x, axis=-1, keepdims=True)
    o_ref[...] = x * lax.rsqrt(ms + RMS_EPS) * g_ref[...]


def rmsnorm(x, g, tm):
    M, K = x.shape
    tm = min(tm, M)
    return pl.pallas_call(
        _rmsnorm_kernel,
        grid=(M // tm,),
        in_specs=[pl.BlockSpec((tm, K), lambda i: (i, 0)), pl.BlockSpec((1, K), lambda i: (0, 0))],
        out_specs=pl.BlockSpec((tm, K), lambda i: (i, 0)),
        out_shape=jax.ShapeDtypeStruct((M, K), F32),
        compiler_params=_cparams(("parallel",)),
        name="rmsnorm",
    )(x, g)


def _rot_half(x, first_half):
    w = x.shape[1]
    return jnp.where(first_half, pltpu.roll(x, w - HEAD_DIM // 2, 1), pltpu.roll(x, HEAD_DIM // 2, 1))


ROPE_BLOCK = 256


def _rope_kernel(q_ref, k_ref, cos_ref, sin_ref, qo_ref, ko_ref):
    reps = ROPE_BLOCK // (2 * HEAD_DIM)
    cos = jnp.concatenate([cos_ref[0]] * reps, axis=1)
    sin = jnp.concatenate([sin_ref[0]] * reps, axis=1)
    lane = lax.broadcasted_iota(jnp.int32, cos.shape, 1)
    first_half = (lane % HEAD_DIM) < (HEAD_DIM // 2)
    q = q_ref[0]
    k = k_ref[0]
    qo_ref[0] = q * cos + _rot_half(q, first_half) * sin
    ko_ref[0] = k * cos + _rot_half(k, first_half) * sin


def rope(proj3, cos_t, sin_t, tt):
    N, T, _ = proj3.shape
    tt = min(tt, T)
    tab = lambda n, i, j: (n, i, 0)
    rw = ROPE_BLOCK
    assert C_ATT % rw == 0 and MIX % rw == 0
    qb = C_ATT // rw
    kb = (C_ATT + MIX) // rw
    out = jax.ShapeDtypeStruct((N, T, MIX), F32)
    return pl.pallas_call(
        _rope_kernel,
        grid=(N, T // tt, MIX // rw),
        in_specs=[pl.BlockSpec((1, tt, rw), lambda n, i, j: (n, i, qb + j)),
                  pl.BlockSpec((1, tt, rw), lambda n, i, j: (n, i, kb + j)),
                  pl.BlockSpec((1, tt, 128), tab),
                  pl.BlockSpec((1, tt, 128), tab)],
        out_specs=[pl.BlockSpec((1, tt, rw), lambda n, i, j: (n, i, j)),
                   pl.BlockSpec((1, tt, rw), lambda n, i, j: (n, i, j))],
        out_shape=[out, out],
        compiler_params=_cparams(("parallel", "parallel", "parallel")),
        name="rope",
    )(proj3, proj3, cos_t, sin_t)


def _prompt_attn_kernel(*refs):
    q_ref, k_ref, v_ref, cos_ref, sin_ref = refs[:5]
    o_ref, ko_all_ref, vo_all_ref, qs_ref, acc_ref, m_ref, l_ref = refs[-7:]
    ko_ref = ko_all_ref.at[0]
    vo_ref = vo_all_ref.at[0]
    S = q_ref.shape[1]
    B = BAND
    P = 2 * HEAD_DIM
    lane = lax.broadcasted_iota(jnp.int32, (B, P), 1)
    first_half = (lane % HEAD_DIM) < (HEAD_DIM // 2)
    h0 = _head0_mask((B, P))
    scale = HEAD_DIM ** -0.5

    def rope_rows(i, carry):
        for u in range(2):
            rows = pl.ds(pl.multiple_of((2 * i + u) * B, B), B)
            cos = cos_ref[0, rows, :]
            sin = sin_ref[0, rows, :]
            q = q_ref[0, rows, :]
            k = k_ref[0, rows, :]
            qs_ref[rows, :] = q * cos + _rot_half(q, first_half) * sin
            ko_ref[0, rows, :] = k * cos + _rot_half(k, first_half) * sin
            vo_ref[0, rows, :] = v_ref[0, rows, :]
        return carry

    lax.fori_loop(0, S // (2 * B), rope_rows, 0)

    def attend(q, kk, vv, valid):
        ms, ls, os_ = [], [], []
        for h in range(2):
            qh = jnp.where(h0 if h == 0 else ~h0, q, 0.0).astype(BF16)
            s = lax.dot_general(qh, kk, NT, preferred_element_type=F32) * scale
            s = jnp.where(valid, s, NEG_INF)
            m = jnp.max(s, axis=-1, keepdims=True)
            e = jnp.exp(s - m)
            ms.append(m)
            ls.append(jnp.sum(e, axis=-1, keepdims=True))
            os_.append(jnp.dot(e.astype(BF16), vv, preferred_element_type=F32))
        return jnp.where(h0, ms[0], ms[1]), jnp.where(h0, ls[0], ls[1]), jnp.where(h0, os_[0], os_[1])

    def merge(rows, m_c, l_c, o_c):
        m_p = m_ref[rows, :]
        m_n = jnp.maximum(m_p, m_c)
        a_p = jnp.exp(m_p - m_n)
        a_c = jnp.exp(m_c - m_n)
        acc_ref[rows, :] = acc_ref[rows, :] * a_p + o_c * a_c
        l_ref[rows, :] = l_ref[rows, :] * a_p + l_c * a_c
        m_ref[rows, :] = m_n

    qi2 = lax.broadcasted_iota(jnp.int32, (B, 2 * B), 0)
    ki2 = lax.broadcasted_iota(jnp.int32, (B, 2 * B), 1)
    dist2 = qi2 + B - ki2
    band2 = (dist2 >= 0) & (dist2 <= B)
    qi1 = lax.broadcasted_iota(jnp.int32, (B, B), 0)
    ki1 = lax.broadcasted_iota(jnp.int32, (B, B), 1)
    causal1 = qi1 >= ki1

    def two_block_keys(cur, prev):
        kk = jnp.concatenate([ko_ref[0, prev, :], ko_ref[0, cur, :]], axis=0).astype(BF16)
        vv = jnp.concatenate([v_ref[0, prev, :], v_ref[0, cur, :]], axis=0).astype(BF16)
        return kk, vv

    d16 = DILATIONS[2]
    assert S == B * d16
    per_iter = 8

    def dil16(it, carry):
        for u in range(per_iter):
            cur = pl.ds(it * per_iter + u, B, stride=d16)
            kk, vv = two_block_keys(cur, cur)
            m_c, l_c, o_c = attend(qs_ref[cur, :], kk, vv, band2 & (ki2 >= B))
            acc_ref[cur, :] = o_c
            m_ref[cur, :] = m_c
            l_ref[cur, :] = l_c
        return carry

    lax.fori_loop(0, d16 // per_iter, dil16, 0)

    d4 = DILATIONS[1]

    def dil4(ib, carry):
        base = pl.multiple_of(ib * (B * d4), B * d4)
        pbase = pl.multiple_of(jnp.maximum(ib - 1, 0) * (B * d4), B * d4)
        for r in range(d4):
            cur = pl.ds(base + r, B, stride=d4)
            prev = pl.ds(pbase + r, B, stride=d4)
            kk, vv = two_block_keys(cur, prev)
            m_c, l_c, o_c = attend(qs_ref[cur, :], kk, vv, band2 & ((ki2 >= B) | (ib > 0)))
            merge(cur, m_c, l_c, o_c)
        return carry

    lax.fori_loop(0, S // (B * d4), dil4, 0)

    unroll = 4

    def dil1(it, carry):
        for u in range(unroll):
            ib = it * unroll + u
            cur = pl.ds(pl.multiple_of(ib * B, B), B)
            prev = pl.ds(pl.multiple_of(jnp.maximum(ib - 1, 0) * B, B), B)
            kk, vv = two_block_keys(cur, prev)
            m_c, l_c, o_c = attend(qs_ref[cur, :], kk, vv, band2 & ((ki2 >= B) | (ib > 0)))
            m_p = m_ref[cur, :]
            m_n = jnp.maximum(m_p, m_c)
            a_p = jnp.exp(m_p - m_n)
            a_c = jnp.exp(m_c - m_n)
            acc = acc_ref[cur, :] * a_p + o_c * a_c
            l_n = l_ref[cur, :] * a_p + l_c * a_c
            o_ref[0, cur, :] = (acc / l_n).astype(BF16)
        return carry

    lax.fori_loop(0, S // (B * unroll), dil1, 0)


def prompt_attention(proj3, cos_t, sin_t, mixed, layer, depth, kv_all=None):
    N, S, _ = proj3.shape
    qb, kb, vb = C_ATT // 128, (C_ATT + MIX) // 128, (C_ATT + 2 * MIX) // 128
    col = lambda b: pl.BlockSpec((1, S, 128), lambda n, h: (n, 0, b + h))
    tab = pl.BlockSpec((1, S, 128), lambda n, h: (0, 0, 0))
    anyspec = pl.BlockSpec(memory_space=pl.ANY)
    kv = jax.ShapeDtypeStruct((depth, N, S, MIX), F32)
    kv_spec = pl.BlockSpec((1, 1, S, 128), lambda n, h: (layer, n, 0, h))
    args = [proj3, proj3, proj3, cos_t, sin_t, mixed]
    in_specs = [col(qb), col(kb), col(vb), tab, tab, anyspec]
    aliases = {5: 0}
    if kv_all is not None:
        args += list(kv_all)
        in_specs += [anyspec, anyspec]
        aliases.update({6: 1, 7: 2})
    return pl.pallas_call(
        _prompt_attn_kernel,
        grid=(N, HEADS // 2),
        in_specs=in_specs,
        out_specs=[col(M_ATT // 128), kv_spec, kv_spec],
        out_shape=[jax.ShapeDtypeStruct(mixed.shape, mixed.dtype), kv, kv],
        scratch_shapes=[pltpu.VMEM((S, 128), F32)] * 4,
        input_output_aliases=aliases,
        compiler_params=_cparams(("parallel", "parallel")),
        name="prompt_attention",
    )(*args)


def _sample_attn_kernel(*refs, t_valid):
    q_ref, kn_ref, vn_ref, kb_ref, vb_ref = refs[:5]
    o_ref, ks_all_ref, vs_all_ref = refs[-3:]
    ks_ref = ks_all_ref.at[0]
    vs_ref = vs_all_ref.at[0]
    TQ = q_ref.shape[1]
    M = kb_ref.shape[1]
    TN_ = kn_ref.shape[1]
    h0 = _head0_mask((TQ, 2 * HEAD_DIM))
    scale = HEAD_DIM ** -0.5

    def counts(delta, in_range):
        c = jnp.zeros(delta.shape, F32)
        for win, dil in zip(WINDOWS, DILATIONS):
            c = c + jnp.where((delta % dil == 0) & (delta <= win) & in_range, 1.0, 0.0)
        return c

    tq = lax.broadcasted_iota(jnp.int32, (TQ, M), 0)
    kb_i = lax.broadcasted_iota(jnp.int32, (TQ, M), 1)
    d_buf = M + tq - kb_i
    c_buf = counts(d_buf, d_buf >= 0)
    tq2 = lax.broadcasted_iota(jnp.int32, (TQ, TN_), 0)
    tn2 = lax.broadcasted_iota(jnp.int32, (TQ, TN_), 1)
    d_new = tq2 - tn2
    c_new = counts(d_new, (d_new >= 0) & (tn2 < t_valid))

    row8 = lax.broadcasted_iota(jnp.int32, (8, 2 * HEAD_DIM), 0)
    for p in range(4):
        sl = slice(128 * p, 128 * (p + 1))
        qp = q_ref[0, :, sl]
        kb = kb_ref[0, :, sl]
        vb = vb_ref[0, :, sl]
        kn = kn_ref[0, :, sl]
        vn = vn_ref[0, :, sl]
        kb16, vb16, kn16, vn16 = kb.astype(BF16), vb.astype(BF16), kn.astype(BF16), vn.astype(BF16)
        outs = []
        for h in range(2):
            qh = jnp.where(h0 if h == 0 else ~h0, qp, 0.0).astype(BF16)
            s_b = lax.dot_general(qh, kb16, NT, preferred_element_type=F32) * scale
            s_n = lax.dot_general(qh, kn16, NT, preferred_element_type=F32) * scale
            s_b = jnp.where(c_buf > 0, s_b, NEG_INF)
            s_n = jnp.where(c_new > 0, s_n, NEG_INF)
            m = jnp.maximum(jnp.max(s_b, axis=-1, keepdims=True), jnp.max(s_n, axis=-1, keepdims=True))
            w_b = c_buf * jnp.exp(s_b - m)
            w_n = c_new * jnp.exp(s_n - m)
            l = jnp.sum(w_b, axis=-1, keepdims=True) + jnp.sum(w_n, axis=-1, keepdims=True)
            o = (jnp.dot(w_b.astype(BF16), vb16, preferred_element_type=F32)
                 + jnp.dot(w_n.astype(BF16), vn16, preferred_element_type=F32))
            outs.append(o / l)
        o_ref[0, :, sl] = jnp.where(h0, outs[0], outs[1]).astype(BF16)

        for src, new, dst in ((kb, kn, ks_ref), (vb, vn, vs_ref)):
            rolled = pltpu.roll(src, M - t_valid, 0)
            new_r = pltpu.roll(new[0:8], 8 - t_valid, 0)
            dst[0, 0:M - 8, sl] = rolled[0:M - 8]
            dst[0, M - 8:M, sl] = jnp.where(row8 >= 8 - t_valid, new_r, rolled[M - 8:M])


def sample_attention(q_rot, k_new, v_new, k_buf, v_buf, t_valid, layer, depth, kv_all=None):
    N, TQ, _ = q_rot.shape
    M = k_buf.shape[2]
    TN_ = k_new.shape[1]
    assert t_valid <= 8 <= TQ
    blk = lambda r: pl.BlockSpec((1, r, 512), lambda n, j: (n, 0, j))
    buf = pl.BlockSpec((None, 1, M, 512), lambda n, j: (layer, n, 0, j))
    kv = jax.ShapeDtypeStruct((depth, N, M, MIX), F32)
    kv_spec = pl.BlockSpec((1, 1, M, 512), lambda n, j: (layer, n, 0, j))
    args = [q_rot, k_new, v_new, k_buf, v_buf]
    in_specs = [blk(TQ), blk(TN_), blk(TN_), buf, buf]
    aliases = {}
    if kv_all is not None:
        args += list(kv_all)
        in_specs += [pl.BlockSpec(memory_space=pl.ANY)] * 2
        aliases = {5: 1, 6: 2}
    return pl.pallas_call(
        functools.partial(_sample_attn_kernel, t_valid=t_valid),
        grid=(N, MIX // 512),
        in_specs=in_specs,
        out_specs=[blk(TQ), kv_spec, kv_spec],
        out_shape=[jax.ShapeDtypeStruct((N, TQ, MIX), BF16), kv, kv],
        input_output_aliases=aliases,
        compiler_params=_cparams(("parallel", "parallel")),
        name="sample_attention",
    )(*args)


def _glu_kernel(a_ref, g_ref, o_ref):
    o_ref[0] = a_ref[0] * _sigmoid(g_ref[0])


def glu(proj3, tt):
    N, T, _ = proj3.shape
    tt = min(tt, T)
    gw = 256
    assert C_CONV % gw == 0 and CONV_DIM % gw == 0
    ab = C_CONV // gw
    gb = (C_CONV + CONV_DIM) // gw
    return pl.pallas_call(
        _glu_kernel,
        grid=(N, T // tt, CONV_DIM // gw),
        in_specs=[pl.BlockSpec((1, tt, gw), lambda n, i, j: (n, i, ab + j)),
                  pl.BlockSpec((1, tt, gw), lambda n, i, j: (n, i, gb + j))],
        out_specs=pl.BlockSpec((1, tt, gw), lambda n, i, j: (n, i, j)),
        out_shape=jax.ShapeDtypeStruct((N, T, CONV_DIM), F32),
        compiler_params=_cparams(("parallel", "parallel", "parallel")),
        name="glu",
    )(proj3, proj3)


def _dwconv_kernel(u_ref, uh_ref, h0_ref, w_ref, b_ref, lw_ref, lb_ref, *rest, single_tile):
    o_ref, ext_ref, y_ref = rest[-3:]
    tt = u_ref.shape[1]
    i = pl.program_id(1)

    @pl.when(i == 0)
    def _():
        ext_ref[0:CONV_HALO, :] = h0_ref[0]

    if not single_tile:
        @pl.when(i > 0)
        def _():
            ext_ref[0:CONV_HALO, :] = uh_ref[0]

    ext_ref[CONV_HALO:, :] = u_ref[0]
    lead = CONV_HALO - (CONV_WIDTH - 1)
    rows = min(16, tt)
    for r0 in range(0, tt, rows):
        for c0 in range(0, CONV_DIM, 512):
            acc = jnp.broadcast_to(b_ref[:, c0:c0 + 512], (rows, 512))
            for j in range(CONV_WIDTH):
                acc = acc + ext_ref[r0 + lead + j:r0 + lead + j + rows, c0:c0 + 512] * w_ref[j:j + 1, c0:c0 + 512]
            y_ref[r0:r0 + rows, c0:c0 + 512] = acc
    y = y_ref[...]
    mu = jnp.mean(y, axis=-1, keepdims=True)
    var = jnp.mean(jnp.square(y - mu), axis=-1, keepdims=True)
    c = (y - mu) * lax.rsqrt(var + LN_EPS) * lw_ref[...] + lb_ref[...]
    o_ref[0] = (c * _sigmoid(c)).astype(BF16)


def dwconv_ln_silu(u, halo0, w, b, lw, lb, tt, mixed=None):
    N, T, C = u.shape
    tt = min(tt, T)
    hb = tt // CONV_HALO if tt >= CONV_HALO else 1
    in_specs = [pl.BlockSpec((1, tt, C), lambda n, i: (n, i, 0)),
                pl.BlockSpec((1, min(CONV_HALO, T), C), lambda n, i: (n, jnp.maximum(i * hb - 1, 0), 0)),
                pl.BlockSpec((1, CONV_HALO, C), lambda n, i: (n, 0, 0)),
                pl.BlockSpec((32, C), lambda n, i: (0, 0)),
                pl.BlockSpec((1, C), lambda n, i: (0, 0)),
                pl.BlockSpec((1, C), lambda n, i: (0, 0)),
                pl.BlockSpec((1, C), lambda n, i: (0, 0))]
    args = [u, u, halo0, w, b, lw, lb]
    if mixed is None:
        out_shape = jax.ShapeDtypeStruct((N, T, C), BF16)
        out_spec = pl.BlockSpec((1, tt, C), lambda n, i: (n, i, 0))
        aliases = {}
    else:
        in_specs.append(pl.BlockSpec(memory_space=pl.ANY))
        args.append(mixed)
        out_shape = jax.ShapeDtypeStruct(mixed.shape, mixed.dtype)
        out_spec = pl.BlockSpec((1, tt, C), lambda n, i: (n, i, M_CONV // C))
        aliases = {7: 0}
    return pl.pallas_call(
        functools.partial(_dwconv_kernel, single_tile=(T == tt)),
        grid=(N, T // tt),
        in_specs=in_specs,
        out_specs=out_spec,
        out_shape=out_shape,
        scratch_shapes=[pltpu.VMEM((CONV_HALO + tt, C), F32), pltpu.VMEM((tt, C), F32)],
        input_output_aliases=aliases,
        compiler_params=_cparams(("parallel", "arbitrary")),
        name="dwconv_ln_silu",
    )(*args)


def _ffn_conv_act(ext_ref, wg_ref, wv_ref, bg_ref, bv_ref, store, tt):
    lead = FFN_HALO - (FFN_CONV_WIDTH - 1)
    rows = min(32, tt)
    for r0 in range(0, tt, rows):
        ys = []
        for s, w_ref, b_ref in ((0, wg_ref, bg_ref), (1, wv_ref, bv_ref)):
            acc = jnp.broadcast_to(b_ref[...], (rows, b_ref.shape[1]))
            for j in range(FFN_CONV_WIDTH):
                acc = acc + ext_ref[s, r0 + lead + j:r0 + lead + j + rows, :] * w_ref[j:j + 1, :]
            ys.append(acc)
        gate, val = ys
        half = 0.5 * gate
        store(r0, rows, ((half + half * jnp.tanh(half)) * val).astype(BF16))


def _ffn_act_kernel(ug_ref, uv_ref, hg_ref, hv_ref, h0g_ref, h0v_ref, wg_ref, wv_ref, bg_ref, bv_ref, o_ref, ext_ref):
    tt = ug_ref.shape[1]
    i = pl.program_id(1)

    @pl.when(i == 0)
    def _():
        ext_ref[0, 0:FFN_HALO, :] = h0g_ref[0]
        ext_ref[1, 0:FFN_HALO, :] = h0v_ref[0]

    @pl.when(i > 0)
    def _():
        ext_ref[0, 0:FFN_HALO, :] = hg_ref[0]
        ext_ref[1, 0:FFN_HALO, :] = hv_ref[0]

    ext_ref[0, FFN_HALO:, :] = ug_ref[0]
    ext_ref[1, FFN_HALO:, :] = uv_ref[0]

    def store(r0, rows, val):
        o_ref[0, r0:r0 + rows, :] = val

    _ffn_conv_act(ext_ref, wg_ref, wv_ref, bg_ref, bv_ref, store, tt)


def ffn_act(u3, halo0, w, b, tt, tf):
    N, T, F2 = u3.shape
    F = F2 // 2
    tt = min(tt, T)
    nf = F // tf
    hb = tt // FFN_HALO
    g = lambda n, i, j: (n, i, j)
    v = lambda n, i, j: (n, i, j + nf)
    hg = lambda n, i, j: (n, jnp.maximum(i * hb - 1, 0), j)
    hv = lambda n, i, j: (n, jnp.maximum(i * hb - 1, 0), j + nf)
    return pl.pallas_call(
        _ffn_act_kernel,
        grid=(N, T // tt, nf),
        in_specs=[pl.BlockSpec((1, tt, tf), g), pl.BlockSpec((1, tt, tf), v),
                  pl.BlockSpec((1, FFN_HALO, tf), hg), pl.BlockSpec((1, FFN_HALO, tf), hv),
                  pl.BlockSpec((1, FFN_HALO, tf), lambda n, i, j: (n, 0, j)),
                  pl.BlockSpec((1, FFN_HALO, tf), lambda n, i, j: (n, 0, j + nf)),
                  pl.BlockSpec((8, tf), lambda n, i, j: (0, j)), pl.BlockSpec((8, tf), lambda n, i, j: (0, j + nf)),
                  pl.BlockSpec((1, tf), lambda n, i, j: (0, j)), pl.BlockSpec((1, tf), lambda n, i, j: (0, j + nf))],
        out_specs=pl.BlockSpec((1, tt, tf), g),
        out_shape=jax.ShapeDtypeStruct((N, T, F), BF16),
        scratch_shapes=[pltpu.VMEM((2, FFN_HALO + tt, tf), F32)],
        compiler_params=_cparams(("parallel", "parallel", "parallel")),
        name="ffn_act",
    )(u3, u3, u3, u3, halo0, halo0, w, w, b, b)


def _ffn_fused_kernel(x_ref, g_ref, wg_ref, wv_ref, wd_ref, cwg_ref, cwv_ref, cbg_ref, cbv_ref, h0g_ref, h0v_ref,
                      o_ref, st_ref, h_ref, ext_ref, carry_ref, act_ref, *, tiles_per_seq):
    i = pl.program_id(0)
    j = pl.program_id(1)
    tm = x_ref.shape[0]

    @pl.when(j == 0)
    def _():
        x = x_ref[...]
        ms = jnp.mean(x * x, axis=-1, keepdims=True)
        h_ref[...] = (x * lax.rsqrt(ms + RMS_EPS) * g_ref[...]).astype(BF16)
        o_ref[...] = x

    h = h_ref[...]
    ug = jnp.dot(h, wg_ref[...], preferred_element_type=F32)
    uv = jnp.dot(h, wv_ref[...], preferred_element_type=F32)
    seq_start = (i % tiles_per_seq) == 0
    ext_ref[0, 0:FFN_HALO, :] = jnp.where(seq_start, h0g_ref[0], carry_ref[j, 0])
    ext_ref[1, 0:FFN_HALO, :] = jnp.where(seq_start, h0v_ref[0], carry_ref[j, 1])
    ext_ref[0, FFN_HALO:, :] = ug
    ext_ref[1, FFN_HALO:, :] = uv
    carry_ref[j, 0] = ug[tm - FFN_HALO:, :]
    carry_ref[j, 1] = uv[tm - FFN_HALO:, :]

    def store(r0, rows, val):
        act_ref[r0:r0 + rows, :] = val

    _ffn_conv_act(ext_ref, cwg_ref, cwv_ref, cbg_ref, cbv_ref, store, tm)
    o_ref[...] += jnp.dot(act_ref[...], wd_ref[...], preferred_element_type=F32)

    @pl.when((j == pl.num_programs(1) - 1) & (i % tiles_per_seq == tiles_per_seq - 1))
    def _():
        st_ref[0] = carry_ref[...]


def ffn_fused(x, g, w_up, w_down, layer, cw, cb, halo0, seq_len, tm, tf):
    M, D = x.shape
    F = w_down.shape[1]
    assert M % tm == 0 and seq_len % tm == 0 and F % tf == 0
    nf = F // tf
    tps = seq_len // tm
    gate = lambda i, j: (0, j)
    val = lambda i, j: (0, j + nf)
    return pl.pallas_call(
        functools.partial(_ffn_fused_kernel, tiles_per_seq=tps),
        grid=(M // tm, nf),
        in_specs=[pl.BlockSpec((tm, D), lambda i, j: (i, 0), pipeline_mode=pl.Buffered(1)),
                  pl.BlockSpec((1, D), lambda i, j: (0, 0)),
                  pl.BlockSpec((None, D, tf), lambda i, j: (layer, 0, j)),
                  pl.BlockSpec((None, D, tf), lambda i, j: (layer, 0, j + nf)),
                  pl.BlockSpec((None, tf, D), lambda i, j: (layer, j, 0)),
                  pl.BlockSpec((8, tf), gate), pl.BlockSpec((8, tf), val),
                  pl.BlockSpec((1, tf), gate), pl.BlockSpec((1, tf), val),
                  pl.BlockSpec((1, FFN_HALO, tf), lambda i, j: (i // tps, 0, j)),
                  pl.BlockSpec((1, FFN_HALO, tf), lambda i, j: (i // tps, 0, j + nf))],
        out_specs=[pl.BlockSpec((tm, D), lambda i, j: (i, 0)),
                   pl.BlockSpec((1, nf, 2, FFN_HALO, tf), lambda i, j: (i // tps, 0, 0, 0, 0))],
        out_shape=[jax.ShapeDtypeStruct((M, D), F32),
                   jax.ShapeDtypeStruct((M // seq_len, nf, 2, FFN_HALO, tf), F32)],
        scratch_shapes=[pltpu.VMEM((tm, D), BF16), pltpu.VMEM((2, FFN_HALO + tm, tf), F32),
                        pltpu.VMEM((nf, 2, FFN_HALO, tf), F32), pltpu.VMEM((tm, tf), BF16)],
        compiler_params=_cparams(("arbitrary", "arbitrary")),
        name="ffn_fused",
    )(x, g, w_up, w_up, w_down, cw, cw, cb, cb, halo0, halo0)


def _rwkv_kernel(r_ref, k_ref, v_ref, lo_ref,
                 sr_ref, sk_ref, sv_ref, slo_ref,
                 mur_ref, muk_ref, muv_ref, mulo_ref,
                 w0_ref, a0_ref, kk_ref, ka_ref, rk_ref, lnw_ref, lnb_ref,
                 w2_ref, a2_ref, g2_ref, s0_ref,
                 o_ref, so_ref,
                 S_ref, pr_ref, pk_ref, pv_ref, plo_ref, *, t_valid, t_total):
    c = pl.program_id(2)
    Tc = r_ref.shape[1]
    PW = 2 * HEAD_DIM
    n_pairs = r_ref.shape[2] // PW

    @pl.when(c == 0)
    def _():
        S_ref[...] = s0_ref[0]
        pr_ref[...] = sr_ref[0]
        pk_ref[...] = sk_ref[0]
        pv_ref[...] = sv_ref[0]
        plo_ref[...] = slo_ref[0]

    def lerp(p_ref, prev_ref, mu_ref):
        p = p_ref[0]
        row = lax.broadcasted_iota(jnp.int32, p.shape, 0)
        shifted = jnp.where(row == 0, prev_ref[...], pltpu.roll(p, 1, 0))
        prev_ref[...] = p[Tc - 1:Tc, :]
        return p + (shifted - p) * mu_ref[...]

    xr = lerp(r_ref, pr_ref, mur_ref)
    xk = lerp(k_ref, pk_ref, muk_ref)
    xv = lerp(v_ref, pv_ref, muv_ref)
    xlo = lerp(lo_ref, plo_ref, mulo_ref)
    xw = xlo[:, :DECAY_LORA]
    xa = xlo[:, DECAY_LORA:DECAY_LORA + AAA_LORA]
    xg = xlo[:, DECAY_LORA + AAA_LORA:]

    h0p = _head0_mask((Tc, PW))

    def head_sum(x):
        parts = []
        for p in range(n_pairs):
            xp = x[:, PW * p:PW * (p + 1)]
            s_0 = jnp.sum(jnp.where(h0p, xp, 0.0), axis=-1, keepdims=True)
            s_1 = jnp.sum(jnp.where(h0p, 0.0, xp), axis=-1, keepdims=True)
            parts.append(jnp.where(h0p, s_0, s_1))
        return parts[0] if n_pairs == 1 else jnp.concatenate(parts, axis=1)

    def mm(x, y_):
        return jnp.dot(x, y_, preferred_element_type=F32)

    z = w0_ref[...] + mm(jnp.tanh(xw).astype(BF16), w2_ref[...])
    w_log = jnp.minimum(z, 0.0) - jnp.log(1.0 + jnp.exp(-jnp.abs(z))) - 0.5
    ld = -jnp.exp(w_log)
    a = _sigmoid(a0_ref[...] + mm(xa.astype(BF16), a2_ref[...]))
    g = mm(_sigmoid(xg).astype(BF16), g2_ref[...])

    kk = xk * kk_ref[...]
    kk = kk * lax.rsqrt(jnp.maximum(head_sum(kk * kk), 1e-24))
    kmod = xk * (1.0 + (a - 1.0) * ka_ref[...])
    avec = -kk
    bvec = kk * a
    bonus = head_sum(xr * kmod * rk_ref[...]) * xv
    vval = xv

    if t_valid < t_total:
        row = lax.broadcasted_iota(jnp.int32, (Tc, n_pairs * PW), 0) + c * Tc
        ok = row < t_valid
        ld = jnp.where(ok, ld, 0.0)
        avec = jnp.where(ok, avec, 0.0)
        bvec = jnp.where(ok, bvec, 0.0)
        kmod_s = jnp.where(ok, kmod, 0.0)
        vval = jnp.where(ok, vval, 0.0)
    else:
        kmod_s = kmod

    C = CHUNK
    tri_r = lax.broadcasted_iota(jnp.int32, (C, C), 0)
    tri_c = lax.broadcasted_iota(jnp.int32, (C, C), 1)
    tril = jnp.where(tri_r >= tri_c, 1.0, 0.0).astype(F32)
    it = lax.broadcasted_iota(jnp.int32, (2 * C, 2 * C), 0) % C
    js = lax.broadcasted_iota(jnp.int32, (2 * C, 2 * C), 1) % C
    strict = it > js
    incl = it >= js
    h0c = _head0_mask((C, 2 * HEAD_DIM))

    def stack(x):
        return jnp.concatenate([jnp.where(h0c, x, 0.0), jnp.where(h0c, 0.0, x)], axis=0)

    n_chunks = Tc // C
    grp = 2 if n_chunks % 2 == 0 else 1
    gw = 2 * C * grp
    eye_g = jnp.where(lax.broadcasted_iota(jnp.int32, (gw, gw), 0)
                      == lax.broadcasted_iota(jnp.int32, (gw, gw), 1), 1.0, 0.0).astype(F32)
    zero_blk = jnp.zeros((2 * C, 2 * C), F32)

    def bdiag(blocks):
        if len(blocks) == 1:
            return blocks[0]
        return jnp.concatenate([jnp.concatenate([blocks[0], zero_blk], axis=1),
                                jnp.concatenate([zero_blk, blocks[1]], axis=1)], axis=0)

    def cat(blocks):
        return blocks[0] if len(blocks) == 1 else jnp.concatenate(blocks, axis=0)

    a16, r32, bb16, kb16, v16, gts, Ls, AKs, RBs, RKs = ({} for _ in range(10))
    units = [(p, j) for j in range(n_chunks) for p in range(n_pairs)]
    for p, j in units:
        sl = (slice(C * j, C * (j + 1)), slice(PW * p, PW * (p + 1)))
        ldc = ld[sl]
        cs = jnp.dot(tril, ldc, precision=lax.Precision.HIGHEST, preferred_element_type=F32)
        tot = cs[C - 1:C, :]
        eg = jnp.exp(cs)
        egi = jnp.exp(-cs)
        ege = jnp.exp(cs - ldc)
        et = jnp.exp(tot - cs)
        gts[p, j] = jnp.exp(tot)
        a_s = stack(avec[sl] * ege).astype(BF16)
        r_s = stack(xr[sl] * eg)
        b_s = stack(bvec[sl] * egi).astype(BF16)
        k_s = stack(kmod_s[sl] * egi).astype(BF16)
        sc = lax.dot_general(jnp.concatenate([a_s, r_s.astype(BF16)], axis=0), jnp.concatenate([b_s, k_s], axis=0), NT,
                             preferred_element_type=F32)
        a16[p, j] = a_s
        r32[p, j] = r_s
        bb16[p, j] = stack(bvec[sl] * et).astype(BF16)
        kb16[p, j] = stack(kmod_s[sl] * et).astype(BF16)
        v16[p, j] = stack(vval[sl]).astype(BF16)
        Ls[p, j] = jnp.where(strict, sc[:2 * C, :2 * C], 0.0)
        AKs[p, j] = jnp.where(strict, sc[:2 * C, 2 * C:], 0.0)
        RBs[p, j] = jnp.where(incl, sc[2 * C:, :2 * C], 0.0)
        RKs[p, j] = jnp.where(incl, sc[2 * C:, 2 * C:], 0.0)

    groups = [[(p, j) for j in range(g0, g0 + grp)] for g0 in range(0, n_chunks, grp) for p in range(n_pairs)]
    Ps = [bdiag([Ls[c_] for c_ in g_]) for g_ in groups]
    Tms = [eye_g + p_ for p_ in Ps]
    for _ in range(C.bit_length() - 2):
        for gi in range(len(groups)):
            P16 = Ps[gi].astype(BF16)
            Ps[gi] = mm(P16, P16)
            Tms[gi] = Tms[gi] + mm(Tms[gi].astype(BF16), Ps[gi].astype(BF16))

    RAs, YNs, Gs, Ns = {}, {}, {}, {}
    for gi, g_ in enumerate(groups):
        T16 = Tms[gi].astype(BF16)
        A2 = cat([a16[c_] for c_ in g_])
        V2 = cat([v16[c_] for c_ in g_])
        R2 = cat([r32[c_] for c_ in g_])
        RB16 = bdiag([RBs[c_] for c_ in g_]).astype(BF16)
        RK16 = bdiag([RKs[c_] for c_ in g_]).astype(BF16)
        TA16 = mm(T16, A2).astype(BF16)
        TV16 = mm(T16, mm(bdiag([AKs[c_] for c_ in g_]).astype(BF16), V2).astype(BF16)).astype(BF16)
        RA = R2 + mm(RB16, TA16)
        YN = mm(jnp.concatenate([RB16, RK16], axis=1), jnp.concatenate([TV16, V2], axis=0))
        for q, c_ in enumerate(g_):
            rs = slice(2 * C * q, 2 * C * (q + 1))
            RAs[c_] = RA[rs].astype(BF16)
            YNs[c_] = YN[rs]
            Gs[c_] = lax.dot_general(TA16[rs], bb16[c_], TN, preferred_element_type=F32).astype(BF16)
            Ns[c_] = lax.dot_general(jnp.concatenate([TV16[rs], v16[c_]], axis=0),
                                     jnp.concatenate([bb16[c_], kb16[c_]], axis=0), TN, preferred_element_type=F32)

    ys = {}
    Ss = [S_ref[p] for p in range(n_pairs)]
    for p, j in units:
        S16 = Ss[p].astype(BF16)
        Y = lax.dot_general(RAs[p, j], S16, NT, preferred_element_type=F32) + YNs[p, j]
        ys[p, j] = Y[:C] + Y[C:]
        Ss[p] = Ss[p] * gts[p, j] + mm(S16, Gs[p, j]) + Ns[p, j]
    for p in range(n_pairs):
        S_ref[p] = Ss[p]
    y_pairs = [cat([ys[p, j] for j in range(n_chunks)]) for p in range(n_pairs)]
    y = y_pairs[0] if n_pairs == 1 else jnp.concatenate(y_pairs, axis=1)

    mu = head_sum(y) * (1.0 / HEAD_DIM)
    var = head_sum(jnp.square(y - mu)) * (1.0 / HEAD_DIM)
    yn = (y - mu) * lax.rsqrt(var + GN_EPS) * lnw_ref[...] + lnb_ref[...]
    o_ref[0] = ((yn + bonus) * g).astype(BF16)

    @pl.when(c == pl.num_programs(2) - 1)
    def _():
        for p in range(n_pairs):
            so_ref[0, p] = Ss[p]


def rwkv_time_mix(proj3, shift0, s0, prm, t_valid, tc, out_width=MIX, pairs=2):
    N, T, _ = proj3.shape
    tc = min(tc, T)
    W = 2 * HEAD_DIM * pairs
    assert T % tc == 0 and tc % CHUNK == 0 and MIX % W == 0 and C_RWKV % W == 0
    rb = C_RWKV // W
    nb = MIX // W
    pspec = lambda w, off: pl.BlockSpec((1, tc, w), lambda n, h, c: (n, c, off(h)))
    sspec = lambda w, off: pl.BlockSpec((1, 1, w), lambda n, h, c: (n, 0, off(h)))
    mspec = lambda w, off: pl.BlockSpec((1, w), lambda n, h, c: (0, off(h)))
    hspec = pl.BlockSpec((1, W), lambda n, h, c: (0, h))
    assert (C_RWKV + 3 * MIX) % LORA_IN == 0
    lo_p = (C_RWKV + 3 * MIX) // LORA_IN
    lo_s = 3 * MIX // LORA_IN
    in_specs = [
        pspec(W, lambda h: rb + h), pspec(W, lambda h: rb + nb + h), pspec(W, lambda h: rb + 2 * nb + h),
        pspec(LORA_IN, lambda h: lo_p),
        sspec(W, lambda h: h), sspec(W, lambda h: nb + h), sspec(W, lambda h: 2 * nb + h),
        sspec(LORA_IN, lambda h: lo_s),
        mspec(W, lambda h: h), mspec(W, lambda h: nb + h), mspec(W, lambda h: 2 * nb + h),
        mspec(LORA_IN, lambda h: lo_s),
        hspec, hspec, hspec, hspec, hspec, hspec, hspec,
        pl.BlockSpec((DECAY_LORA, W), lambda n, h, c: (0, h)),
        pl.BlockSpec((AAA_LORA, W), lambda n, h, c: (0, h)),
        pl.BlockSpec((GATE_PAD, W), lambda n, h, c: (0, h)),
        pl.BlockSpec((1, pairs, 128, 128), lambda n, h, c: (n, h, 0, 0)),
    ]
    args = ([proj3] * 4 + [shift0] * 4 + [prm['mu']] * 4
            + [prm['w0'], prm['a0'], prm['k_k'], prm['k_a'], prm['r_k'], prm['ln_w'], prm['ln_b'],
               prm['w2'], prm['a2'], prm['g2'], s0])
    return pl.pallas_call(
        functools.partial(_rwkv_kernel, t_valid=t_valid, t_total=T),
        grid=(N, nb, T // tc),
        in_specs=in_specs,
        out_specs=[pl.BlockSpec((1, tc, W), lambda n, h, c: (n, c, M_RWKV // W + h)),
                   pl.BlockSpec((1, pairs, 128, 128), lambda n, h, c: (n, h, 0, 0))],
        out_shape=[jax.ShapeDtypeStruct((N, T, out_width), BF16),
                   jax.ShapeDtypeStruct((N, HEADS // 2, 128, 128), F32)],
        scratch_shapes=[pltpu.VMEM((pairs, 128, 128), F32), pltpu.VMEM((1, W), F32), pltpu.VMEM((1, W), F32),
                        pltpu.VMEM((1, W), F32), pltpu.VMEM((1, LORA_IN), F32)],
        compiler_params=_cparams(("parallel", "parallel", "arbitrary")),
        name="rwkv7_time_mix",
    )(*args)


def _pad_rwkv_cols(t):
    return jnp.pad(t, [(0, 0)] * (t.ndim - 1) + [(0, RWKV_PAD - RWKV_PROJ)])


PREP_ROWS = 512
PREP_COLS = 1024


def _prep_w_in_kernel(cur_ref, prev_ref, o_ref):
    j = pl.program_id(2)
    cur = cur_ref[...].astype(F32)
    prev = prev_ref[...].astype(F32)
    lane = lax.broadcasted_iota(jnp.int32, cur.shape, 1)
    col = lane + j * PREP_COLS
    shift = RWKV_PAD - RWKV_PROJ
    shifted = jnp.where(lane < shift, pltpu.roll(prev, shift, 1), pltpu.roll(cur, shift, 1))
    in_tail = (col >= RWKV_PAD) & (col < C_CONV + CONV_PROJ)
    o_ref[...] = jnp.where(col < RWKV_PROJ, cur, jnp.where(in_tail, shifted, 0.0)).astype(BF16)


def prep_w_in(w16):
    depth, K, n = w16.shape
    assert K % PREP_ROWS == 0 and NP % PREP_COLS == 0 and pl.cdiv(n, PREP_COLS) == NP // PREP_COLS
    assert n + RWKV_PAD - RWKV_PROJ == C_CONV + CONV_PROJ
    blk = (None, PREP_ROWS, PREP_COLS)
    return pl.pallas_call(
        _prep_w_in_kernel,
        grid=(depth, K // PREP_ROWS, NP // PREP_COLS),
        in_specs=[pl.BlockSpec(blk, lambda l, i, j: (l, i, j)),
                  pl.BlockSpec(blk, lambda l, i, j: (l, i, jnp.maximum(j - 1, 0)))],
        out_specs=pl.BlockSpec(blk, lambda l, i, j: (l, i, j)),
        out_shape=jax.ShapeDtypeStruct((depth, K, NP), BF16),
        compiler_params=_cparams(("parallel", "parallel", "parallel")),
        name="prep_w_in",
    )(w16, w16)


def _matmul_weights(w_in, w_out, w_up, w_down):
    return {'w_in': prep_w_in(w_in.astype(BF16)), 'w_out': w_out.astype(BF16), 'w_up': w_up.astype(BF16), 'w_down': w_down.astype(BF16)}


def _layer_params(l, big, norm_mix, rwkv_mu, rwkv_w0, rwkv_w2, rwkv_a0, rwkv_a2, rwkv_g2, rwkv_k_k, rwkv_k_a,
                  rwkv_r_k, rwkv_ln_w, rwkv_ln_b, conv_w, conv_b, conv_ln_w, conv_ln_b, norm_ffn,
                  ffn_conv_w, ffn_conv_b):
    row = lambda t: t.reshape(1, -1)
    return {
        'w_in': big['w_in'], 'w_out': big['w_out'], 'w_up': big['w_up'], 'w_down': big['w_down'],
        'norm_mix': row(norm_mix[l]),
        'mu': _pad_rwkv_cols(row(rwkv_mu[l])),
        'w0': row(rwkv_w0[l]), 'a0': row(rwkv_a0[l]), 'k_k': row(rwkv_k_k[l]), 'k_a': row(rwkv_k_a[l]),
        'r_k': row(rwkv_r_k[l]), 'ln_w': row(rwkv_ln_w[l]), 'ln_b': row(rwkv_ln_b[l]),
        'w2': rwkv_w2[l].astype(BF16), 'a2': rwkv_a2[l].astype(BF16),
        'g2': jnp.pad(rwkv_g2[l], ((0, GATE_PAD - GATE_LORA), (0, 0))).astype(BF16),
        'conv_w': jnp.pad(conv_w[l], ((0, 32 - CONV_WIDTH), (0, 0))), 'conv_b': row(conv_b[l]),
        'conv_ln_w': row(conv_ln_w[l]), 'conv_ln_b': row(conv_ln_b[l]),
        'norm_ffn': row(norm_ffn[l]),
        'ffn_conv_w': jnp.pad(ffn_conv_w[l], ((0, 8 - FFN_CONV_WIDTH), (0, 0))), 'ffn_conv_b': row(ffn_conv_b[l]),
    }


def _rope_tables(pos):
    half = HEAD_DIM // 2
    inv_freq = ROPE_THETA ** (-jnp.arange(half, dtype=F32) * 2.0 / HEAD_DIM)
    ang = pos.astype(F32)[..., None] * inv_freq
    cos, sin = jnp.cos(ang), jnp.sin(ang)
    return jnp.concatenate([cos, cos, cos, cos], axis=-1), jnp.concatenate([-sin, sin, -sin, sin], axis=-1)


def _state_to_blockdiag(s):
    N = s.shape[0]
    s = s.reshape(N, HEADS // 2, 2, HEAD_DIM, HEAD_DIM)
    z = jnp.zeros_like(s[:, :, 0])
    top = jnp.concatenate([s[:, :, 0], z], axis=-1)
    bot = jnp.concatenate([z, s[:, :, 1]], axis=-1)
    return jnp.concatenate([top, bot], axis=-2)


def _blockdiag_to_state(s):
    N = s.shape[0]
    return jnp.stack([s[:, :, :HEAD_DIM, :HEAD_DIM], s[:, :, HEAD_DIM:, HEAD_DIM:]], axis=2).reshape(
        N, HEADS, HEAD_DIM, HEAD_DIM)


def _last_rows(buf, u, t_valid):
    keep = buf.shape[1]
    if t_valid >= keep:
        return u[:, t_valid - keep:t_valid]
    return jnp.concatenate([buf[:, t_valid:], u[:, :t_valid]], axis=1)


def _front_pad(buf, rows):
    return jnp.pad(buf, ((0, 0), (rows - buf.shape[1], 0), (0, 0)))


def _prompt_layer(x2, N, S, prm, cos_t, sin_t, layer, depth, kv_all):
    f32 = x2.dtype
    proj3 = norm_matmul(x2, prm['norm_mix'], prm['w_in'], layer, 1024, 512, single_buffer_x=True).reshape(N, S, NP)

    mixed, s_new = rwkv_time_mix(proj3, jnp.zeros((N, 1, RWKV_PAD), f32), jnp.zeros((N, HEADS // 2, 128, 128), f32),
                                 prm, S, 256, out_width=D_MODEL, pairs=4)
    shift_new = proj3[:, S - 1:S, C_RWKV:C_RWKV + RWKV_PROJ]

    mixed, k_all, v_all = prompt_attention(proj3, cos_t, sin_t, mixed, layer, depth, kv_all)

    u = glu(proj3, 1024)
    mixed = dwconv_ln_silu(u, jnp.zeros((N, CONV_HALO, CONV_DIM), f32), prm['conv_w'], prm['conv_b'],
                           prm['conv_ln_w'], prm['conv_ln_b'], 128, mixed=mixed)
    conv_new = u[:, S - (CONV_WIDTH - 1):]

    x2 = matmul_res(mixed.reshape(N * S, D_MODEL), prm['w_out'], layer, x2, 512, 512, D_MODEL)

    x2, tails = ffn_fused(x2, prm['norm_ffn'], prm['w_up'], prm['w_down'], layer, prm['ffn_conv_w'], prm['ffn_conv_b'],
                          jnp.zeros((N, FFN_HALO, 2 * D_FF), f32), S, 512, 256)
    keep = FFN_CONV_WIDTH - 1
    ffn_new = jnp.transpose(tails[:, :, :, FFN_HALO - keep:, :], (0, 3, 2, 1, 4)).reshape(N, keep, 2 * D_FF)
    return x2, (shift_new, _blockdiag_to_state(s_new), conv_new, ffn_new), (k_all, v_all)


def _sample_layer(x2, N, T, t_valid, prm, cos_t, sin_t, carry, attn_bufs, layer, depth, kv_all):
    shift0, wkv0, conv_buf, ffn_buf = carry
    proj3 = norm_matmul(x2, prm['norm_mix'], prm['w_in'], layer, 512, 512).reshape(N, T, NP)

    proj_r = jnp.pad(proj3, ((0, 0), (0, CHUNK - T), (0, 0)))
    o_rwkv, s_new = rwkv_time_mix(proj_r, _pad_rwkv_cols(shift0), _state_to_blockdiag(wkv0), prm, t_valid, CHUNK, pairs=4)
    o_rwkv = o_rwkv[:, :T]
    shift_new = proj3[:, t_valid - 1:t_valid, C_RWKV:C_RWKV + RWKV_PROJ]

    q_rot, k_rot = rope(proj3, cos_t, sin_t, 256)
    v_new = proj3[:, :, C_ATT + 2 * MIX:C_ATT + 3 * MIX]
    padr = ((0, 0), (0, 128 - T), (0, 0))
    o_att, k_all, v_all = sample_attention(q_rot, jnp.pad(k_rot, padr), jnp.pad(v_new, padr),
                                           attn_bufs[0], attn_bufs[1], t_valid, layer, depth, kv_all)

    u = glu(proj3, 256)
    o_conv = dwconv_ln_silu(u, _front_pad(conv_buf, CONV_HALO), prm['conv_w'], prm['conv_b'],
                            prm['conv_ln_w'], prm['conv_ln_b'], 128)
    conv_new = _last_rows(conv_buf, u, t_valid)

    mixed = jnp.concatenate([o_rwkv, o_att, o_conv], axis=-1).reshape(N * T, D_MODEL)
    x2 = matmul_res(mixed, prm['w_out'], layer, x2, 512, 512, D_MODEL)

    uf3 = norm_matmul(x2, prm['norm_ffn'], prm['w_up'], layer, 512, 512).reshape(N, T, 2 * D_FF)
    act = ffn_act(uf3, _front_pad(ffn_buf, FFN_HALO), prm['ffn_conv_w'], prm['ffn_conv_b'], 512, D_FF // 2)
    ffn_new = _last_rows(ffn_buf, uf3, t_valid)
    x2 = matmul_res(act.reshape(N * T, D_FF), prm['w_down'], layer, x2, 512, 512, D_FF // 2)
    return x2, (shift_new, _blockdiag_to_state(s_new), conv_new, ffn_new), (k_all, v_all)


def kernel(x_prompt, x_sample, state_rwkv_shift, state_rwkv_wkv, state_attn_k, state_attn_v, state_conv, state_ffn_conv, pos_sample, norm_mix, w_in, rwkv_mu, rwkv_w0, rwkv_w2, rwkv_a0, rwkv_a2, rwkv_g2, rwkv_k_k, rwkv_k_a, rwkv_r_k, rwkv_ln_w, rwkv_ln_b, conv_w, conv_b, conv_ln_w, conv_ln_b, w_out, norm_ffn, w_up, ffn_conv_w, ffn_conv_b, w_down, norm_final):
    B, S, _ = x_prompt.shape
    NB, TS, _ = x_sample.shape
    depth = w_in.shape[0]
    win_buf = state_attn_k.shape[2]
    assert win_buf == S, "prompt key/value state is the whole rotated sequence"
    TSP = 8

    xp = x_prompt.reshape(B * S, D_MODEL)
    xs = jnp.pad(x_sample, ((0, 0), (0, TSP - TS), (0, 0))).reshape(NB * TSP, D_MODEL)
    cos_p, sin_p = _rope_tables(jnp.arange(S, dtype=jnp.int32)[None])
    cos_s, sin_s = _rope_tables(jnp.pad(pos_sample, ((0, 0), (0, TSP - TS))))
    new_p, new_s = [], []
    kv_p, kv_s = None, None
    big = _matmul_weights(w_in, w_out, w_up, w_down)
    bufs = (state_attn_k.reshape(depth, NB, win_buf, MIX), state_attn_v.reshape(depth, NB, win_buf, MIX))
    for l in range(depth):
        prm = _layer_params(l, big, norm_mix, rwkv_mu, rwkv_w0, rwkv_w2, rwkv_a0, rwkv_a2, rwkv_g2, rwkv_k_k,
                            rwkv_k_a, rwkv_r_k, rwkv_ln_w, rwkv_ln_b, conv_w, conv_b, conv_ln_w, conv_ln_b,
                            norm_ffn, ffn_conv_w, ffn_conv_b)
        xp, st_p, kv_p = _prompt_layer(xp, B, S, prm, cos_p, sin_p, l, depth, kv_p)
        carry_s = (state_rwkv_shift[l], state_rwkv_wkv[l], state_conv[l], state_ffn_conv[l])
        xs, st_s, kv_s = _sample_layer(xs, NB, TSP, TS, prm, cos_s, sin_s, carry_s, bufs, l, depth, kv_s)
        new_p.append(st_p)
        new_s.append(st_s)

    g = norm_final.reshape(1, D_MODEL)
    y_prompt = rmsnorm(xp, g, 512).reshape(B, S, D_MODEL)
    y_sample = rmsnorm(xs, g, 512).reshape(NB, TSP, D_MODEL)[:, :TS]

    def stack(states, i):
        return jnp.stack([st[i] for st in states], axis=0)

    heads_p = (depth, B, win_buf, HEADS, HEAD_DIM)
    heads_s = (depth, NB, win_buf, HEADS, HEAD_DIM)
    return (y_prompt, y_sample,
            stack(new_p, 0), stack(new_s, 0), stack(new_p, 1), stack(new_s, 1),
            kv_p[0].reshape(heads_p), kv_s[0].reshape(heads_s), kv_p[1].reshape(heads_p), kv_s[1].reshape(heads_s),
            stack(new_p, 2), stack(new_s, 2), stack(new_p, 3), stack(new_s, 3))
```
